```python
import math
import jax
import jax.numpy as jnp
from jax import lax
import numpy as np

D_MODEL = 1024
BATCH = 8
SEQ = 4096
DEPTH = 1

GRID_W = 64
CTX_LEN = 256
HY_WIDTH = 512
HY_ORDER = 2
HY_COLS = (HY_ORDER + 1) * HY_WIDTH
S5_WIDTH = 512
S5_GROUP = 16
S5_GROUPS = S5_WIDTH // S5_GROUP
S5_STATE = 64
MIX_WIDTH = HY_WIDTH + S5_WIDTH
IN_COLS = HY_COLS + S5_WIDTH
SHORT_CONV = 3
POS_EMB_DIM = 33
FILTER_WIDTH = 64
FILTER_OUT = HY_ORDER * 2 * HY_WIDTH
DECAY_FAST = 0.3
DECAY_SLOW = 1.5
DECAY_TARGET = 1e-2
N_EXPERTS = 16
EXPERT_FF = 2816
CAPACITY_FACTOR = 2
N_MOD = 6
NORM_EPS = 1e-6

kernel_name = "hybrid_hyena_s5_ec_dit_block"


def rmsnorm(x, g):
    xf = x.astype(jnp.float32)
    y = xf * lax.rsqrt(jnp.mean(xf * xf, axis=-1, keepdims=True) + NORM_EPS)
    return (y * g.astype(jnp.float32)).astype(x.dtype)


def adaln(cond, w, b):
    return jnp.split(jax.nn.silu(cond) @ w + b, N_MOD, axis=-1)


def modulate(h, shift, scale):
    return h * (1.0 + scale[:, None]) + shift[:, None]


def short_conv_rows(z, rows, w, b):
    bsz, length, ch = z.shape
    zr = z.reshape(bsz, rows, length // rows, ch)
    p = jnp.pad(zr, ((0, 0), (0, 0), (1, 1), (0, 0)))
    y = p[:, :, :-2] * w[0] + p[:, :, 1:-1] * w[1] + p[:, :, 2:] * w[2] + b
    return y.reshape(bsz, length, ch)


def hyena_filter_spectra(length, w1, b1, freq, w2, b2, w3, b3):
    f32 = jnp.float32
    w1, b1, freq, w2, b2, w3, b3 = [a.astype(f32) for a in (w1, b1, freq, w2, b2, w3, b3)]
    t = jnp.linspace(0.0, 1.0, length, dtype=f32)[:, None]
    bands = (POS_EMB_DIM - 1) // 2
    omega = 2.0 * math.pi * jnp.arange(length, dtype=f32)[:, None] / length
    f = jnp.linspace(1e-4, bands - 1, bands, dtype=f32)[None, :]
    feats = jnp.concatenate([t, jnp.cos(f * omega), -jnp.sin(f * omega)], axis=-1)
    h = jnp.sin(freq[0] * (feats @ w1 + b1))
    h = jnp.sin(freq[1] * (h @ w2 + b2))
    h = (h @ w3 + b3).reshape(length, HY_ORDER, 2, HY_WIDTH)
    deltas = jnp.abs(jnp.linspace(math.log(DECAY_TARGET) / DECAY_FAST,
                                  math.log(DECAY_TARGET) / DECAY_SLOW, HY_WIDTH, dtype=f32))
    h = h * jnp.exp(-t * deltas)[:, None, None, :]
    fwd, bwd = h[:, :, 0], h[:, :, 1]
    k = jnp.concatenate([fwd, jnp.zeros((1,) + fwd.shape[1:], f32), bwd[:0:-1]], axis=0)
    k = k / jnp.sum(jnp.abs(k), axis=0, keepdims=True)
    return jnp.fft.rfft(k, axis=0)


def long_conv(u, kf, bias):
    length = u.shape[1]
    uf = jnp.fft.rfft(u, n=2 * length, axis=1)
    y = jnp.fft.irfft(uf * kf[None], n=2 * length, axis=1)[:, :length]
    return y + u * bias.astype(jnp.float32)


def hyena_branch(z, rows, conv_w, conv_b, filt, hy_bias):
    length = z.shape[1]
    z = short_conv_rows(z, rows, conv_w, conv_b).astype(jnp.float32)
    v, x1, x2 = jnp.split(z, 3, axis=-1)
    kf = hyena_filter_spectra(length, *filt)
    y = x1 * long_conv(v, kf[:, 0], hy_bias[0])
    return x2 * long_conv(y, kf[:, 1], hy_bias[1])


def _diag_binop(e1, e2):
    a1, b1 = e1
    a2, b2 = e2
    return a1 * a2, a2 * b1 + b2


def s5_discretize(lam_re, lam_im, log_step, b_re, b_im):
    f32 = jnp.float32
    lam = lax.complex(lam_re.astype(f32), lam_im.astype(f32))
    lam_bar = jnp.exp(lam * jnp.exp(log_step.astype(f32))[:, None])
    b_bar = ((lam_bar - 1.0) / lam)[..., None] * lax.complex(b_re.astype(f32), b_im.astype(f32))
    return lam_bar, b_bar


def s5_scan(u_tm, lam_bar, b_bar, init, reverse):
    bu = jnp.einsum('lbgh,gph->lbgp', u_tm.astype(jnp.complex64), b_bar)
    edge = bu.shape[0] - 1 if reverse else 0
    bu = bu.at[edge].add(lam_bar * init)
    a = jnp.broadcast_to(lam_bar[None, None], (bu.shape[0], 1) + lam_bar.shape)
    _, states = lax.associative_scan(_diag_binop, (a, bu), reverse=reverse, axis=0)
    return states


def s5_states(u, lam_re, lam_im, log_step, b_re, b_im, init_f, init_b):
    bsz, length, _ = u.shape
    u_tm = jnp.transpose(u.reshape(bsz, length, S5_GROUPS, S5_GROUP), (1, 0, 2, 3))
    lam_f, bb_f = s5_discretize(lam_re[0], lam_im[0], log_step[0], b_re[0], b_im[0])
    lam_b, bb_b = s5_discretize(lam_re[1], lam_im[1], log_step[1], b_re[1], b_im[1])
    st_f = s5_scan(u_tm, lam_f, bb_f, init_f, False)
    st_b = s5_scan(u_tm, lam_b, bb_b, init_b, True)
    return st_f, st_b


def s5_readout(u, st_f, st_b, c_re, c_im, d, glu_w, glu_b):
    f32 = jnp.float32
    bsz, length, _ = u.shape
    c_f = lax.complex(c_re[0].astype(f32), c_im[0].astype(f32))
    c_b = lax.complex(c_re[1].astype(f32), c_im[1].astype(f32))
    y = jnp.real(jnp.einsum('lbgp,ghp->blgh', st_f, c_f) + jnp.einsum('lbgp,ghp->blgh', st_b, c_b))
    y = y.reshape(bsz, length, S5_WIDTH) + d.astype(f32) * u
    y = jax.nn.gelu(y)
    return y * jax.nn.sigmoid(y @ glu_w.astype(f32) + glu_b.astype(f32))


def expert_choice_ffn(h, router_w, w_gate, w_up, w_down):
    n, dm = h.shape[1], h.shape[2]
    cap = CAPACITY_FACTOR * n // N_EXPERTS
    aff = jax.nn.softmax(jnp.einsum('bnd,de->bne', h, router_w).astype(jnp.float32), axis=-1)
    gates, idx = lax.top_k(jnp.swapaxes(aff, 1, 2), cap)
    xg = jax.vmap(lambda hb, ib: hb[ib])(h, idx)

    def expert(args):
        xe, wg, wu, wd = args
        return (jax.nn.silu(xe @ wg) * (xe @ wu)) @ wd

    ye = lax.map(expert, (jnp.swapaxes(xg, 0, 1), w_gate, w_up, w_down))
    ye = jnp.swapaxes(ye, 0, 1) * gates[..., None].astype(h.dtype)
    return jax.vmap(lambda ib, yb: jnp.zeros((n, dm), yb.dtype).at[ib.reshape(-1)].add(yb.reshape(-1, dm)))(idx, ye)


def setup_inputs(seed: int = 0) -> dict:
    key = jax.random.key(seed)
    ks = iter(jax.random.split(key, 40))
    f32 = jnp.float32

    def nrm(shape, scale):
        return scale * jax.random.normal(next(ks), shape, f32)

    D, L = D_MODEL, DEPTH
    G, P, H = S5_GROUPS, S5_STATE, S5_GROUP
    return {
        "x": nrm((BATCH, SEQ, D), 1.0),
        "c": nrm((BATCH, D), 1.0),
        "ctx": nrm((BATCH, CTX_LEN, D), 1.0),
        "c_ctx": nrm((D,), 1.0),
        "mod_w": nrm((L, D, N_MOD * D), D ** -0.5),
        "mod_b": nrm((L, N_MOD * D), 0.01),
        "norm1_g": 1.0 + nrm((L, D), 0.02),
        "norm2_g": 1.0 + nrm((L, D), 0.02),
        "w_in": nrm((L, D, IN_COLS), D ** -0.5),
        "w_out": nrm((L, MIX_WIDTH, D), MIX_WIDTH ** -0.5),
        "conv_w": nrm((L, SHORT_CONV, HY_COLS), SHORT_CONV ** -0.5),
        "conv_b": nrm((L, HY_COLS), 0.01),
        "hy_w1": nrm((L, POS_EMB_DIM, FILTER_WIDTH), POS_EMB_DIM ** -0.5),
        "hy_b1": nrm((L, FILTER_WIDTH), 0.1),
        "hy_freq": 1.0 + nrm((L, 2, FILTER_WIDTH), 0.02),
        "hy_w2": nrm((L, FILTER_WIDTH, FILTER_WIDTH), FILTER_WIDTH ** -0.5),
        "hy_b2": nrm((L, FILTER_WIDTH), 0.1),
        "hy_w3": nrm((L, FILTER_WIDTH, FILTER_OUT), FILTER_WIDTH ** -0.5),
        "hy_b3": nrm((L, FILTER_OUT), 0.01),
        "hy_bias": nrm((L, HY_ORDER, HY_WIDTH), 1.0),
        "s5_lam_re": -0.5 + nrm((L, 2, G, P), 0.01),
        "s5_lam_im": math.pi * jnp.arange(P, dtype=f32) + nrm((L, 2, G, P), 0.01),
        "s5_log_step": jax.random.uniform(next(ks), (L, 2, G), f32, math.log(1e-3), math.log(1e-1)),
        "s5_b_re": nrm((L, 2, G, P, H), (2.0 * H) ** -0.5),
        "s5_b_im": nrm((L, 2, G, P, H), (2.0 * H) ** -0.5),
        "s5_c_re": nrm((L, 2, G, H, P), (2.0 * P) ** -0.5),
        "s5_c_im": nrm((L, 2, G, H, P), (2.0 * P) ** -0.5),
        "s5_d": nrm((L, S5_WIDTH), 1.0),
        "s5_glu_w": nrm((L, S5_WIDTH, S5_WIDTH), S5_WIDTH ** -0.5),
        "s5_glu_b": nrm((L, S5_WIDTH), 0.01),
        "router_w": nrm((L, D, N_EXPERTS), D ** -0.5),
        "ex_w_gate": nrm((L, N_EXPERTS, D, EXPERT_FF), D ** -0.5),
        "ex_w_up": nrm((L, N_EXPERTS, D, EXPERT_FF), D ** -0.5),
        "ex_w_down": nrm((L, N_EXPERTS, EXPERT_FF, D), EXPERT_FF ** -0.5),
        "final_g": 1.0 + nrm((D,), 0.02),
    }


def reference(x, c, ctx, c_ctx, mod_w, mod_b, norm1_g, norm2_g, w_in, w_out, conv_w, conv_b,
              hy_w1, hy_b1, hy_freq, hy_w2, hy_b2, hy_w3, hy_b3, hy_bias,
              s5_lam_re, s5_lam_im, s5_log_step, s5_b_re, s5_b_im, s5_c_re, s5_c_im, s5_d,
              s5_glu_w, s5_glu_b, router_w, ex_w_gate, ex_w_up, ex_w_down, final_g):
    rows = x.shape[1] // GRID_W
    bsz = x.shape[0]
    for l in range(DEPTH):
        last = l == DEPTH - 1
        sh1, sc1, g1, sh2, sc2, g2 = adaln(c, mod_w[l], mod_b[l])
        csh1, csc1, cg1, csh2, csc2, cg2 = adaln(c_ctx[None], mod_w[l], mod_b[l])
        filt = (hy_w1[l], hy_b1[l], hy_freq[l], hy_w2[l], hy_b2[l], hy_w3[l], hy_b3[l])
        s5_par = (s5_lam_re[l], s5_lam_im[l], s5_log_step[l], s5_b_re[l], s5_b_im[l])
        s5_out = (s5_c_re[l], s5_c_im[l], s5_d[l], s5_glu_w[l], s5_glu_b[l])

        hc = modulate(rmsnorm(ctx, norm1_g[l]), csh1, csc1)
        u_c = (hc @ w_in[l][:, HY_COLS:]).astype(jnp.float32)
        zero = jnp.zeros((bsz, S5_GROUPS, S5_STATE), jnp.complex64)
        stc_f, stc_b = s5_states(u_c, *s5_par, zero, zero)
        init_f, init_b = stc_f[-1], stc_b[0]
        if not last:
            yc = jnp.concatenate([hyena_branch(hc @ w_in[l][:, :HY_COLS], 1, conv_w[l], conv_b[l], filt, hy_bias[l]),
                                  s5_readout(u_c, stc_f, stc_b, *s5_out)], axis=-1)
            ctx = ctx + cg1[:, None] * (yc.astype(ctx.dtype) @ w_out[l])
            hc2 = modulate(rmsnorm(ctx, norm2_g[l]), csh2, csc2)
            ctx = ctx + cg2[:, None] * expert_choice_ffn(hc2, router_w[l], ex_w_gate[l], ex_w_up[l], ex_w_down[l])

        h = modulate(rmsnorm(x, norm1_g[l]), sh1, sc1)
        z = h @ w_in[l]
        u = z[..., HY_COLS:].astype(jnp.float32)
        st_f, st_b = s5_states(u, *s5_par, init_f, init_b)
        y = jnp.concatenate([hyena_branch(z[..., :HY_COLS], rows, conv_w[l], conv_b[l], filt, hy_bias[l]),
                             s5_readout(u, st_f, st_b, *s5_out)], axis=-1)
        x = x + g1[:, None] * (y.astype(x.dtype) @ w_out[l])
        h2 = modulate(rmsnorm(x, norm2_g[l]), sh2, sc2)
        x = x + g2[:, None] * expert_choice_ffn(h2, router_w[l], ex_w_gate[l], ex_w_up[l], ex_w_down[l])
    return rmsnorm(x, final_g)
```

```python
import functools
import math

import numpy as np
import jax
import jax.numpy as jnp
from jax import lax
from jax.experimental import pallas as pl
from jax.experimental.pallas import tpu as pltpu

F32 = jnp.float32
BF16 = jnp.bfloat16
I32 = jnp.int32
HIGHEST = lax.Precision.HIGHEST

SUBLANES = 8
LANES = 128
VMEM_LIMIT_BYTES = 58 * 1024 * 1024

GRID_W = 64
N_MOD = 6
NORM_EPS = 1e-6
POS_BANDS = 16
DECAY_FAST = 0.3
DECAY_SLOW = 1.5
DECAY_TARGET = 1e-2
CAPACITY_FACTOR = 2

DFT_N1 = 64
DFT_N2 = 128
DFT_K1 = DFT_N1 // 2 + 1
DFT_ROWS = 2 * DFT_K1


def _cparams(sem, vmem=None):
    return pltpu.CompilerParams(dimension_semantics=sem, vmem_limit_bytes=vmem)


def _silu(x):
    return x * (1.0 / (1.0 + jnp.exp(-x)))


def _rmsnorm(x, g):
    ms = jnp.mean(x * x, axis=-1, keepdims=True)
    return x * lax.rsqrt(ms + NORM_EPS) * g


def _adaln_kernel(c_ref, w_ref, b_ref, o_ref):
    s = _silu(c_ref[...])
    o_ref[...] = jnp.dot(s, w_ref[...], precision=HIGHEST, preferred_element_type=F32) + b_ref[...]


def adaln_mods(cond, mod_w, mod_b, tn=1536):
    rows, d = cond.shape
    n = mod_w.shape[1]
    return pl.pallas_call(
        _adaln_kernel,
        grid=(n // tn,),
        in_specs=[pl.BlockSpec((rows, d), lambda j: (0, 0)),
                  pl.BlockSpec((d, tn), lambda j: (0, j)),
                  pl.BlockSpec((1, tn), lambda j: (0, j))],
        out_specs=pl.BlockSpec((rows, tn), lambda j: (0, j)),
        out_shape=jax.ShapeDtypeStruct((rows, n), F32),
        compiler_params=_cparams(("arbitrary",)),
        name="adaln",
    )(cond, mod_w, mod_b.reshape(1, n))


def _inproj_kernel(x_ref, g_ref, sh_ref, sc_ref, w_ref, *o_refs, splits):
    h = _rmsnorm(x_ref[...], g_ref[...])
    h = h * (1.0 + sc_ref[...]) + sh_ref[...]
    z = jnp.dot(h.astype(BF16), w_ref[...], preferred_element_type=F32)
    off = 0
    for o_ref, n in zip(o_refs, splits):
        o_ref[...] = z[:, off:off + n]
        off += n


def inproj(x, g, shift, scale, w_bf16, splits, ts=512):
    b, s, d = x.shape
    n = w_bf16.shape[1]
    ts = min(ts, s)
    return pl.pallas_call(
        functools.partial(_inproj_kernel, splits=splits),
        grid=(b, s // ts),
        in_specs=[pl.BlockSpec((None, ts, d), lambda i, j: (i, j, 0)),
                  pl.BlockSpec((1, d), lambda i, j: (0, 0)),
                  pl.BlockSpec((None, 1, d), lambda i, j: (i, 0, 0)),
                  pl.BlockSpec((None, 1, d), lambda i, j: (i, 0, 0)),
                  pl.BlockSpec((d, n), lambda i, j: (0, 0))],
        out_specs=[pl.BlockSpec((None, ts, m), lambda i, j: (i, j, 0)) for m in splits],
        out_shape=[jax.ShapeDtypeStruct((b, s, m), F32) for m in splits],
        compiler_params=_cparams(("arbitrary", "arbitrary")),
        name="inproj",
    )(x, g.reshape(1, d), shift, scale, w_bf16)


def _filt_mlp_kernel(w1t_ref, w1c_ref, w1s_ref, b1_ref, fr_ref, w2_ref, b2_ref, w3_ref, b3_ref,
                     o_ref, *, length, tl):
    i0 = pl.program_id(0) * tl
    idx = (lax.broadcasted_iota(I32, (tl, 1), 0) + i0).astype(F32)
    t = idx / float(length - 1)
    omega = (2.0 * math.pi) * idx / float(length)
    fstep = ((POS_BANDS - 1) - 1e-4) / (POS_BANDS - 1)
    f = 1e-4 + lax.broadcasted_iota(I32, (1, POS_BANDS), 1).astype(F32) * fstep
    arg = omega * f
    pre = (t * w1t_ref[...]
           + jnp.dot(jnp.cos(arg), w1c_ref[...], precision=HIGHEST, preferred_element_type=F32)
           - jnp.dot(jnp.sin(arg), w1s_ref[...], precision=HIGHEST, preferred_element_type=F32)
           + b1_ref[...])
    fr = fr_ref[...]
    h = jnp.sin(fr[0:1, :] * pre)
    h = jnp.sin(fr[1:2, :] * (jnp.dot(h, w2_ref[...], precision=HIGHEST,
                                      preferred_element_type=F32) + b2_ref[...]))
    o_ref[...] = jnp.dot(h, w3_ref[...], precision=HIGHEST, preferred_element_type=F32) + b3_ref[...]


def hyena_filter_mlp(length, w1, b1, freq, w2, b2, w3, b3, tl=512):
    fw = w1.shape[1]
    n = w3.shape[1]
    full = lambda shape: pl.BlockSpec(shape, lambda i: (0, 0))
    return pl.pallas_call(
        functools.partial(_filt_mlp_kernel, length=length, tl=tl),
        grid=(length // tl,),
        in_specs=[full((1, fw)), full((POS_BANDS, fw)), full((POS_BANDS, fw)), full((1, fw)),
                  full((2, fw)), full((fw, fw)), full((1, fw)), full((fw, n)), full((1, n))],
        out_specs=pl.BlockSpec((tl, n), lambda i: (i, 0)),
        out_shape=jax.ShapeDtypeStruct((length, n), F32),
        compiler_params=_cparams(("arbitrary",)),
        name="hyena_filter_mlp",
    )(w1[0:1], w1[1:1 + POS_BANDS], w1[1 + POS_BANDS:], b1.reshape(1, fw), freq, w2,
      b2.reshape(1, fw), w3, b3.reshape(1, n))


@functools.lru_cache(maxsize=None)
def _dft_tables():
    n1n, n2n, k1n = DFT_N1, DFT_N2, DFT_K1
    n = n1n * n2n
    half = n1n // 2
    k1 = np.arange(k1n)[:, None]
    n1 = np.arange(half)[None, :]
    th = 2.0 * np.pi * k1 * n1 / n1n
    f1 = np.zeros((DFT_ROWS, half))
    f1[0::2] = np.cos(th)
    f1[1::2] = -np.sin(th)
    wgt = np.where((k1 == 0) | (k1 == half), 1.0, 2.0)
    g1 = np.zeros((half, DFT_ROWS))
    g1[:, 0::2] = (wgt * np.cos(th)).T / n
    g1[:, 1::2] = (-wgt * np.sin(th)).T / n
    eye = np.eye(SUBLANES)
    fk = np.kron(f1, eye)
    gk = np.kron(g1, eye)
    k2 = np.arange(n2n)[:, None]
    n2 = np.arange(n2n)[None, :]
    f3 = np.zeros((k1n, 2 * n2n, 2 * n2n))
    for kk in range(k1n):
        ph = 2.0 * np.pi * n2 * (n1n * k2 + kk) / n
        tr, ti = np.cos(ph), -np.sin(ph)
        f3[kk, :n2n, :n2n] = tr
        f3[kk, :n2n, n2n:] = -ti
        f3[kk, n2n:, :n2n] = ti
        f3[kk, n2n:, n2n:] = tr
    g3 = np.transpose(f3, (0, 2, 1))
    to = lambda a: jnp.asarray(a, dtype=F32).astype(BF16)
    return to(fk), to(gk), to(f3), to(g3)


def _dft_stage1(src_ref, a_ref, fk_ref):
    half = DFT_N1 // 2

    def body(m, carry):
        base = pl.multiple_of(m * SUBLANES, SUBLANES)
        rows = [src_ref[pl.ds(base + DFT_N2 * n1, SUBLANES), :] for n1 in range(half)]
        rhs = jnp.concatenate(rows, axis=0).astype(BF16)
        out = jnp.dot(fk_ref[...], rhs, preferred_element_type=F32)
        for j in range(DFT_ROWS):
            a_ref[pl.ds(base + DFT_N2 * j, SUBLANES), :] = out[SUBLANES * j:SUBLANES * (j + 1)]
        return carry

    lax.fori_loop(0, DFT_N2 // SUBLANES, body, 0)


def _dft_stage3(a_ref, f3_ref, k1):
    r0 = pl.multiple_of(k1 * (2 * DFT_N2), 2 * DFT_N2)
    a = a_ref[pl.ds(r0, 2 * DFT_N2), :].astype(BF16)
    x = jnp.dot(f3_ref[k1], a, preferred_element_type=F32)
    return x[:DFT_N2], x[DFT_N2:]


def _filt_spec_kernel(hf_ref, hb_ref, fk_ref, f3_ref, o_ref, src_ref, af_ref, ab_ref, *, length):
    c = hf_ref.shape[1]
    cb = pl.program_id(1)
    hw = pl.num_programs(1) * c
    row = lax.broadcasted_iota(I32, (length, 1), 0)
    t = row.astype(F32) / float(length - 1)
    ch = (lax.broadcasted_iota(I32, (1, c), 1) + cb * c).astype(F32)
    d0 = math.log(DECAY_TARGET) / DECAY_FAST
    d1 = math.log(DECAY_TARGET) / DECAY_SLOW
    deltas = jnp.abs(d0 + ch * ((d1 - d0) / float(hw - 1)))
    decay = jnp.exp(-t * deltas)
    fwd = hf_ref[...] * decay
    bwd = jnp.where(row == 0, 0.0, hb_ref[...] * decay)
    inv = 1.0 / (jnp.sum(jnp.abs(fwd), axis=0, keepdims=True)
                 + jnp.sum(jnp.abs(bwd), axis=0, keepdims=True))
    src_ref[...] = fwd
    _dft_stage1(src_ref, af_ref, fk_ref)
    src_ref[...] = bwd
    _dft_stage1(src_ref, ab_ref, fk_ref)

    def body(k1, carry):
        fr, fi = _dft_stage3(af_ref, f3_ref, k1)
        br, bi = _dft_stage3(ab_ref, f3_ref, k1)
        o_ref[k1, 0] = ((fr + br) * inv).astype(o_ref.dtype)
        o_ref[k1, 1] = ((fi - bi) * inv).astype(o_ref.dtype)
        return carry

    lax.fori_loop(0, DFT_K1, body, 0)


def hyena_filter_spectra(hraw, n_order, width, c_blk=256):
    length = hraw.shape[0]
    assert 2 * length == DFT_N1 * DFT_N2
    fk, _, f3, _ = _dft_tables()
    ncb = width // c_blk
    return pl.pallas_call(
        functools.partial(_filt_spec_kernel, length=length),
        grid=(n_order, ncb),
        in_specs=[pl.BlockSpec((length, c_blk), lambda o, j: (0, o * 2 * ncb + j)),
                  pl.BlockSpec((length, c_blk), lambda o, j: (0, o * 2 * ncb + ncb + j)),
                  pl.BlockSpec(fk.shape, lambda o, j: (0, 0)),
                  pl.BlockSpec(f3.shape, lambda o, j: (0, 0, 0))],
        out_specs=pl.BlockSpec((None, DFT_K1, 2, DFT_N2, c_blk), lambda o, j: (o, 0, 0, 0, j)),
        out_shape=jax.ShapeDtypeStruct((n_order, DFT_K1, 2, DFT_N2, width), BF16),
        scratch_shapes=[pltpu.VMEM((length, c_blk), F32),
                        pltpu.VMEM((DFT_ROWS * DFT_N2, c_blk), F32),
                        pltpu.VMEM((DFT_ROWS * DFT_N2, c_blk), F32)],
        compiler_params=_cparams(("arbitrary", "arbitrary"), VMEM_LIMIT_BYTES),
        name="hyena_filter_spectrum",
    )(hraw, hraw, fk, f3)


def _short_conv(z, w_ref, b_ref, rows):
    length = z.shape[0]
    pos = lax.broadcasted_iota(I32, (length, 1), 0) % (length // rows)
    zm = jnp.where(pos == 0, 0.0, pltpu.roll(z, 1, 0))
    zp = jnp.where(pos == (length // rows) - 1, 0.0, pltpu.roll(z, length - 1, 0))
    w = w_ref[...]
    return zm * w[0:1, :] + z * w[1:2, :] + zp * w[2:3, :] + b_ref[...]


def _hyena_conv_kernel(s_ref, m_ref, kf_ref, bias_ref, cws_ref, cbs_ref, cwm_ref, cbm_ref,
                       fk_ref, f3_ref, g3_ref, gk_ref, o_ref, src_ref, a_ref, *, rows, conv_signal):
    half = DFT_N1 // 2
    sig = s_ref[...]
    if conv_signal:
        sig = _short_conv(sig, cws_ref, cbs_ref, rows)
    src_ref[...] = sig
    o_ref[...] = _short_conv(m_ref[...], cwm_ref, cbm_ref, rows)
    _dft_stage1(src_ref, a_ref, fk_ref)

    def body3(k1, carry):
        xr, xi = _dft_stage3(a_ref, f3_ref, k1)
        kr = kf_ref[k1, 0].astype(F32)
        ki = kf_ref[k1, 1].astype(F32)
        y = jnp.concatenate([xr * kr - xi * ki, xr * ki + xi * kr], axis=0).astype(BF16)
        r0 = pl.multiple_of(k1 * (2 * DFT_N2), 2 * DFT_N2)
        a_ref[pl.ds(r0, 2 * DFT_N2), :] = jnp.dot(g3_ref[k1], y, preferred_element_type=F32)
        return carry

    lax.fori_loop(0, DFT_K1, body3, 0)
    bias = bias_ref[...]

    def body1(m, carry):
        base = pl.multiple_of(m * SUBLANES, SUBLANES)
        blk = [a_ref[pl.ds(base + DFT_N2 * j, SUBLANES), :] for j in range(DFT_ROWS)]
        rhs = jnp.concatenate(blk, axis=0).astype(BF16)
        out = jnp.dot(gk_ref[...], rhs, preferred_element_type=F32)
        for n1 in range(half):
            r = pl.ds(base + DFT_N2 * n1, SUBLANES)
            conv = out[SUBLANES * n1:SUBLANES * (n1 + 1)]
            o_ref[r, :] = o_ref[r, :] * (conv + src_ref[r, :] * bias)
        return carry

    lax.fori_loop(0, DFT_N2 // SUBLANES, body1, 0)


def hyena_conv(sig, sig_col, mul, mul_col, kf, bias, cw_s, cb_s, cw_m, cb_m, rows, conv_signal,
               c_blk=256):
    b, length, _ = sig.shape
    width = kf.shape[-1]
    ncb = width // c_blk
    fk, gk, f3, g3 = _dft_tables()
    const2 = lambda a: pl.BlockSpec(a.shape, lambda j, i: (0, 0))
    const3 = lambda a: pl.BlockSpec(a.shape, lambda j, i: (0, 0, 0))
    chan = lambda r: pl.BlockSpec((r, c_blk), lambda j, i: (0, j))
    return pl.pallas_call(
        functools.partial(_hyena_conv_kernel, rows=rows, conv_signal=conv_signal),
        grid=(ncb, b),
        in_specs=[pl.BlockSpec((None, length, c_blk), lambda j, i: (i, 0, sig_col * ncb + j)),
                  pl.BlockSpec((None, length, c_blk), lambda j, i: (i, 0, mul_col * ncb + j)),
                  pl.BlockSpec((DFT_K1, 2, DFT_N2, c_blk), lambda j, i: (0, 0, 0, j)),
                  chan(1), chan(3), chan(1), chan(3), chan(1),
                  const2(fk), const3(f3), const3(g3), const2(gk)],
        out_specs=pl.BlockSpec((None, length, c_blk), lambda j, i: (i, 0, j)),
        out_shape=jax.ShapeDtypeStruct((b, length, width), F32),
        scratch_shapes=[pltpu.VMEM((length, c_blk), F32),
                        pltpu.VMEM((DFT_ROWS * DFT_N2, c_blk), F32)],
        compiler_params=_cparams(("arbitrary", "arbitrary"), VMEM_LIMIT_BYTES),
        name="hyena_conv",
    )(sig, mul, kf, bias.reshape(1, width), cw_s, cb_s.reshape(1, width), cw_m,
      cb_m.reshape(1, width), fk, f3, g3, gk)


def _s5_disc_kernel(lr_ref, li_ref, dt_ref, br_ref, bi_ref, ar_ref, ai_ref, bbr_ref, bbi_ref):
    lr, li, dt = lr_ref[...], li_ref[...], jnp.exp(dt_ref[...])
    mag = jnp.exp(lr * dt)
    ar = mag * jnp.cos(li * dt)
    ai = mag * jnp.sin(li * dt)
    den = 1.0 / (lr * lr + li * li)
    qr = ((ar - 1.0) * lr + ai * li) * den
    qi = (ai * lr - (ar - 1.0) * li) * den
    br, bi = br_ref[...], bi_ref[...]
    ar_ref[...] = ar
    ai_ref[...] = ai
    bbr_ref[...] = qr * br - qi * bi
    bbi_ref[...] = qr * bi + qi * br


def s5_discretize(lam_re, lam_im, log_step, b_re, b_im):
    nd, g, p, h = b_re.shape
    rep = lambda a: jnp.broadcast_to(a[..., None], (nd, g, p, h)).reshape(nd * g, p * h)
    dt = jnp.broadcast_to(log_step[:, :, None, None], (nd, g, p, h)).reshape(nd * g, p * h)
    flat = lambda a: a.reshape(nd * g, p * h)
    shp = jax.ShapeDtypeStruct((nd * g, p * h), F32)
    ar, ai, bbr, bbi = pl.pallas_call(
        _s5_disc_kernel, out_shape=[shp] * 4, name="s5_discretize",
    )(rep(lam_re), rep(lam_im), dt, flat(b_re), flat(b_im))
    un = lambda a: a.reshape(nd, g, p, h)
    return un(ar)[..., 0], un(ai)[..., 0], un(bbr), un(bbi)


def _s5_scan_kernel(uf_ref, ub_ref, bf_ref, bb_ref, cf_ref, cb_ref, lam_ref, yf_ref, yb_ref,
                    xf_ref, xb_ref, st_ref, *, steps, nk):
    half = xf_ref.shape[1] // 2
    sk = half // nk
    ck = uf_ref.shape[1] // nk

    @pl.when(pl.program_id(0) == 0)
    def _():
        st_ref[...] = jnp.zeros_like(st_ref)

    def project_in(u_ref, w_ref, x_ref):
        for k in range(nk):
            uk = u_ref[:, k * ck:(k + 1) * ck].astype(BF16)
            x_ref[:, k * sk:(k + 1) * sk] = jnp.dot(uk, w_ref[0, k], preferred_element_type=F32)
            x_ref[:, half + k * sk:half + (k + 1) * sk] = jnp.dot(uk, w_ref[1, k],
                                                                  preferred_element_type=F32)

    def scan(x_ref, d, reverse):
        lr = lam_ref[d, 0]
        li = lam_ref[d, 1]

        def body(t, carry):
            sr, si = carry
            tt = (steps - 1 - t) if reverse else t
            r = pl.ds(pl.multiple_of(tt * SUBLANES, SUBLANES), SUBLANES)
            nr = lr * sr - li * si + x_ref[r, :half]
            ni = lr * si + li * sr + x_ref[r, half:]
            x_ref[r, :half] = nr
            x_ref[r, half:] = ni
            return nr, ni

        sr, si = lax.fori_loop(0, steps, body, (st_ref[d, 0], st_ref[d, 1]))
        st_ref[d, 0] = sr
        st_ref[d, 1] = si

    def project_out(x_ref, w_ref, y_ref):
        for k in range(nk):
            xr = x_ref[:, k * sk:(k + 1) * sk].astype(BF16)
            xi = x_ref[:, half + k * sk:half + (k + 1) * sk].astype(BF16)
            y_ref[:, k * ck:(k + 1) * ck] = (
                jnp.dot(xr, w_ref[0, k], preferred_element_type=F32)
                + jnp.dot(xi, w_ref[1, k], preferred_element_type=F32))

    project_in(uf_ref, bf_ref, xf_ref)
    scan(xf_ref, 0, False)
    project_out(xf_ref, cf_ref, yf_ref)
    project_in(ub_ref, bb_ref, xb_ref)
    scan(xb_ref, 1, True)
    project_out(xb_ref, cb_ref, yb_ref)


def _block_diag_in(bb, nk):
    g, p, h = bb.shape
    gpk = g // nk
    eye = jnp.eye(gpk, dtype=bb.dtype)
    w = jnp.einsum('kaph,ab->kahbp', bb.reshape(nk, gpk, p, h), eye)
    return w.reshape(nk, gpk * h, gpk * p)


def _block_diag_out(cc, nk):
    g, h, p = cc.shape
    gpk = g // nk
    eye = jnp.eye(gpk, dtype=cc.dtype)
    w = jnp.einsum('kahp,ab->kapbh', cc.reshape(nk, gpk, h, p), eye)
    return w.reshape(nk, gpk * p, gpk * h)


def s5_scan(uf, ub, lam_r, lam_i, bb_r, bb_i, c_re, c_im, batch, steps=64):
    rows, width = uf.shape
    nk = width // LANES
    _, g, p = lam_r.shape
    nst = g * p
    rpc = steps * batch
    nchunk = rows // rpc
    w_in = jnp.stack([jnp.stack([_block_diag_in(bb_r[d], nk), _block_diag_in(bb_i[d], nk)])
                      for d in range(2)]).astype(BF16)
    w_out = jnp.stack([jnp.stack([_block_diag_out(c_re[d], nk), -_block_diag_out(c_im[d], nk)])
                       for d in range(2)]).astype(BF16)
    lam = jnp.stack([lam_r.reshape(2, nst), lam_i.reshape(2, nst)], axis=1)
    lam = jnp.broadcast_to(lam[:, :, None, :], (2, 2, batch, nst))
    fwd = pl.BlockSpec((rpc, width), lambda i: (i, 0))
    bwd = pl.BlockSpec((rpc, width), lambda i: (nchunk - 1 - i, 0))
    shp = jax.ShapeDtypeStruct((rows, width), F32)
    return pl.pallas_call(
        functools.partial(_s5_scan_kernel, steps=steps, nk=nk),
        grid=(nchunk,),
        in_specs=[fwd, bwd,
                  pl.BlockSpec((None,) + w_in.shape[1:], lambda i: (0, 0, 0, 0, 0)),
                  pl.BlockSpec((None,) + w_in.shape[1:], lambda i: (1, 0, 0, 0, 0)),
                  pl.BlockSpec((None,) + w_out.shape[1:], lambda i: (0, 0, 0, 0, 0)),
                  pl.BlockSpec((None,) + w_out.shape[1:], lambda i: (1, 0, 0, 0, 0)),
                  pl.BlockSpec(lam.shape, lambda i: (0, 0, 0, 0))],
        out_specs=[fwd, bwd],
        out_shape=[shp, shp],
        scratch_shapes=[pltpu.VMEM((rpc, 2 * nst), F32), pltpu.VMEM((rpc, 2 * nst), F32),
                        pltpu.VMEM((2, 2, batch, nst), F32)],
        compiler_params=_cparams(("arbitrary",), VMEM_LIMIT_BYTES),
        name="s5_scan",
    )(uf, ub, w_in, w_in, w_out, w_out, lam)


def _mixer_tail_kernel(x_ref, hy_ref, yf_ref, yb_ref, u_ref, d_ref, gw_ref, gb_ref, woh_ref, wos_ref,
                       g1_ref, n2_ref, sh2_ref, sc2_ref, rwt_ref, x1_ref, h2_ref, lg_ref):
    y = yf_ref[...] + yb_ref[...] + d_ref[...] * u_ref[...]
    y = 0.5 * y * (1.0 + jnp.tanh(math.sqrt(2.0 / math.pi) * (y + 0.044715 * (y * y * y))))
    gate = jnp.dot(y.astype(BF16), gw_ref[...], preferred_element_type=F32) + gb_ref[...]
    s5 = y * (1.0 / (1.0 + jnp.exp(-gate)))
    mix = (jnp.dot(hy_ref[...].astype(BF16), woh_ref[...], preferred_element_type=F32)
           + jnp.dot(s5.astype(BF16), wos_ref[...], preferred_element_type=F32))
    x1 = x_ref[...] + g1_ref[...] * mix
    x1_ref[...] = x1
    h2 = _rmsnorm(x1, n2_ref[...]) * (1.0 + sc2_ref[...]) + sh2_ref[...]
    h2_ref[...] = h2.astype(BF16)
    lg_ref[...] = lax.dot_general(rwt_ref[...], h2, (((1,), (1,)), ((), ())), precision=HIGHEST,
                                  preferred_element_type=F32)


def mixer_tail(x, hy, yf, yb, u, s5_d, glu_w, glu_b, w_out, g1, norm2_g, sh2, sc2, router_w, ts=512):
    b, s, d = x.shape
    hw = hy.shape[2]
    sw = u.shape[2]
    ne = router_w.shape[1]
    tok = lambda n: pl.BlockSpec((None, ts, n), lambda i, j: (i, j, 0))
    vec = lambda n: pl.BlockSpec((1, n), lambda i, j: (0, 0))
    per_b = pl.BlockSpec((None, 1, d), lambda i, j: (i, 0, 0))
    mat = lambda r, c: pl.BlockSpec((r, c), lambda i, j: (0, 0))
    return pl.pallas_call(
        _mixer_tail_kernel,
        grid=(b, s // ts),
        in_specs=[tok(d), tok(hw), tok(sw), tok(sw), tok(sw), vec(sw), mat(sw, sw), vec(sw),
                  mat(hw, d), mat(sw, d), per_b, vec(d), per_b, per_b, mat(ne, d)],
        out_specs=[tok(d), tok(d), pl.BlockSpec((None, ne, ts), lambda i, j: (i, 0, j))],
        out_shape=[jax.ShapeDtypeStruct((b, s, d), F32), jax.ShapeDtypeStruct((b, s, d), BF16),
                   jax.ShapeDtypeStruct((b, ne, s), F32)],
        compiler_params=_cparams(("arbitrary", "arbitrary")),
        name="mixer_tail",
    )(x, hy, yf, yb, u, s5_d.reshape(1, sw), glu_w.astype(BF16), glu_b.reshape(1, sw),
      w_out[:hw].astype(BF16), w_out[hw:].astype(BF16), g1, norm2_g.reshape(1, d), sh2, sc2,
      router_w.T)


def _lane_cumsum_exclusive(x):
    rows, s = x.shape
    ii = lax.broadcasted_iota(I32, (LANES, LANES), 0)
    jj = lax.broadcasted_iota(I32, (LANES, LANES), 1)
    tri = jnp.where(ii < jj, 1.0, 0.0).astype(BF16)
    carry = jnp.zeros((rows, 1), F32)
    out = []
    for blk in range(s // LANES):
        xb = x[:, blk * LANES:(blk + 1) * LANES]
        out.append(jnp.dot(xb.astype(BF16), tri, preferred_element_type=F32) + carry)
        carry = carry + jnp.sum(xb, axis=1, keepdims=True)
    return jnp.concatenate(out, axis=1)


def _route_kernel(lg_ref, pos_em_ref, pos_tm_ref, gate_tm_ref, aff_ref, *, cap):
    lg = lg_ref[...]
    ne, s = lg.shape
    ex = jnp.exp(lg - jnp.max(lg, axis=0, keepdims=True))
    aff_ref[...] = ex / jnp.sum(ex, axis=0, keepdims=True)
    aff = aff_ref[...]
    count_ge = lambda v, t: jnp.sum(jnp.where(v >= t, 1.0, 0.0), axis=1, keepdims=True)

    def coarse(i, tb):
        cand = tb | jnp.left_shift(jnp.int32(1), 30 - i)
        return jnp.where(count_ge(aff, pltpu.bitcast(cand, F32)) >= cap, cand, tb)

    tb = lax.fori_loop(0, 31, coarse, jnp.zeros((ne, 1), I32))
    t_hi = pltpu.bitcast(tb, F32)
    ulp = pltpu.bitcast(tb + 1, F32) - t_hi
    resid = aff - t_hi

    def fine(j, carry):
        c, step = carry
        cand = c + step
        return jnp.where(count_ge(resid, cand) >= cap, cand, c), step * 0.5

    t_lo, _ = lax.fori_loop(0, 24, fine, (jnp.zeros((ne, 1), F32), ulp * 0.5))
    gt = resid > t_lo
    eq = resid == t_lo
    need = cap - jnp.sum(jnp.where(gt, 1.0, 0.0), axis=1, keepdims=True)
    eq_rank = _lane_cumsum_exclusive(jnp.where(eq, 1.0, 0.0))
    sel = gt | (eq & (eq_rank < need))
    pos = _lane_cumsum_exclusive(jnp.where(sel, 1.0, 0.0))
    posf = jnp.where(sel, pos + 1.0, 0.0)
    gate = jnp.where(sel, aff, 0.0)
    pos_em_ref[...] = posf.astype(I32) - 1
    hi = jnp.floor(posf * (1.0 / 16.0))
    lo = posf - 16.0 * hi
    ii = lax.broadcasted_iota(I32, (LANES, LANES), 0)
    jj = lax.broadcasted_iota(I32, (LANES, LANES), 1)
    eye = jnp.where(ii == jj, 1.0, 0.0)
    nt = (((1,), (1,)), ((), ()))
    for blk in range(s // LANES):
        sl = slice(blk * LANES, (blk + 1) * LANES)
        t_hi = lax.dot_general(eye.astype(BF16), hi[:, sl].astype(BF16), nt, preferred_element_type=F32)
        t_lo = lax.dot_general(eye.astype(BF16), lo[:, sl].astype(BF16), nt, preferred_element_type=F32)
        pos_tm_ref[sl, :] = (16.0 * t_hi + t_lo).astype(I32) - 1
        gate_tm_ref[sl, :] = lax.dot_general(eye, gate[:, sl], nt, precision=HIGHEST,
                                             preferred_element_type=F32)


def route(logits, cap):
    b, ne, s = logits.shape
    return pl.pallas_call(
        functools.partial(_route_kernel, cap=cap),
        grid=(b,),
        in_specs=[pl.BlockSpec((None, ne, s), lambda i: (i, 0, 0))],
        out_specs=[pl.BlockSpec((None, ne, s), lambda i: (i, 0, 0)),
                   pl.BlockSpec((None, s, ne), lambda i: (i, 0, 0)),
                   pl.BlockSpec((None, s, ne), lambda i: (i, 0, 0))],
        out_shape=[jax.ShapeDtypeStruct((b, ne, s), I32), jax.ShapeDtypeStruct((b, s, ne), I32),
                   jax.ShapeDtypeStruct((b, s, ne), F32)],
        scratch_shapes=[pltpu.VMEM((ne, s), F32)],
        compiler_params=_cparams(("arbitrary",)),
        name="route",
    )(logits)


def _gather_kernel(pos_ref, h_ref, o_ref, *, tk):
    cap = o_ref.shape[0]
    s = h_ref.shape[0]
    slot = lax.broadcasted_iota(I32, (cap, tk), 0)
    acc = jnp.zeros(o_ref.shape, F32)
    for c in range(s // tk):
        sl = slice(c * tk, (c + 1) * tk)
        onehot = jnp.where(pos_ref[:, sl] == slot, 1.0, 0.0).astype(BF16)
        acc = acc + jnp.dot(onehot, h_ref[sl, :], preferred_element_type=F32)
    o_ref[...] = acc.astype(o_ref.dtype)


def moe_gather(pos_em, h2, cap, tk=512):
    b, ne, s = pos_em.shape
    d = h2.shape[2]
    return pl.pallas_call(
        functools.partial(_gather_kernel, tk=tk),
        grid=(b, ne),
        in_specs=[pl.BlockSpec((None, None, 1, s), lambda i, e: (i, e, 0, 0)),
                  pl.BlockSpec((None, s, d), lambda i, e: (i, 0, 0))],
        out_specs=pl.BlockSpec((None, cap, d), lambda i, e: (e, i, 0)),
        out_shape=jax.ShapeDtypeStruct((ne, b * cap, d), BF16),
        compiler_params=_cparams(("arbitrary", "arbitrary")),
        name="moe_gather",
    )(pos_em.reshape(b, ne, 1, s), h2)


def _ffn_kernel(x_ref, wg_ref, wu_ref, wd_ref, o_ref, acc_ref):
    f = pl.program_id(2)
    x = x_ref[...]
    g = jnp.dot(x, wg_ref[...].astype(BF16), preferred_element_type=F32)
    u = jnp.dot(x, wu_ref[...].astype(BF16), preferred_element_type=F32)
    h = (_silu(g) * u).astype(BF16)
    y = jnp.dot(h, wd_ref[...].astype(BF16), preferred_element_type=F32)

    @pl.when(f == 0)
    def _():
        acc_ref[...] = y

    @pl.when(f != 0)
    def _():
        acc_ref[...] += y

    @pl.when(f == pl.num_programs(2) - 1)
    def _():
        o_ref[...] = acc_ref[...].astype(o_ref.dtype)


def moe_ffn(xe, w_gate, w_up, w_down, tm=2048, tf=256):
    ne, m, d = xe.shape
    ff = w_gate.shape[2]
    tm = min(tm, m)
    return pl.pallas_call(
        _ffn_kernel,
        grid=(ne, m // tm, ff // tf),
        in_specs=[pl.BlockSpec((None, tm, d), lambda e, i, f: (e, i, 0)),
                  pl.BlockSpec((None, d, tf), lambda e, i, f: (e, 0, f)),
                  pl.BlockSpec((None, d, tf), lambda e, i, f: (e, 0, f)),
                  pl.BlockSpec((None, tf, d), lambda e, i, f: (e, f, 0))],
        out_specs=pl.BlockSpec((None, tm, d), lambda e, i, f: (e, i, 0)),
        out_shape=jax.ShapeDtypeStruct((ne, m, d), BF16),
        scratch_shapes=[pltpu.VMEM((tm, d), F32)],
        compiler_params=_cparams(("arbitrary", "arbitrary", "arbitrary"), VMEM_LIMIT_BYTES),
        name="moe_ffn",
    )(xe, w_gate, w_up, w_down)


def _combine_kernel(pos_ref, gate_ref, ye_ref, x1_ref, g2_ref, fg_ref, o_ref, acc_ref):
    e = pl.program_id(2)
    tt, ne = pos_ref.shape
    cap = ye_ref.shape[0]

    @pl.when(e == 0)
    def _():
        acc_ref[...] = jnp.zeros_like(acc_ref)

    col = lax.broadcasted_iota(I32, (tt, ne), 1) == e
    pe = jnp.sum(jnp.where(col, pos_ref[...], 0), axis=1, keepdims=True)
    ge = jnp.sum(jnp.where(col, gate_ref[...], 0.0), axis=1, keepdims=True)
    slot = lax.broadcasted_iota(I32, (tt, cap), 1)
    onehot = jnp.where(pe == slot, ge, 0.0).astype(BF16)
    acc_ref[...] += jnp.dot(onehot, ye_ref[...], preferred_element_type=F32)

    @pl.when(e == pl.num_programs(2) - 1)
    def _():
        xo = x1_ref[...] + g2_ref[...] * acc_ref[...]
        o_ref[...] = _rmsnorm(xo, fg_ref[...])


def moe_combine(pos_tm, gate_tm, ye, x1, g2, final_g, cap, tt=1024):
    b, s, ne = pos_tm.shape
    d = x1.shape[2]
    return pl.pallas_call(
        _combine_kernel,
        grid=(b, s // tt, ne),
        in_specs=[pl.BlockSpec((None, tt, ne), lambda i, j, e: (i, j, 0)),
                  pl.BlockSpec((None, tt, ne), lambda i, j, e: (i, j, 0)),
                  pl.BlockSpec((None, cap, d), lambda i, j, e: (e, i, 0)),
                  pl.BlockSpec((None, tt, d), lambda i, j, e: (i, j, 0)),
                  pl.BlockSpec((None, 1, d), lambda i, j, e: (i, 0, 0)),
                  pl.BlockSpec((1, d), lambda i, j, e: (0, 0))],
        out_specs=pl.BlockSpec((None, tt, d), lambda i, j, e: (i, j, 0)),
        out_shape=jax.ShapeDtypeStruct((b, s, d), F32),
        scratch_shapes=[pltpu.VMEM((tt, d), F32)],
        compiler_params=_cparams(("arbitrary", "arbitrary", "arbitrary")),
        name="moe_combine",
    )(pos_tm, gate_tm, ye, x1, g2, final_g.reshape(1, d))


def _layer(x, ctx, mods, norm1_g, norm2_g, w_in, w_out, conv_w, conv_b, filt, hy_bias, s5p,
           s5_c_re, s5_c_im, s5_d, s5_glu_w, s5_glu_b, router_w, ex_w_gate, ex_w_up, ex_w_down,
           final_g):
    b, s, d = x.shape
    n_order, hw = hy_bias.shape
    hy_cols = (n_order + 1) * hw
    sw = w_in.shape[1] - hy_cols
    rows = s // GRID_W
    ne = router_w.shape[1]
    cap = CAPACITY_FACTOR * s // ne

    per_b = lambda k: mods[:b, k * d:(k + 1) * d].reshape(b, 1, d)
    ctx_v = lambda k: jnp.broadcast_to(mods[b:b + 1, k * d:(k + 1) * d].reshape(1, 1, d), (b, 1, d))
    sh1, sc1, g1, sh2, sc2, g2 = [per_b(k) for k in range(N_MOD)]

    w_in_bf = w_in.astype(BF16)
    (u_ctx,) = inproj(ctx, norm1_g, ctx_v(0), ctx_v(1), w_in_bf[:, hy_cols:], (sw,))
    z_hy, u = inproj(x, norm1_g, sh1, sc1, w_in_bf, (hy_cols, sw))

    hraw = hyena_filter_mlp(s, *filt)
    kf = hyena_filter_spectra(hraw, n_order, hw)
    cw = lambda k: conv_w[:, k * hw:(k + 1) * hw]
    cb = lambda k: conv_b[k * hw:(k + 1) * hw]
    y1 = hyena_conv(z_hy, 0, z_hy, 1, kf[0], hy_bias[0], cw(0), cb(0), cw(1), cb(1), rows, True)
    hy = hyena_conv(y1, 0, z_hy, 2, kf[1], hy_bias[1], cw(0), cb(0), cw(2), cb(2), rows, False)

    lam_r, lam_i, bb_r, bb_i = s5p
    tmaj = lambda a: jnp.transpose(a, (1, 0, 2)).reshape(-1, sw)
    uf = tmaj(jnp.concatenate([u_ctx, u], axis=1))
    ub = tmaj(jnp.concatenate([u, u_ctx], axis=1))
    yf, yb = s5_scan(uf, ub, lam_r, lam_i, bb_r, bb_i, s5_c_re, s5_c_im, b)
    nctx = ctx.shape[1]
    bmaj = lambda a: jnp.transpose(a.reshape(-1, b, sw), (1, 0, 2))
    yf = bmaj(yf)[:, nctx:]
    yb = bmaj(yb)[:, :s]

    x1, h2, logits = mixer_tail(x, hy, yf, yb, u, s5_d, s5_glu_w, s5_glu_b, w_out, g1, norm2_g,
                                sh2, sc2, router_w)
    pos_em, pos_tm, gate_tm = route(logits, cap)
    xe = moe_gather(pos_em, h2, cap)
    ye = moe_ffn(xe, ex_w_gate, ex_w_up, ex_w_down)
    return moe_combine(pos_tm, gate_tm, ye, x1, g2, final_g, cap)


def kernel(x, c, ctx, c_ctx, mod_w, mod_b, norm1_g, norm2_g, w_in, w_out, conv_w, conv_b, hy_w1, hy_b1, hy_freq, hy_w2, hy_b2, hy_w3, hy_b3, hy_bias, s5_lam_re, s5_lam_im, s5_log_step, s5_b_re, s5_b_im, s5_c_re, s5_c_im, s5_d, s5_glu_w, s5_glu_b, router_w, ex_w_gate, ex_w_up, ex_w_down, final_g):
    depth = mod_w.shape[0]
    assert depth == 1, "context-token updates of non-final layers are not implemented"
    b, _, d = x.shape
    l = 0
    pad = (-(b + 1)) % SUBLANES
    cond = jnp.concatenate([c, c_ctx[None], jnp.zeros((pad, d), F32)], axis=0)
    mods = adaln_mods(cond, mod_w[l], mod_b[l])
    filt = (hy_w1[l], hy_b1[l], hy_freq[l], hy_w2[l], hy_b2[l], hy_w3[l], hy_b3[l])
    s5p = s5_discretize(s5_lam_re[l], s5_lam_im[l], s5_log_step[l], s5_b_re[l], s5_b_im[l])
    return _layer(x, ctx, mods, norm1_g[l], norm2_g[l], w_in[l], w_out[l], conv_w[l], conv_b[l],
                  filt, hy_bias[l], s5p, s5_c_re[l], s5_c_im[l], s5_d[l], s5_glu_w[l], s5_glu_b[l],
                  router_w[l], ex_w_gate[l], ex_w_up[l], ex_w_down[l], final_g)
```

```python
import functools
import math

import numpy as np
import jax
import jax.numpy as jnp
from jax import lax
from jax.experimental import pallas as pl
from jax.experimental.pallas import tpu as pltpu

F32 = jnp.float32
BF16 = jnp.bfloat16
I32 = jnp.int32
HIGHEST = lax.Precision.HIGHEST

SUBLANES = 8
LANES = 128
VMEM_LIMIT_BYTES = 58 * 1024 * 1024

GRID_W = 64
N_MOD = 6
NORM_EPS = 1e-6
POS_BANDS = 16
DECAY_FAST = 0.3
DECAY_SLOW = 1.5
DECAY_TARGET = 1e-2
CAPACITY_FACTOR = 2

DFT_N1 = 64
DFT_N2 = 128
DFT_K1 = DFT_N1 // 2 + 1
DFT_ROWS = 2 * DFT_K1


def _cparams(sem, vmem=None):
    return pltpu.CompilerParams(dimension_semantics=sem, vmem_limit_bytes=vmem)


def _silu(x):
    return x * (1.0 / (1.0 + jnp.exp(-x)))


def _rmsnorm(x, g):
    ms = jnp.mean(x * x, axis=-1, keepdims=True)
    return x * lax.rsqrt(ms + NORM_EPS) * g


def _adaln_kernel(c_ref, w_ref, b_ref, o_ref):
    s = _silu(c_ref[...])
    o_ref[...] = jnp.dot(s, w_ref[...], precision=HIGHEST, preferred_element_type=F32) + b_ref[...]


def adaln_mods(cond, mod_w, mod_b, tn=1536):
    rows, d = cond.shape
    n = mod_w.shape[1]
    return pl.pallas_call(
        _adaln_kernel,
        grid=(n // tn,),
        in_specs=[pl.BlockSpec((rows, d), lambda j: (0, 0)),
                  pl.BlockSpec((d, tn), lambda j: (0, j)),
                  pl.BlockSpec((1, tn), lambda j: (0, j))],
        out_specs=pl.BlockSpec((rows, tn), lambda j: (0, j)),
        out_shape=jax.ShapeDtypeStruct((rows, n), F32),
        compiler_params=_cparams(("arbitrary",)),
        name="adaln",
    )(cond, mod_w, mod_b.reshape(1, n))


def _inproj_kernel(x_ref, g_ref, sh_ref, sc_ref, w_ref, *o_refs, splits):
    h = _rmsnorm(x_ref[...], g_ref[...])
    h = h * (1.0 + sc_ref[...]) + sh_ref[...]
    z = jnp.dot(h.astype(BF16), w_ref[...], preferred_element_type=F32)
    off = 0
    for o_ref, n in zip(o_refs, splits):
        o_ref[...] = z[:, off:off + n]
        off += n


def inproj(x, g, shift, scale, w_bf16, splits, ts=512):
    b, s, d = x.shape
    n = w_bf16.shape[1]
    ts = min(ts, s)
    return pl.pallas_call(
        functools.partial(_inproj_kernel, splits=splits),
        grid=(b, s // ts),
        in_specs=[pl.BlockSpec((None, ts, d), lambda i, j: (i, j, 0)),
                  pl.BlockSpec((1, d), lambda i, j: (0, 0)),
                  pl.BlockSpec((None, 1, d), lambda i, j: (i, 0, 0)),
                  pl.BlockSpec((None, 1, d), lambda i, j: (i, 0, 0)),
                  pl.BlockSpec((d, n), lambda i, j: (0, 0))],
        out_specs=[pl.BlockSpec((None, ts, m), lambda i, j: (i, j, 0)) for m in splits],
        out_shape=[jax.ShapeDtypeStruct((b, s, m), F32) for m in splits],
        compiler_params=_cparams(("arbitrary", "arbitrary")),
        name="inproj",
    )(x, g.reshape(1, d), shift, scale, w_bf16)


def _filt_mlp_kernel(w1t_ref, w1c_ref, w1s_ref, b1_ref, fr_ref, w2_ref, b2_ref, w3_ref, b3_ref,
                     o_ref, *, length, tl):
    i0 = pl.program_id(0) * tl
    idx = (lax.broadcasted_iota(I32, (tl, 1), 0) + i0).astype(F32)
    t = idx / float(length - 1)
    omega = (2.0 * math.pi) * idx / float(length)
    fstep = ((POS_BANDS - 1) - 1e-4) / (POS_BANDS - 1)
    f = 1e-4 + lax.broadcasted_iota(I32, (1, POS_BANDS), 1).astype(F32) * fstep
    arg = omega * f
    pre = (t * w1t_ref[...]
           + jnp.dot(jnp.cos(arg), w1c_ref[...], precision=HIGHEST, preferred_element_type=F32)
           - jnp.dot(jnp.sin(arg), w1s_ref[...], precision=HIGHEST, preferred_element_type=F32)
           + b1_ref[...])
    fr = fr_ref[...]
    h = jnp.sin(fr[0:1, :] * pre)
    h = jnp.sin(fr[1:2, :] * (jnp.dot(h, w2_ref[...], precision=HIGHEST,
                                      preferred_element_type=F32) + b2_ref[...]))
    o_ref[...] = jnp.dot(h, w3_ref[...], precision=HIGHEST, preferred_element_type=F32) + b3_ref[...]


def hyena_filter_mlp(length, w1, b1, freq, w2, b2, w3, b3, tl=512):
    fw = w1.shape[1]
    n = w3.shape[1]
    full = lambda shape: pl.BlockSpec(shape, lambda i: (0, 0))
    return pl.pallas_call(
        functools.partial(_filt_mlp_kernel, length=length, tl=tl),
        grid=(length // tl,),
        in_specs=[full((1, fw)), full((POS_BANDS, fw)), full((POS_BANDS, fw)), full((1, fw)),
                  full((2, fw)), full((fw, fw)), full((1, fw)), full((fw, n)), full((1, n))],
        out_specs=pl.BlockSpec((tl, n), lambda i: (i, 0)),
        out_shape=jax.ShapeDtypeStruct((length, n), F32),
        compiler_params=_cparams(("arbitrary",)),
        name="hyena_filter_mlp",
    )(w1[0:1], w1[1:1 + POS_BANDS], w1[1 + POS_BANDS:], b1.reshape(1, fw), freq, w2,
      b2.reshape(1, fw), w3, b3.reshape(1, n))


@functools.lru_cache(maxsize=None)
def _dft_tables():
    n1n, n2n, k1n = DFT_N1, DFT_N2, DFT_K1
    n = n1n * n2n
    half = n1n // 2
    k1 = np.arange(k1n)[:, None]
    n1 = np.arange(half)[None, :]
    th = 2.0 * np.pi * k1 * n1 / n1n
    f1 = np.zeros((DFT_ROWS, half))
    f1[0::2] = np.cos(th)
    f1[1::2] = -np.sin(th)
    wgt = np.where((k1 == 0) | (k1 == half), 1.0, 2.0)
    g1 = np.zeros((half, DFT_ROWS))
    g1[:, 0::2] = (wgt * np.cos(th)).T / n
    g1[:, 1::2] = (-wgt * np.sin(th)).T / n
    eye = np.eye(SUBLANES)
    fk = np.kron(f1, eye)
    gk = np.kron(g1, eye)
    k2 = np.arange(n2n)[:, None]
    n2 = np.arange(n2n)[None, :]
    f3 = np.zeros((k1n, 2 * n2n, 2 * n2n))
    for kk in range(k1n):
        ph = 2.0 * np.pi * n2 * (n1n * k2 + kk) / n
        tr, ti = np.cos(ph), -np.sin(ph)
        f3[kk, :n2n, :n2n] = tr
        f3[kk, :n2n, n2n:] = -ti
        f3[kk, n2n:, :n2n] = ti
        f3[kk, n2n:, n2n:] = tr
    g3 = np.transpose(f3, (0, 2, 1))
    to = lambda a: jnp.asarray(a, dtype=F32).astype(BF16)
    return to(fk), to(gk), to(f3), to(g3)


def _dft_stage1(src_ref, a_ref, fk_ref):
    half = DFT_N1 // 2

    def body(m, carry):
        sub = pl.ds(pl.multiple_of(m * SUBLANES, SUBLANES), SUBLANES)
        rows = [src_ref.at[pl.ds(DFT_N2 * n1, DFT_N2)][sub, :] for n1 in range(half)]
        rhs = jnp.concatenate(rows, axis=0).astype(BF16)
        out = jnp.dot(fk_ref[...], rhs, preferred_element_type=F32)
        for j in range(DFT_ROWS):
            a_ref.at[pl.ds(DFT_N2 * j, DFT_N2)][sub, :] = out[SUBLANES * j:SUBLANES * (j + 1)]
        return carry

    lax.fori_loop(0, DFT_N2 // SUBLANES, body, 0, unroll=2)


def _dft_stage3(a_ref, f3_ref, k1):
    r0 = pl.multiple_of(k1 * (2 * DFT_N2), 2 * DFT_N2)
    a = a_ref[pl.ds(r0, 2 * DFT_N2), :].astype(BF16)
    x = jnp.dot(f3_ref[k1], a, preferred_element_type=F32)
    return x[:DFT_N2], x[DFT_N2:]


def _filt_spec_kernel(hf_ref, hb_ref, fk_ref, f3_ref, o_ref, src_ref, af_ref, ab_ref, *, length):
    c = hf_ref.shape[1]
    cb = pl.program_id(1)
    hw = pl.num_programs(1) * c
    row = lax.broadcasted_iota(I32, (length, 1), 0)
    t = row.astype(F32) / float(length - 1)
    ch = (lax.broadcasted_iota(I32, (1, c), 1) + cb * c).astype(F32)
    d0 = math.log(DECAY_TARGET) / DECAY_FAST
    d1 = math.log(DECAY_TARGET) / DECAY_SLOW
    deltas = jnp.abs(d0 + ch * ((d1 - d0) / float(hw - 1)))
    decay = jnp.exp(-t * deltas)
    fwd = hf_ref[...] * decay
    bwd = jnp.where(row == 0, 0.0, hb_ref[...] * decay)
    inv = 1.0 / (jnp.sum(jnp.abs(fwd), axis=0, keepdims=True)
                 + jnp.sum(jnp.abs(bwd), axis=0, keepdims=True))
    src_ref[...] = fwd
    _dft_stage1(src_ref, af_ref, fk_ref)
    src_ref[...] = bwd
    _dft_stage1(src_ref, ab_ref, fk_ref)

    def body(k1, carry):
        fr, fi = _dft_stage3(af_ref, f3_ref, k1)
        br, bi = _dft_stage3(ab_ref, f3_ref, k1)
        o_ref[k1, 0] = ((fr + br) * inv).astype(o_ref.dtype)
        o_ref[k1, 1] = ((fi - bi) * inv).astype(o_ref.dtype)
        return carry

    lax.fori_loop(0, DFT_K1, body, 0)


def hyena_filter_spectra(hraw, n_order, width, c_blk=256):
    length = hraw.shape[0]
    assert 2 * length == DFT_N1 * DFT_N2
    fk, _, f3, _ = _dft_tables()
    ncb = width // c_blk
    return pl.pallas_call(
        functools.partial(_filt_spec_kernel, length=length),
        grid=(n_order, ncb),
        in_specs=[pl.BlockSpec((length, c_blk), lambda o, j: (0, o * 2 * ncb + j)),
                  pl.BlockSpec((length, c_blk), lambda o, j: (0, o * 2 * ncb + ncb + j)),
                  pl.BlockSpec(fk.shape, lambda o, j: (0, 0)),
                  pl.BlockSpec(f3.shape, lambda o, j: (0, 0, 0))],
        out_specs=pl.BlockSpec((None, DFT_K1, 2, DFT_N2, c_blk), lambda o, j: (o, 0, 0, 0, j)),
        out_shape=jax.ShapeDtypeStruct((n_order, DFT_K1, 2, DFT_N2, width), BF16),
        scratch_shapes=[pltpu.VMEM((length, c_blk), F32),
                        pltpu.VMEM((DFT_ROWS * DFT_N2, c_blk), F32),
                        pltpu.VMEM((DFT_ROWS * DFT_N2, c_blk), F32)],
        compiler_params=_cparams(("arbitrary", "arbitrary"), VMEM_LIMIT_BYTES),
        name="hyena_filter_spectrum",
    )(hraw, hraw, fk, f3)


def _short_conv(z, w_ref, b_ref, rows):
    length = z.shape[0]
    pos = lax.broadcasted_iota(I32, (length, 1), 0) % (length // rows)
    zm = jnp.where(pos == 0, 0.0, pltpu.roll(z, 1, 0))
    zp = jnp.where(pos == (length // rows) - 1, 0.0, pltpu.roll(z, length - 1, 0))
    w = w_ref[...]
    return zm * w[0:1, :] + z * w[1:2, :] + zp * w[2:3, :] + b_ref[...]


def _hyena_conv_kernel(s_ref, m_ref, kf_ref, bias_ref, cws_ref, cbs_ref, cwm_ref, cbm_ref,
                       fk_ref, f3_ref, g3_ref, gk_ref, o_ref, src_ref, mul_ref, a_ref, *,
                       rows, conv_signal, group):
    half = DFT_N1 // 2
    blk_rows = 2 * DFT_N2
    sig = s_ref[...]
    if conv_signal:
        sig = _short_conv(sig, cws_ref, cbs_ref, rows)
    src_ref[...] = sig
    mul_ref[...] = _short_conv(m_ref[...], cwm_ref, cbm_ref, rows)
    _dft_stage1(src_ref, a_ref, fk_ref)

    def body3(i, carry):
        k1s = [i * group + q for q in range(group)]
        r0s = [pl.multiple_of(k1 * blk_rows, blk_rows) for k1 in k1s]
        blocks = [a_ref[pl.ds(r0, blk_rows), :].astype(BF16) for r0 in r0s]
        outs = []
        for k1, a in zip(k1s, blocks):
            x = jnp.dot(f3_ref[k1], a, preferred_element_type=F32)
            xr, xi = x[:DFT_N2], x[DFT_N2:]
            kr = kf_ref[k1, 0].astype(F32)
            ki = kf_ref[k1, 1].astype(F32)
            y = jnp.concatenate([xr * kr - xi * ki, xr * ki + xi * kr], axis=0).astype(BF16)
            outs.append(jnp.dot(g3_ref[k1], y, preferred_element_type=F32))
        for r0, o in zip(r0s, outs):
            a_ref[pl.ds(r0, blk_rows), :] = o
        return carry

    lax.fori_loop(0, DFT_K1 // group, body3, 0)
    bias = bias_ref[...]

    def body1(m, carry):
        sub = pl.ds(pl.multiple_of(m * SUBLANES, SUBLANES), SUBLANES)
        blk = [a_ref.at[pl.ds(DFT_N2 * j, DFT_N2)][sub, :] for j in range(DFT_ROWS)]
        rhs = jnp.concatenate(blk, axis=0).astype(BF16)
        out = jnp.dot(gk_ref[...], rhs, preferred_element_type=F32)
        for n1 in range(half):
            blk_n1 = pl.ds(DFT_N2 * n1, DFT_N2)
            conv = out[SUBLANES * n1:SUBLANES * (n1 + 1)]
            o_ref.at[blk_n1][sub, :] = mul_ref.at[blk_n1][sub, :] * (
                conv + src_ref.at[blk_n1][sub, :] * bias)
        return carry

    lax.fori_loop(0, DFT_N2 // SUBLANES, body1, 0, unroll=2)


def hyena_conv(sig, sig_col, mul, mul_col, kf, bias, cw_s, cb_s, cw_m, cb_m, rows, conv_signal,
               c_blk=256, group=11):
    b, length, _ = sig.shape
    width = kf.shape[-1]
    ncb = width // c_blk
    assert DFT_K1 % group == 0
    fk, gk, f3, g3 = _dft_tables()
    once = pl.Buffered(1)
    const2 = lambda a: pl.BlockSpec(a.shape, lambda j, i: (0, 0), pipeline_mode=once)
    const3 = lambda a: pl.BlockSpec(a.shape, lambda j, i: (0, 0, 0), pipeline_mode=once)
    chan = lambda r: pl.BlockSpec((r, c_blk), lambda j, i: (0, j), pipeline_mode=once)
    return pl.pallas_call(
        functools.partial(_hyena_conv_kernel, rows=rows, conv_signal=conv_signal, group=group),
        grid=(ncb, b),
        in_specs=[pl.BlockSpec((None, length, c_blk), lambda j, i: (i, 0, sig_col * ncb + j)),
                  pl.BlockSpec((None, length, c_blk), lambda j, i: (i, 0, mul_col * ncb + j)),
                  pl.BlockSpec((DFT_K1, 2, DFT_N2, c_blk), lambda j, i: (0, 0, 0, j),
                               pipeline_mode=once),
                  chan(1), chan(3), chan(1), chan(3), chan(1),
                  const2(fk), const3(f3), const3(g3), const2(gk)],
        out_specs=pl.BlockSpec((None, length, c_blk), lambda j, i: (i, 0, j)),
        out_shape=jax.ShapeDtypeStruct((b, length, width), F32),
        scratch_shapes=[pltpu.VMEM((length, c_blk), F32),
                        pltpu.VMEM((length, c_blk), F32),
                        pltpu.VMEM((DFT_ROWS * DFT_N2, c_blk), F32)],
        compiler_params=_cparams(("arbitrary", "arbitrary"), VMEM_LIMIT_BYTES),
        name="hyena_conv",
    )(sig, mul, kf, bias.reshape(1, width), cw_s, cb_s.reshape(1, width), cw_m,
      cb_m.reshape(1, width), fk, f3, g3, gk)


def _s5_disc_kernel(lr_ref, li_ref, dt_ref, br_ref, bi_ref, ar_ref, ai_ref, bbr_ref, bbi_ref):
    lr, li, dt = lr_ref[...], li_ref[...], jnp.exp(dt_ref[...])
    mag = jnp.exp(lr * dt)
    ar = mag * jnp.cos(li * dt)
    ai = mag * jnp.sin(li * dt)
    den = 1.0 / (lr * lr + li * li)
    qr = ((ar - 1.0) * lr + ai * li) * den
    qi = (ai * lr - (ar - 1.0) * li) * den
    br, bi = br_ref[...], bi_ref[...]
    ar_ref[...] = ar
    ai_ref[...] = ai
    bbr_ref[...] = qr * br - qi * bi
    bbi_ref[...] = qr * bi + qi * br


def s5_discretize(lam_re, lam_im, log_step, b_re, b_im):
    nd, g, p, h = b_re.shape
    rep = lambda a: jnp.broadcast_to(a[..., None], (nd, g, p, h)).reshape(nd * g, p * h)
    dt = jnp.broadcast_to(log_step[:, :, None, None], (nd, g, p, h)).reshape(nd * g, p * h)
    flat = lambda a: a.reshape(nd * g, p * h)
    shp = jax.ShapeDtypeStruct((nd * g, p * h), F32)
    ar, ai, bbr, bbi = pl.pallas_call(
        _s5_disc_kernel, out_shape=[shp] * 4, name="s5_discretize",
    )(rep(lam_re), rep(lam_im), dt, flat(b_re), flat(b_im))
    un = lambda a: a.reshape(nd, g, p, h)
    return un(ar)[..., 0], un(ai)[..., 0], un(bbr), un(bbi)


def _s5_scan_kernel(uf_ref, ub_ref, bf_ref, bb_ref, cf_ref, cb_ref, lam_ref, yf_ref, yb_ref,
                    xf_ref, xb_ref, st_ref, *, steps, nk):
    half = xf_ref.shape[1] // 2
    sk = half // nk
    ck = uf_ref.shape[1] // nk

    @pl.when(pl.program_id(0) == 0)
    def _():
        st_ref[...] = jnp.zeros_like(st_ref)

    def project_in(u_ref, w_ref, x_ref):
        for k in range(nk):
            uk = u_ref[:, k * ck:(k + 1) * ck].astype(BF16)
            x_ref[:, k * sk:(k + 1) * sk] = jnp.dot(uk, w_ref[0, k], preferred_element_type=F32)
            x_ref[:, half + k * sk:half + (k + 1) * sk] = jnp.dot(uk, w_ref[1, k],
                                                                  preferred_element_type=F32)

    def scan(x_ref, d, reverse):
        lr = lam_ref[d, 0]
        li = lam_ref[d, 1]

        def body(t, carry):
            sr, si = carry
            tt = (steps - 1 - t) if reverse else t
            r = pl.ds(pl.multiple_of(tt * SUBLANES, SUBLANES), SUBLANES)
            nr = lr * sr - li * si + x_ref[r, :half]
            ni = lr * si + li * sr + x_ref[r, half:]
            x_ref[r, :half] = nr
            x_ref[r, half:] = ni
            return nr, ni

        sr, si = lax.fori_loop(0, steps, body, (st_ref[d, 0], st_ref[d, 1]))
        st_ref[d, 0] = sr
        st_ref[d, 1] = si

    def project_out(x_ref, w_ref, y_ref):
        for k in range(nk):
            xr = x_ref[:, k * sk:(k + 1) * sk].astype(BF16)
            xi = x_ref[:, half + k * sk:half + (k + 1) * sk].astype(BF16)
            y_ref[:, k * ck:(k + 1) * ck] = (
                jnp.dot(xr, w_ref[0, k], preferred_element_type=F32)
                + jnp.dot(xi, w_ref[1, k], preferred_element_type=F32))

    project_in(uf_ref, bf_ref, xf_ref)
    scan(xf_ref, 0, False)
    project_out(xf_ref, cf_ref, yf_ref)
    project_in(ub_ref, bb_ref, xb_ref)
    scan(xb_ref, 1, True)
    project_out(xb_ref, cb_ref, yb_ref)


def _block_diag_in(bb, nk):
    g, p, h = bb.shape
    gpk = g // nk
    eye = jnp.eye(gpk, dtype=bb.dtype)
    w = jnp.einsum('kaph,ab->kahbp', bb.reshape(nk, gpk, p, h), eye)
    return w.reshape(nk, gpk * h, gpk * p)


def _block_diag_out(cc, nk):
    g, h, p = cc.shape
    gpk = g // nk
    eye = jnp.eye(gpk, dtype=cc.dtype)
    w = jnp.einsum('kahp,ab->kapbh', cc.reshape(nk, gpk, h, p), eye)
    return w.reshape(nk, gpk * p, gpk * h)


def s5_scan(uf, ub, lam_r, lam_i, bb_r, bb_i, c_re, c_im, batch, steps=64):
    rows, width = uf.shape
    nk = width // LANES
    _, g, p = lam_r.shape
    nst = g * p
    rpc = steps * batch
    nchunk = rows // rpc
    w_in = jnp.stack([jnp.stack([_block_diag_in(bb_r[d], nk), _block_diag_in(bb_i[d], nk)])
                      for d in range(2)]).astype(BF16)
    w_out = jnp.stack([jnp.stack([_block_diag_out(c_re[d], nk), -_block_diag_out(c_im[d], nk)])
                       for d in range(2)]).astype(BF16)
    lam = jnp.stack([lam_r.reshape(2, nst), lam_i.reshape(2, nst)], axis=1)
    lam = jnp.broadcast_to(lam[:, :, None, :], (2, 2, batch, nst))
    fwd = pl.BlockSpec((rpc, width), lambda i: (i, 0))
    bwd = pl.BlockSpec((rpc, width), lambda i: (nchunk - 1 - i, 0))
    shp = jax.ShapeDtypeStruct((rows, width), F32)
    return pl.pallas_call(
        functools.partial(_s5_scan_kernel, steps=steps, nk=nk),
        grid=(nchunk,),
        in_specs=[fwd, bwd,
                  pl.BlockSpec((None,) + w_in.shape[1:], lambda i: (0, 0, 0, 0, 0)),
                  pl.BlockSpec((None,) + w_in.shape[1:], lambda i: (1, 0, 0, 0, 0)),
                  pl.BlockSpec((None,) + w_out.shape[1:], lambda i: (0, 0, 0, 0, 0)),
                  pl.BlockSpec((None,) + w_out.shape[1:], lambda i: (1, 0, 0, 0, 0)),
                  pl.BlockSpec(lam.shape, lambda i: (0, 0, 0, 0))],
        out_specs=[fwd, bwd],
        out_shape=[shp, shp],
        scratch_shapes=[pltpu.VMEM((rpc, 2 * nst), F32), pltpu.VMEM((rpc, 2 * nst), F32),
                        pltpu.VMEM((2, 2, batch, nst), F32)],
        compiler_params=_cparams(("arbitrary",), VMEM_LIMIT_BYTES),
        name="s5_scan",
    )(uf, ub, w_in, w_in, w_out, w_out, lam)


def _mixer_tail_kernel(x_ref, hy_ref, yf_ref, yb_ref, u_ref, d_ref, gw_ref, gb_ref, woh_ref, wos_ref,
                       g1_ref, n2_ref, sh2_ref, sc2_ref, rwt_ref, x1_ref, h2_ref, lg_ref):
    y = yf_ref[...] + yb_ref[...] + d_ref[...] * u_ref[...]
    y = 0.5 * y * (1.0 + jnp.tanh(math.sqrt(2.0 / math.pi) * (y + 0.044715 * (y * y * y))))
    gate = jnp.dot(y.astype(BF16), gw_ref[...], preferred_element_type=F32) + gb_ref[...]
    s5 = y * (1.0 / (1.0 + jnp.exp(-gate)))
    mix = (jnp.dot(hy_ref[...].astype(BF16), woh_ref[...], preferred_element_type=F32)
           + jnp.dot(s5.astype(BF16), wos_ref[...], preferred_element_type=F32))
    x1 = x_ref[...] + g1_ref[...] * mix
    x1_ref[...] = x1
    h2 = _rmsnorm(x1, n2_ref[...]) * (1.0 + sc2_ref[...]) + sh2_ref[...]
    h2_ref[...] = h2.astype(BF16)
    lg_ref[...] = lax.dot_general(rwt_ref[...], h2, (((1,), (1,)), ((), ())), precision=HIGHEST,
                                  preferred_element_type=F32)


def mixer_tail(x, hy, yf, yb, u, s5_d, glu_w, glu_b, w_out, g1, norm2_g, sh2, sc2, router_w, ts=512):
    b, s, d = x.shape
    hw = hy.shape[2]
    sw = u.shape[2]
    ne = router_w.shape[1]
    tok = lambda n: pl.BlockSpec((None, ts, n), lambda i, j: (i, j, 0))
    vec = lambda n: pl.BlockSpec((1, n), lambda i, j: (0, 0))
    per_b = pl.BlockSpec((None, 1, d), lambda i, j: (i, 0, 0))
    mat = lambda r, c: pl.BlockSpec((r, c), lambda i, j: (0, 0))
    return pl.pallas_call(
        _mixer_tail_kernel,
        grid=(b, s // ts),
        in_specs=[tok(d), tok(hw), tok(sw), tok(sw), tok(sw), vec(sw), mat(sw, sw), vec(sw),
                  mat(hw, d), mat(sw, d), per_b, vec(d), per_b, per_b, mat(ne, d)],
        out_specs=[tok(d), tok(d), pl.BlockSpec((None, ne, ts), lambda i, j: (i, 0, j))],
        out_shape=[jax.ShapeDtypeStruct((b, s, d), F32), jax.ShapeDtypeStruct((b, s, d), BF16),
                   jax.ShapeDtypeStruct((b, ne, s), F32)],
        compiler_params=_cparams(("arbitrary", "arbitrary")),
        name="mixer_tail",
    )(x, hy, yf, yb, u, s5_d.reshape(1, sw), glu_w.astype(BF16), glu_b.reshape(1, sw),
      w_out[:hw].astype(BF16), w_out[hw:].astype(BF16), g1, norm2_g.reshape(1, d), sh2, sc2,
      router_w.T)


def _lane_cumsum_exclusive(x):
    rows, s = x.shape
    ii = lax.broadcasted_iota(I32, (LANES, LANES), 0)
    jj = lax.broadcasted_iota(I32, (LANES, LANES), 1)
    tri = jnp.where(ii < jj, 1.0, 0.0).astype(BF16)
    carry = jnp.zeros((rows, 1), F32)
    out = []
    for blk in range(s // LANES):
        xb = x[:, blk * LANES:(blk + 1) * LANES]
        out.append(jnp.dot(xb.astype(BF16), tri, preferred_element_type=F32) + carry)
        carry = carry + jnp.sum(xb, axis=1, keepdims=True)
    return jnp.concatenate(out, axis=1)


def _route_kernel(lg_ref, pos_em_ref, pos_tm_ref, gate_tm_ref, aff_ref, *, cap):
    lg = lg_ref[...]
    ne, s = lg.shape
    ex = jnp.exp(lg - jnp.max(lg, axis=0, keepdims=True))
    aff_ref[...] = ex / jnp.sum(ex, axis=0, keepdims=True)
    aff = aff_ref[...]
    count_ge = lambda v, t: jnp.sum(jnp.where(v >= t, 1.0, 0.0), axis=1, keepdims=True)

    def coarse(i, tb):
        cand = tb | jnp.left_shift(jnp.int32(1), 30 - i)
        return jnp.where(count_ge(aff, pltpu.bitcast(cand, F32)) >= cap, cand, tb)

    tb = lax.fori_loop(0, 31, coarse, jnp.zeros((ne, 1), I32))
    t_hi = pltpu.bitcast(tb, F32)
    ulp = pltpu.bitcast(tb + 1, F32) - t_hi
    resid = aff - t_hi

    def fine(j, carry):
        c, step = carry
        cand = c + step
        return jnp.where(count_ge(resid, cand) >= cap, cand, c), step * 0.5

    t_lo, _ = lax.fori_loop(0, 24, fine, (jnp.zeros((ne, 1), F32), ulp * 0.5))
    gt = resid > t_lo
    eq = resid == t_lo
    need = cap - jnp.sum(jnp.where(gt, 1.0, 0.0), axis=1, keepdims=True)
    eq_rank = _lane_cumsum_exclusive(jnp.where(eq, 1.0, 0.0))
    sel = gt | (eq & (eq_rank < need))
    pos = _lane_cumsum_exclusive(jnp.where(sel, 1.0, 0.0))
    posf = jnp.where(sel, pos + 1.0, 0.0)
    gate = jnp.where(sel, aff, 0.0)
    pos_em_ref[...] = posf.astype(I32) - 1
    hi = jnp.floor(posf * (1.0 / 16.0))
    lo = posf - 16.0 * hi
    ii = lax.broadcasted_iota(I32, (LANES, LANES), 0)
    jj = lax.broadcasted_iota(I32, (LANES, LANES), 1)
    eye = jnp.where(ii == jj, 1.0, 0.0)
    nt = (((1,), (1,)), ((), ()))
    for blk in range(s // LANES):
        sl = slice(blk * LANES, (blk + 1) * LANES)
        t_hi = lax.dot_general(eye.astype(BF16), hi[:, sl].astype(BF16), nt, preferred_element_type=F32)
        t_lo = lax.dot_general(eye.astype(BF16), lo[:, sl].astype(BF16), nt, preferred_element_type=F32)
        pos_tm_ref[sl, :] = (16.0 * t_hi + t_lo).astype(I32) - 1
        gate_tm_ref[sl, :] = lax.dot_general(eye, gate[:, sl], nt, precision=HIGHEST,
                                             preferred_element_type=F32)


def route(logits, cap):
    b, ne, s = logits.shape
    return pl.pallas_call(
        functools.partial(_route_kernel, cap=cap),
        grid=(b,),
        in_specs=[pl.BlockSpec((None, ne, s), lambda i: (i, 0, 0))],
        out_specs=[pl.BlockSpec((None, ne, s), lambda i: (i, 0, 0)),
                   pl.BlockSpec((None, s, ne), lambda i: (i, 0, 0)),
                   pl.BlockSpec((None, s, ne), lambda i: (i, 0, 0))],
        out_shape=[jax.ShapeDtypeStruct((b, ne, s), I32), jax.ShapeDtypeStruct((b, s, ne), I32),
                   jax.ShapeDtypeStruct((b, s, ne), F32)],
        scratch_shapes=[pltpu.VMEM((ne, s), F32)],
        compiler_params=_cparams(("arbitrary",)),
        name="route",
    )(logits)


def _gather_kernel(pos_ref, h_ref, o_ref, *, tk):
    cap = o_ref.shape[0]
    s = h_ref.shape[0]
    slot = lax.broadcasted_iota(I32, (cap, tk), 0)
    acc = jnp.zeros(o_ref.shape, F32)
    for c in range(s // tk):
        sl = slice(c * tk, (c + 1) * tk)
        onehot = jnp.where(pos_ref[:, sl] == slot, 1.0, 0.0).astype(BF16)
        acc = acc + jnp.dot(onehot, h_ref[sl, :], preferred_element_type=F32)
    o_ref[...] = acc.astype(o_ref.dtype)


def moe_gather(pos_em, h2, cap, tk=512):
    b, ne, s = pos_em.shape
    d = h2.shape[2]
    return pl.pallas_call(
        functools.partial(_gather_kernel, tk=tk),
        grid=(b, ne),
        in_specs=[pl.BlockSpec((None, None, 1, s), lambda i, e: (i, e, 0, 0)),
                  pl.BlockSpec((None, s, d), lambda i, e: (i, 0, 0))],
        out_specs=pl.BlockSpec((None, cap, d), lambda i, e: (e, i, 0)),
        out_shape=jax.ShapeDtypeStruct((ne, b * cap, d), BF16),
        compiler_params=_cparams(("arbitrary", "arbitrary")),
        name="moe_gather",
    )(pos_em.reshape(b, ne, 1, s), h2)


def _ffn_kernel(x_ref, wg_ref, wu_ref, wd_ref, o_ref, acc_ref):
    f = pl.program_id(2)

    @pl.when(f == 0)
    def _():
        acc_ref[...] = jnp.zeros_like(acc_ref)

    x = x_ref[...]
    g = jnp.dot(x, wg_ref[...].astype(BF16), preferred_element_type=F32)
    u = jnp.dot(x, wu_ref[...].astype(BF16), preferred_element_type=F32)
    h = (_silu(g) * u).astype(BF16)
    acc_ref[...] += jnp.dot(h, wd_ref[...].astype(BF16), preferred_element_type=F32)

    @pl.when(f == pl.num_programs(2) - 1)
    def _():
        o_ref[...] = acc_ref[...].astype(o_ref.dtype)


def moe_ffn(xe, w_gate, w_up, w_down, tm=2048, tf=256):
    ne, m, d = xe.shape
    ff = w_gate.shape[2]
    tm = min(tm, m)
    return pl.pallas_call(
        _ffn_kernel,
        grid=(ne, m // tm, ff // tf),
        in_specs=[pl.BlockSpec((None, tm, d), lambda e, i, f: (e, i, 0)),
                  pl.BlockSpec((None, d, tf), lambda e, i, f: (e, 0, f)),
                  pl.BlockSpec((None, d, tf), lambda e, i, f: (e, 0, f)),
                  pl.BlockSpec((None, tf, d), lambda e, i, f: (e, f, 0))],
        out_specs=pl.BlockSpec((None, tm, d), lambda e, i, f: (e, i, 0)),
        out_shape=jax.ShapeDtypeStruct((ne, m, d), BF16),
        scratch_shapes=[pltpu.VMEM((tm, d), F32)],
        compiler_params=_cparams(("arbitrary", "arbitrary", "arbitrary"), VMEM_LIMIT_BYTES),
        name="moe_ffn",
    )(xe, w_gate, w_up, w_down)


def _combine_kernel(pos_ref, gate_ref, ye_ref, x1_ref, g2_ref, fg_ref, o_ref, acc_ref):
    e = pl.program_id(2)
    tt, ne = pos_ref.shape
    cap = ye_ref.shape[0]

    @pl.when(e == 0)
    def _():
        acc_ref[...] = jnp.zeros_like(acc_ref)

    col = lax.broadcasted_iota(I32, (tt, ne), 1) == e
    pe = jnp.sum(jnp.where(col, pos_ref[...], 0), axis=1, keepdims=True)
    ge = jnp.sum(jnp.where(col, gate_ref[...], 0.0), axis=1, keepdims=True)
    slot = lax.broadcasted_iota(I32, (tt, cap), 1)
    onehot = jnp.where(pe == slot, ge, 0.0).astype(BF16)
    acc_ref[...] += jnp.dot(onehot, ye_ref[...], preferred_element_type=F32)

    @pl.when(e == pl.num_programs(2) - 1)
    def _():
        xo = x1_ref[...] + g2_ref[...] * acc_ref[...]
        o_ref[...] = _rmsnorm(xo, fg_ref[...])


def moe_combine(pos_tm, gate_tm, ye, x1, g2, final_g, cap, tt=1024):
    b, s, ne = pos_tm.shape
    d = x1.shape[2]
    return pl.pallas_call(
        _combine_kernel,
        grid=(b, s // tt, ne),
        in_specs=[pl.BlockSpec((None, tt, ne), lambda i, j, e: (i, j, 0)),
                  pl.BlockSpec((None, tt, ne), lambda i, j, e: (i, j, 0)),
                  pl.BlockSpec((None, cap, d), lambda i, j, e: (e, i, 0)),
                  pl.BlockSpec((None, tt, d), lambda i, j, e: (i, j, 0)),
                  pl.BlockSpec((None, 1, d), lambda i, j, e: (i, 0, 0)),
                  pl.BlockSpec((1, d), lambda i, j, e: (0, 0))],
        out_specs=pl.BlockSpec((None, tt, d), lambda i, j, e: (i, j, 0)),
        out_shape=jax.ShapeDtypeStruct((b, s, d), F32),
        scratch_shapes=[pltpu.VMEM((tt, d), F32)],
        compiler_params=_cparams(("arbitrary", "arbitrary", "arbitrary")),
        name="moe_combine",
    )(pos_tm, gate_tm, ye, x1, g2, final_g.reshape(1, d))


def _layer(x, ctx, mods, norm1_g, norm2_g, w_in, w_out, conv_w, conv_b, filt, hy_bias, s5p,
           s5_c_re, s5_c_im, s5_d, s5_glu_w, s5_glu_b, router_w, ex_w_gate, ex_w_up, ex_w_down,
           final_g):
    b, s, d = x.shape
    n_order, hw = hy_bias.shape
    hy_cols = (n_order + 1) * hw
    sw = w_in.shape[1] - hy_cols
    rows = s // GRID_W
    ne = router_w.shape[1]
    cap = CAPACITY_FACTOR * s // ne

    per_b = lambda k: mods[:b, k * d:(k + 1) * d].reshape(b, 1, d)
    ctx_v = lambda k: jnp.broadcast_to(mods[b:b + 1, k * d:(k + 1) * d].reshape(1, 1, d), (b, 1, d))
    sh1, sc1, g1, sh2, sc2, g2 = [per_b(k) for k in range(N_MOD)]

    w_in_bf = w_in.astype(BF16)
    (u_ctx,) = inproj(ctx, norm1_g, ctx_v(0), ctx_v(1), w_in_bf[:, hy_cols:], (sw,))
    z_hy, u = inproj(x, norm1_g, sh1, sc1, w_in_bf, (hy_cols, sw))

    hraw = hyena_filter_mlp(s, *filt)
    kf = hyena_filter_spectra(hraw, n_order, hw)
    cw = lambda k: conv_w[:, k * hw:(k + 1) * hw]
    cb = lambda k: conv_b[k * hw:(k + 1) * hw]
    y1 = hyena_conv(z_hy, 0, z_hy, 1, kf[0], hy_bias[0], cw(0), cb(0), cw(1), cb(1), rows, True)
    hy = hyena_conv(y1, 0, z_hy, 2, kf[1], hy_bias[1], cw(0), cb(0), cw(2), cb(2), rows, False)

    lam_r, lam_i, bb_r, bb_i = s5p
    tmaj = lambda a: jnp.transpose(a, (1, 0, 2)).reshape(-1, sw)
    uf = tmaj(jnp.concatenate([u_ctx, u], axis=1))
    ub = tmaj(jnp.concatenate([u, u_ctx], axis=1))
    yf, yb = s5_scan(uf, ub, lam_r, lam_i, bb_r, bb_i, s5_c_re, s5_c_im, b)
    nctx = ctx.shape[1]
    bmaj = lambda a: jnp.transpose(a.reshape(-1, b, sw), (1, 0, 2))
    yf = bmaj(yf)[:, nctx:]
    yb = bmaj(yb)[:, :s]

    x1, h2, logits = mixer_tail(x, hy, yf, yb, u, s5_d, s5_glu_w, s5_glu_b, w_out, g1, norm2_g,
                                sh2, sc2, router_w)
    pos_em, pos_tm, gate_tm = route(logits, cap)
    xe = moe_gather(pos_em, h2, cap)
    ye = moe_ffn(xe, ex_w_gate, ex_w_up, ex_w_down)
    return moe_combine(pos_tm, gate_tm, ye, x1, g2, final_g, cap)


def kernel(x, c, ctx, c_ctx, mod_w, mod_b, norm1_g, norm2_g, w_in, w_out, conv_w, conv_b, hy_w1, hy_b1, hy_freq, hy_w2, hy_b2, hy_w3, hy_b3, hy_bias, s5_lam_re, s5_lam_im, s5_log_step, s5_b_re, s5_b_im, s5_c_re, s5_c_im, s5_d, s5_glu_w, s5_glu_b, router_w, ex_w_gate, ex_w_up, ex_w_down, final_g):
    depth = mod_w.shape[0]
    assert depth == 1, "context-token updates of non-final layers are not implemented"
    b, _, d = x.shape
    l = 0
    pad = (-(b + 1)) % SUBLANES
    cond = jnp.concatenate([c, c_ctx[None], jnp.zeros((pad, d), F32)], axis=0)
    mods = adaln_mods(cond, mod_w[l], mod_b[l])
    filt = (hy_w1[l], hy_b1[l], hy_freq[l], hy_w2[l], hy_b2[l], hy_w3[l], hy_b3[l])
    s5p = s5_discretize(s5_lam_re[l], s5_lam_im[l], s5_log_step[l], s5_b_re[l], s5_b_im[l])
    return _layer(x, ctx, mods, norm1_g[l], norm2_g[l], w_in[l], w_out[l], conv_w[l], conv_b[l],
                  filt, hy_bias[l], s5p, s5_c_re[l], s5_c_im[l], s5_d[l], s5_glu_w[l], s5_glu_b[l],
                  router_w[l], ex_w_gate[l], ex_w_up[l], ex_w_down[l], final_g)
```

```python
import functools
import math

import numpy as np
import jax
import jax.numpy as jnp
from jax import lax
from jax.experimental import pallas as pl
from jax.experimental.pallas import tpu as pltpu

F32 = jnp.float32
BF16 = jnp.bfloat16
I32 = jnp.int32
HIGHEST = lax.Precision.HIGHEST

SUBLANES = 8
LANES = 128
VMEM_LIMIT_BYTES = 58 * 1024 * 1024

GRID_W = 64
N_MOD = 6
NORM_EPS = 1e-6
POS_BANDS = 16
DECAY_FAST = 0.3
DECAY_SLOW = 1.5
DECAY_TARGET = 1e-2
CAPACITY_FACTOR = 2

DFT_N1 = 64
DFT_N2 = 128
DFT_K1 = DFT_N1 // 2 + 1
DFT_ROWS = 2 * DFT_K1


def _cparams(sem, vmem=None):
    return pltpu.CompilerParams(dimension_semantics=sem, vmem_limit_bytes=vmem)


def _silu(x):
    return x * (1.0 / (1.0 + jnp.exp(-x)))


def _rmsnorm(x, g):
    ms = jnp.mean(x * x, axis=-1, keepdims=True)
    return x * lax.rsqrt(ms + NORM_EPS) * g


def _adaln_kernel(c_ref, w_ref, b_ref, o_ref):
    s = _silu(c_ref[...])
    o_ref[...] = jnp.dot(s, w_ref[...], precision=HIGHEST, preferred_element_type=F32) + b_ref[...]


def adaln_mods(cond, mod_w, mod_b, tn=1536):
    rows, d = cond.shape
    n = mod_w.shape[1]
    return pl.pallas_call(
        _adaln_kernel,
        grid=(n // tn,),
        in_specs=[pl.BlockSpec((rows, d), lambda j: (0, 0)),
                  pl.BlockSpec((d, tn), lambda j: (0, j)),
                  pl.BlockSpec((1, tn), lambda j: (0, j))],
        out_specs=pl.BlockSpec((rows, tn), lambda j: (0, j)),
        out_shape=jax.ShapeDtypeStruct((rows, n), F32),
        compiler_params=_cparams(("arbitrary",)),
        name="adaln",
    )(cond, mod_w, mod_b.reshape(1, n))


def _inproj_kernel(x_ref, g_ref, sh_ref, sc_ref, w_ref, *o_refs, splits):
    h = _rmsnorm(x_ref[...], g_ref[...])
    h = h * (1.0 + sc_ref[...]) + sh_ref[...]
    z = jnp.dot(h.astype(BF16), w_ref[...], preferred_element_type=F32)
    off = 0
    for o_ref, n in zip(o_refs, splits):
        o_ref[...] = z[:, off:off + n]
        off += n


def inproj(x, g, shift, scale, w_bf16, splits, ts=512):
    b, s, d = x.shape
    n = w_bf16.shape[1]
    ts = min(ts, s)
    return pl.pallas_call(
        functools.partial(_inproj_kernel, splits=splits),
        grid=(b, s // ts),
        in_specs=[pl.BlockSpec((None, ts, d), lambda i, j: (i, j, 0)),
                  pl.BlockSpec((1, d), lambda i, j: (0, 0)),
                  pl.BlockSpec((None, 1, d), lambda i, j: (i, 0, 0)),
                  pl.BlockSpec((None, 1, d), lambda i, j: (i, 0, 0)),
                  pl.BlockSpec((d, n), lambda i, j: (0, 0))],
        out_specs=[pl.BlockSpec((None, ts, m), lambda i, j: (i, j, 0)) for m in splits],
        out_shape=[jax.ShapeDtypeStruct((b, s, m), F32) for m in splits],
        compiler_params=_cparams(("arbitrary", "arbitrary")),
        name="inproj",
    )(x, g.reshape(1, d), shift, scale, w_bf16)


def _filt_mlp_kernel(w1t_ref, w1c_ref, w1s_ref, b1_ref, fr_ref, w2_ref, b2_ref, w3_ref, b3_ref,
                     o_ref, *, length, tl):
    i0 = pl.program_id(0) * tl
    idx = (lax.broadcasted_iota(I32, (tl, 1), 0) + i0).astype(F32)
    t = idx / float(length - 1)
    omega = (2.0 * math.pi) * idx / float(length)
    fstep = ((POS_BANDS - 1) - 1e-4) / (POS_BANDS - 1)
    f = 1e-4 + lax.broadcasted_iota(I32, (1, POS_BANDS), 1).astype(F32) * fstep
    arg = omega * f
    pre = (t * w1t_ref[...]
           + jnp.dot(jnp.cos(arg), w1c_ref[...], precision=HIGHEST, preferred_element_type=F32)
           - jnp.dot(jnp.sin(arg), w1s_ref[...], precision=HIGHEST, preferred_element_type=F32)
           + b1_ref[...])
    fr = fr_ref[...]
    h = jnp.sin(fr[0:1, :] * pre)
    h = jnp.sin(fr[1:2, :] * (jnp.dot(h, w2_ref[...], precision=HIGHEST,
                                      preferred_element_type=F32) + b2_ref[...]))
    o_ref[...] = jnp.dot(h, w3_ref[...], precision=HIGHEST, preferred_element_type=F32) + b3_ref[...]


def hyena_filter_mlp(length, w1, b1, freq, w2, b2, w3, b3, tl=512):
    fw = w1.shape[1]
    n = w3.shape[1]
    full = lambda shape: pl.BlockSpec(shape, lambda i: (0, 0))
    return pl.pallas_call(
        functools.partial(_filt_mlp_kernel, length=length, tl=tl),
        grid=(length // tl,),
        in_specs=[full((1, fw)), full((POS_BANDS, fw)), full((POS_BANDS, fw)), full((1, fw)),
                  full((2, fw)), full((fw, fw)), full((1, fw)), full((fw, n)), full((1, n))],
        out_specs=pl.BlockSpec((tl, n), lambda i: (i, 0)),
        out_shape=jax.ShapeDtypeStruct((length, n), F32),
        compiler_params=_cparams(("arbitrary",)),
        name="hyena_filter_mlp",
    )(w1[0:1], w1[1:1 + POS_BANDS], w1[1 + POS_BANDS:], b1.reshape(1, fw), freq, w2,
      b2.reshape(1, fw), w3, b3.reshape(1, n))


@functools.lru_cache(maxsize=None)
def _dft_tables():
    n1n, n2n, k1n = DFT_N1, DFT_N2, DFT_K1
    n = n1n * n2n
    half = n1n // 2
    k1 = np.arange(k1n)[:, None]
    n1 = np.arange(half)[None, :]
    th = 2.0 * np.pi * k1 * n1 / n1n
    f1 = np.zeros((DFT_ROWS, half))
    f1[0::2] = np.cos(th)
    f1[1::2] = -np.sin(th)
    wgt = np.where((k1 == 0) | (k1 == half), 1.0, 2.0)
    g1 = np.zeros((half, DFT_ROWS))
    g1[:, 0::2] = (wgt * np.cos(th)).T / n
    g1[:, 1::2] = (-wgt * np.sin(th)).T / n
    eye = np.eye(SUBLANES)
    fk = np.kron(f1, eye)
    gk = np.kron(g1, eye)
    k2 = np.arange(n2n)[:, None]
    n2 = np.arange(n2n)[None, :]
    f3 = np.zeros((k1n, 2 * n2n, 2 * n2n))
    for kk in range(k1n):
        ph = 2.0 * np.pi * n2 * (n1n * k2 + kk) / n
        tr, ti = np.cos(ph), -np.sin(ph)
        f3[kk, :n2n, :n2n] = tr
        f3[kk, :n2n, n2n:] = -ti
        f3[kk, n2n:, :n2n] = ti
        f3[kk, n2n:, n2n:] = tr
    g3 = np.transpose(f3, (0, 2, 1))
    to = lambda a: jnp.asarray(a, dtype=F32).astype(BF16)
    return to(fk), to(gk), to(f3), to(g3)


def _dft_stage1(src_ref, a_ref, fk_ref):
    half = DFT_N1 // 2

    def body(m, carry):
        sub = pl.ds(pl.multiple_of(m * SUBLANES, SUBLANES), SUBLANES)
        rows = [src_ref.at[pl.ds(DFT_N2 * n1, DFT_N2)][sub, :] for n1 in range(half)]
        rhs = jnp.concatenate(rows, axis=0).astype(BF16)
        out = jnp.dot(fk_ref[...], rhs, preferred_element_type=F32)
        for j in range(DFT_ROWS):
            a_ref.at[pl.ds(DFT_N2 * j, DFT_N2)][sub, :] = out[SUBLANES * j:SUBLANES * (j + 1)]
        return carry

    lax.fori_loop(0, DFT_N2 // SUBLANES, body, 0, unroll=2)


def _dft_stage3(a_ref, f3_ref, k1):
    r0 = pl.multiple_of(k1 * (2 * DFT_N2), 2 * DFT_N2)
    a = a_ref[pl.ds(r0, 2 * DFT_N2), :].astype(BF16)
    x = jnp.dot(f3_ref[k1], a, preferred_element_type=F32)
    return x[:DFT_N2], x[DFT_N2:]


def _filt_spec_kernel(hf_ref, hb_ref, fk_ref, f3_ref, o_ref, src_ref, af_ref, ab_ref, *, length):
    c = hf_ref.shape[1]
    cb = pl.program_id(1)
    hw = pl.num_programs(1) * c
    row = lax.broadcasted_iota(I32, (length, 1), 0)
    t = row.astype(F32) / float(length - 1)
    ch = (lax.broadcasted_iota(I32, (1, c), 1) + cb * c).astype(F32)
    d0 = math.log(DECAY_TARGET) / DECAY_FAST
    d1 = math.log(DECAY_TARGET) / DECAY_SLOW
    deltas = jnp.abs(d0 + ch * ((d1 - d0) / float(hw - 1)))
    decay = jnp.exp(-t * deltas)
    fwd = hf_ref[...] * decay
    bwd = jnp.where(row == 0, 0.0, hb_ref[...] * decay)
    inv = 1.0 / (jnp.sum(jnp.abs(fwd), axis=0, keepdims=True)
                 + jnp.sum(jnp.abs(bwd), axis=0, keepdims=True))
    src_ref[...] = fwd
    _dft_stage1(src_ref, af_ref, fk_ref)
    src_ref[...] = bwd
    _dft_stage1(src_ref, ab_ref, fk_ref)

    def body(k1, carry):
        fr, fi = _dft_stage3(af_ref, f3_ref, k1)
        br, bi = _dft_stage3(ab_ref, f3_ref, k1)
        o_ref[k1, 0] = ((fr + br) * inv).astype(o_ref.dtype)
        o_ref[k1, 1] = ((fi - bi) * inv).astype(o_ref.dtype)
        return carry

    lax.fori_loop(0, DFT_K1, body, 0)


def hyena_filter_spectra(hraw, n_order, width, c_blk=256):
    length = hraw.shape[0]
    assert 2 * length == DFT_N1 * DFT_N2
    fk, _, f3, _ = _dft_tables()
    ncb = width // c_blk
    return pl.pallas_call(
        functools.partial(_filt_spec_kernel, length=length),
        grid=(n_order, ncb),
        in_specs=[pl.BlockSpec((length, c_blk), lambda o, j: (0, o * 2 * ncb + j)),
                  pl.BlockSpec((length, c_blk), lambda o, j: (0, o * 2 * ncb + ncb + j)),
                  pl.BlockSpec(fk.shape, lambda o, j: (0, 0)),
                  pl.BlockSpec(f3.shape, lambda o, j: (0, 0, 0))],
        out_specs=pl.BlockSpec((None, DFT_K1, 2, DFT_N2, c_blk), lambda o, j: (o, 0, 0, 0, j)),
        out_shape=jax.ShapeDtypeStruct((n_order, DFT_K1, 2, DFT_N2, width), BF16),
        scratch_shapes=[pltpu.VMEM((length, c_blk), F32),
                        pltpu.VMEM((DFT_ROWS * DFT_N2, c_blk), F32),
                        pltpu.VMEM((DFT_ROWS * DFT_N2, c_blk), F32)],
        compiler_params=_cparams(("arbitrary", "arbitrary"), VMEM_LIMIT_BYTES),
        name="hyena_filter_spectrum",
    )(hraw, hraw, fk, f3)


def _short_conv(z, w_ref, b_ref, rows):
    length = z.shape[0]
    pos = lax.broadcasted_iota(I32, (length, 1), 0) % (length // rows)
    zm = jnp.where(pos == 0, 0.0, pltpu.roll(z, 1, 0))
    zp = jnp.where(pos == (length // rows) - 1, 0.0, pltpu.roll(z, length - 1, 0))
    w = w_ref[...]
    return zm * w[0:1, :] + z * w[1:2, :] + zp * w[2:3, :] + b_ref[...]


def _hyena_conv_kernel(s_ref, m_ref, kf_ref, bias_ref, cws_ref, cbs_ref, cwm_ref, cbm_ref,
                       fk_ref, f3_ref, g3_ref, gk_ref, o_ref, src_ref, mul_ref, a_ref, *,
                       rows, conv_signal, group):
    half = DFT_N1 // 2
    blk_rows = 2 * DFT_N2
    sig = s_ref[...]
    if conv_signal:
        sig = _short_conv(sig, cws_ref, cbs_ref, rows)
    src_ref[...] = sig
    mul_ref[...] = _short_conv(m_ref[...], cwm_ref, cbm_ref, rows)
    _dft_stage1(src_ref, a_ref, fk_ref)

    def body3(i, carry):
        k1s = [i * group + q for q in range(group)]
        r0s = [pl.multiple_of(k1 * blk_rows, blk_rows) for k1 in k1s]
        blocks = [a_ref[pl.ds(r0, blk_rows), :].astype(BF16) for r0 in r0s]
        outs = []
        for k1, a in zip(k1s, blocks):
            x = jnp.dot(f3_ref[k1], a, preferred_element_type=F32)
            xr, xi = x[:DFT_N2], x[DFT_N2:]
            kr = kf_ref[k1, 0].astype(F32)
            ki = kf_ref[k1, 1].astype(F32)
            y = jnp.concatenate([xr * kr - xi * ki, xr * ki + xi * kr], axis=0).astype(BF16)
            outs.append(jnp.dot(g3_ref[k1], y, preferred_element_type=F32))
        for r0, o in zip(r0s, outs):
            a_ref[pl.ds(r0, blk_rows), :] = o
        return carry

    lax.fori_loop(0, DFT_K1 // group, body3, 0)
    bias = bias_ref[...]

    def body1(m, carry):
        sub = pl.ds(pl.multiple_of(m * SUBLANES, SUBLANES), SUBLANES)
        blk = [a_ref.at[pl.ds(DFT_N2 * j, DFT_N2)][sub, :] for j in range(DFT_ROWS)]
        rhs = jnp.concatenate(blk, axis=0).astype(BF16)
        out = jnp.dot(gk_ref[...], rhs, preferred_element_type=F32)
        for n1 in range(half):
            blk_n1 = pl.ds(DFT_N2 * n1, DFT_N2)
            conv = out[SUBLANES * n1:SUBLANES * (n1 + 1)]
            o_ref.at[blk_n1][sub, :] = mul_ref.at[blk_n1][sub, :] * (
                conv + src_ref.at[blk_n1][sub, :] * bias)
        return carry

    lax.fori_loop(0, DFT_N2 // SUBLANES, body1, 0, unroll=2)


def hyena_conv(sig, sig_col, mul, mul_col, kf, bias, cw_s, cb_s, cw_m, cb_m, rows, conv_signal,
               c_blk=256, group=11):
    b, length, _ = sig.shape
    width = kf.shape[-1]
    ncb = width // c_blk
    assert DFT_K1 % group == 0
    fk, gk, f3, g3 = _dft_tables()
    once = pl.Buffered(1)
    const2 = lambda a: pl.BlockSpec(a.shape, lambda j, i: (0, 0), pipeline_mode=once)
    const3 = lambda a: pl.BlockSpec(a.shape, lambda j, i: (0, 0, 0), pipeline_mode=once)
    chan = lambda r: pl.BlockSpec((r, c_blk), lambda j, i: (0, j), pipeline_mode=once)
    return pl.pallas_call(
        functools.partial(_hyena_conv_kernel, rows=rows, conv_signal=conv_signal, group=group),
        grid=(ncb, b),
        in_specs=[pl.BlockSpec((None, length, c_blk), lambda j, i: (i, 0, sig_col * ncb + j)),
                  pl.BlockSpec((None, length, c_blk), lambda j, i: (i, 0, mul_col * ncb + j)),
                  pl.BlockSpec((DFT_K1, 2, DFT_N2, c_blk), lambda j, i: (0, 0, 0, j),
                               pipeline_mode=once),
                  chan(1), chan(3), chan(1), chan(3), chan(1),
                  const2(fk), const3(f3), const3(g3), const2(gk)],
        out_specs=pl.BlockSpec((None, length, c_blk), lambda j, i: (i, 0, j)),
        out_shape=jax.ShapeDtypeStruct((b, length, width), F32),
        scratch_shapes=[pltpu.VMEM((length, c_blk), F32),
                        pltpu.VMEM((length, c_blk), F32),
                        pltpu.VMEM((DFT_ROWS * DFT_N2, c_blk), F32)],
        compiler_params=_cparams(("arbitrary", "arbitrary"), VMEM_LIMIT_BYTES),
        name="hyena_conv",
    )(sig, mul, kf, bias.reshape(1, width), cw_s, cb_s.reshape(1, width), cw_m,
      cb_m.reshape(1, width), fk, f3, g3, gk)


def _s5_disc_kernel(lr_ref, li_ref, dt_ref, br_ref, bi_ref, ar_ref, ai_ref, bbr_ref, bbi_ref):
    lr, li, dt = lr_ref[...], li_ref[...], jnp.exp(dt_ref[...])
    mag = jnp.exp(lr * dt)
    ar = mag * jnp.cos(li * dt)
    ai = mag * jnp.sin(li * dt)
    den = 1.0 / (lr * lr + li * li)
    qr = ((ar - 1.0) * lr + ai * li) * den
    qi = (ai * lr - (ar - 1.0) * li) * den
    br, bi = br_ref[...], bi_ref[...]
    ar_ref[...] = ar
    ai_ref[...] = ai
    bbr_ref[...] = qr * br - qi * bi
    bbi_ref[...] = qr * bi + qi * br


def s5_discretize(lam_re, lam_im, log_step, b_re, b_im):
    nd, g, p, h = b_re.shape
    rep = lambda a: jnp.broadcast_to(a[..., None], (nd, g, p, h)).reshape(nd * g, p * h)
    dt = jnp.broadcast_to(log_step[:, :, None, None], (nd, g, p, h)).reshape(nd * g, p * h)
    flat = lambda a: a.reshape(nd * g, p * h)
    shp = jax.ShapeDtypeStruct((nd * g, p * h), F32)
    ar, ai, bbr, bbi = pl.pallas_call(
        _s5_disc_kernel, out_shape=[shp] * 4, name="s5_discretize",
    )(rep(lam_re), rep(lam_im), dt, flat(b_re), flat(b_im))
    un = lambda a: a.reshape(nd, g, p, h)
    return un(ar)[..., 0], un(ai)[..., 0], un(bbr), un(bbi)


def _s5_scan_kernel(ucf_ref, uf_ref, ucb_ref, ub_ref, wf_ref, wb_ref, cf_ref, cb_ref, lam_ref,
                    yf_ref, yb_ref, xf_ref, xb_ref, st_ref, *, steps, nc):
    nk = xf_ref.shape[1]
    ck = uf_ref.shape[1] // nk
    i = pl.program_id(0)
    is_ctx = i < nc

    @pl.when(i == 0)
    def _():
        st_ref[...] = jnp.zeros_like(st_ref)

    uf = jnp.where(is_ctx, ucf_ref[...], uf_ref[...])
    ub = jnp.where(is_ctx, ucb_ref[...], ub_ref[...])

    def scan(x_ref, d, reverse):
        lr, li = lam_ref[d, 0], lam_ref[d, 1]

        def body(t, carry):
            sr, si = carry
            tt = (steps - 1 - t) if reverse else t
            r = pl.ds(pl.multiple_of(tt * SUBLANES, SUBLANES), SUBLANES)
            nr = lr * sr - li * si + x_ref[0, :, r, :]
            ni = lr * si + li * sr + x_ref[1, :, r, :]
            x_ref[0, :, r, :] = nr
            x_ref[1, :, r, :] = ni
            return nr, ni

        st_ref[d, 0], st_ref[d, 1] = lax.fori_loop(0, steps, body, (st_ref[d, 0], st_ref[d, 1]))

    for u, w_ref, c_ref, x_ref, y_ref, d in ((uf, wf_ref, cf_ref, xf_ref, yf_ref, 0),
                                             (ub, wb_ref, cb_ref, xb_ref, yb_ref, 1)):
        for k in range(nk):
            uk = u[:, k * ck:(k + 1) * ck].astype(BF16)
            for ri in range(2):
                x_ref[ri, k] = jnp.dot(uk, w_ref[ri, k], preferred_element_type=F32)
        scan(x_ref, d, d == 1)
        for k in range(nk):
            y_ref[:, k * ck:(k + 1) * ck] = sum(
                jnp.dot(x_ref[ri, k].astype(BF16), c_ref[ri, k], preferred_element_type=F32)
                for ri in range(2))


def _block_diag_in(bb, nk):
    g, p, h = bb.shape
    gpk = g // nk
    eye = jnp.eye(gpk, dtype=bb.dtype)
    w = jnp.einsum('kaph,ab->kahbp', bb.reshape(nk, gpk, p, h), eye)
    return w.reshape(nk, gpk * h, gpk * p)


def _block_diag_out(cc, nk):
    g, h, p = cc.shape
    gpk = g // nk
    eye = jnp.eye(gpk, dtype=cc.dtype)
    w = jnp.einsum('kahp,ab->kapbh', cc.reshape(nk, gpk, h, p), eye)
    return w.reshape(nk, gpk * p, gpk * h)


def s5_scan(uc, u, lam_r, lam_i, bb_r, bb_i, c_re, c_im, batch, steps=64):
    width = u.shape[1]
    nk = width // LANES
    _, g, p = lam_r.shape
    sk = g * p // nk
    rpc = steps * batch
    nc, nl = uc.shape[0] // rpc, u.shape[0] // rpc
    assert steps % (2 * nk) == 0 and uc.shape[0] % rpc == 0 and u.shape[0] % rpc == 0
    w_in = jnp.stack([jnp.stack([_block_diag_in(bb_r[d], nk), _block_diag_in(bb_i[d], nk)])
                      for d in range(2)]).astype(BF16)
    w_out = jnp.stack([jnp.stack([_block_diag_out(c_re[d], nk), -_block_diag_out(c_im[d], nk)])
                       for d in range(2)]).astype(BF16)
    lam = jnp.stack([lam_r.reshape(2, nk, 1, sk), lam_i.reshape(2, nk, 1, sk)], axis=1)
    lam = jnp.broadcast_to(lam, (2, 2, nk, batch, sk))
    blk = lambda f: pl.BlockSpec((rpc, width), lambda i: (f(i), 0))
    lat_f = lambda i: jnp.maximum(i - nc, 0)
    lat_b = lambda i: jnp.clip(nl - 1 - i + nc, 0, nl - 1)
    par = lambda a, d: pl.BlockSpec((None,) + a.shape[1:], lambda i: (d, 0, 0, 0, 0))
    shp = jax.ShapeDtypeStruct(u.shape, F32)
    return pl.pallas_call(
        functools.partial(_s5_scan_kernel, steps=steps, nc=nc),
        grid=(nc + nl,),
        in_specs=[blk(lambda i: jnp.minimum(i, nc - 1)), blk(lat_f),
                  blk(lambda i: jnp.maximum(nc - 1 - i, 0)), blk(lat_b),
                  par(w_in, 0), par(w_in, 1), par(w_out, 0), par(w_out, 1),
                  pl.BlockSpec(lam.shape, lambda i: (0, 0, 0, 0, 0))],
        out_specs=[blk(lat_f), blk(lat_b)],
        out_shape=[shp, shp],
        scratch_shapes=[pltpu.VMEM((2, nk, rpc, sk), F32), pltpu.VMEM((2, nk, rpc, sk), F32),
                        pltpu.VMEM((2, 2, nk, batch, sk), F32)],
        compiler_params=_cparams(("arbitrary",), VMEM_LIMIT_BYTES),
        name="s5_scan",
    )(uc, u, uc, u, w_in, w_in, w_out, w_out, lam)


def _mixer_tail_kernel(x_ref, hy_ref, ys_ref, u_ref, d_ref, gw_ref, gb_ref, woh_ref, wos_ref,
                       g1_ref, n2_ref, sh2_ref, sc2_ref, rwt_ref, x1_ref, h2_ref, lg_ref):
    y = ys_ref[...] + d_ref[...] * u_ref[...]
    y = 0.5 * y * (1.0 + jnp.tanh(math.sqrt(2.0 / math.pi) * (y + 0.044715 * (y * y * y))))
    gate = jnp.dot(y.astype(BF16), gw_ref[...], preferred_element_type=F32) + gb_ref[...]
    s5 = y * (1.0 / (1.0 + jnp.exp(-gate)))
    mix = (jnp.dot(hy_ref[...].astype(BF16), woh_ref[...], preferred_element_type=F32)
           + jnp.dot(s5.astype(BF16), wos_ref[...], preferred_element_type=F32))
    x1 = x_ref[...] + g1_ref[...] * mix
    x1_ref[...] = x1
    h2 = _rmsnorm(x1, n2_ref[...]) * (1.0 + sc2_ref[...]) + sh2_ref[...]
    h2_ref[...] = h2.astype(BF16)
    lg_ref[...] = lax.dot_general(rwt_ref[...], h2, (((1,), (1,)), ((), ())), precision=HIGHEST,
                                  preferred_element_type=F32)


def mixer_tail(x, hy, ys, u, s5_d, glu_w, glu_b, w_out, g1, norm2_g, sh2, sc2, router_w, ts=512):
    b, s, d = x.shape
    hw = hy.shape[2]
    sw = u.shape[2]
    ne = router_w.shape[1]
    tok = lambda n: pl.BlockSpec((None, ts, n), lambda i, j: (i, j, 0))
    vec = lambda n: pl.BlockSpec((1, n), lambda i, j: (0, 0))
    per_b = pl.BlockSpec((None, 1, d), lambda i, j: (i, 0, 0))
    mat = lambda r, c: pl.BlockSpec((r, c), lambda i, j: (0, 0))
    return pl.pallas_call(
        _mixer_tail_kernel,
        grid=(b, s // ts),
        in_specs=[tok(d), tok(hw), tok(sw), tok(sw), vec(sw), mat(sw, sw), vec(sw),
                  mat(hw, d), mat(sw, d), per_b, vec(d), per_b, per_b, mat(ne, d)],
        out_specs=[tok(d), tok(d), pl.BlockSpec((None, ne, ts), lambda i, j: (i, 0, j))],
        out_shape=[jax.ShapeDtypeStruct((b, s, d), F32), jax.ShapeDtypeStruct((b, s, d), BF16),
                   jax.ShapeDtypeStruct((b, ne, s), F32)],
        compiler_params=_cparams(("arbitrary", "arbitrary")),
        name="mixer_tail",
    )(x, hy, ys, u, s5_d.reshape(1, sw), glu_w.astype(BF16), glu_b.reshape(1, sw),
      w_out[:hw].astype(BF16), w_out[hw:].astype(BF16), g1, norm2_g.reshape(1, d), sh2, sc2,
      router_w.T)


def _lane_cumsum_exclusive(x):
    rows, s = x.shape
    ii = lax.broadcasted_iota(I32, (LANES, LANES), 0)
    jj = lax.broadcasted_iota(I32, (LANES, LANES), 1)
    tri = jnp.where(ii < jj, 1.0, 0.0).astype(BF16)
    carry = jnp.zeros((rows, 1), F32)
    out, base = [], []
    for blk in range(s // LANES):
        xb = x[:, blk * LANES:(blk + 1) * LANES]
        out.append(jnp.dot(xb.astype(BF16), tri, preferred_element_type=F32) + carry)
        base.append(carry)
        carry = carry + jnp.sum(xb, axis=1, keepdims=True)
    return jnp.concatenate(out, axis=1), jnp.concatenate(base, axis=1)


def _route_kernel(lg_ref, pos_em_ref, pos_tm_ref, gate_tm_ref, base_ref, aff_ref, *, cap):
    lg = lg_ref[...]
    ne, s = lg.shape
    ex = jnp.exp(lg - jnp.max(lg, axis=0, keepdims=True))
    aff_ref[...] = ex / jnp.sum(ex, axis=0, keepdims=True)
    aff = aff_ref[...]
    count_ge = lambda v, t: jnp.sum(jnp.where(v >= t, 1.0, 0.0), axis=1, keepdims=True)

    def coarse(i, tb):
        cand = tb | jnp.left_shift(jnp.int32(1), 30 - i)
        return jnp.where(count_ge(aff, pltpu.bitcast(cand, F32)) >= cap, cand, tb)

    tb = lax.fori_loop(0, 31, coarse, jnp.zeros((ne, 1), I32))
    t_hi = pltpu.bitcast(tb, F32)
    ulp = pltpu.bitcast(tb + 1, F32) - t_hi
    resid = aff - t_hi

    def fine(j, carry):
        c, step = carry
        cand = c + step
        return jnp.where(count_ge(resid, cand) >= cap, cand, c), step * 0.5

    t_lo, _ = lax.fori_loop(0, 24, fine, (jnp.zeros((ne, 1), F32), ulp * 0.5))
    gt = resid > t_lo
    eq = resid == t_lo
    need = cap - jnp.sum(jnp.where(gt, 1.0, 0.0), axis=1, keepdims=True)
    eq_rank, _ = _lane_cumsum_exclusive(jnp.where(eq, 1.0, 0.0))
    sel = gt | (eq & (eq_rank < need))
    pos, base = _lane_cumsum_exclusive(jnp.where(sel, 1.0, 0.0))
    posf = jnp.where(sel, pos + 1.0, 0.0)
    gate = jnp.where(sel, aff, 0.0)
    pos_em_ref[...] = posf.astype(I32) - 1
    base_ref[...] = base.astype(I32)
    hi = jnp.floor(posf * (1.0 / 16.0))
    lo = posf - 16.0 * hi
    ii = lax.broadcasted_iota(I32, (LANES, LANES), 0)
    jj = lax.broadcasted_iota(I32, (LANES, LANES), 1)
    eye = jnp.where(ii == jj, 1.0, 0.0)
    nt = (((1,), (1,)), ((), ()))
    for blk in range(s // LANES):
        sl = slice(blk * LANES, (blk + 1) * LANES)
        t_hi = lax.dot_general(eye.astype(BF16), hi[:, sl].astype(BF16), nt, preferred_element_type=F32)
        t_lo = lax.dot_general(eye.astype(BF16), lo[:, sl].astype(BF16), nt, preferred_element_type=F32)
        pos_tm_ref[sl, :] = (16.0 * t_hi + t_lo).astype(I32) - 1
        gate_tm_ref[sl, :] = lax.dot_general(eye, gate[:, sl], nt, precision=HIGHEST,
                                             preferred_element_type=F32)


def route(logits, cap):
    b, ne, s = logits.shape
    nb = s // LANES
    return pl.pallas_call(
        functools.partial(_route_kernel, cap=cap),
        grid=(b,),
        in_specs=[pl.BlockSpec((None, ne, s), lambda i: (i, 0, 0))],
        out_specs=[pl.BlockSpec((None, ne, s), lambda i: (i, 0, 0)),
                   pl.BlockSpec((None, s, ne), lambda i: (i, 0, 0)),
                   pl.BlockSpec((None, s, ne), lambda i: (i, 0, 0)),
                   pl.BlockSpec((None, ne, nb), lambda i: (i, 0, 0))],
        out_shape=[jax.ShapeDtypeStruct((b, ne, s), I32), jax.ShapeDtypeStruct((b, s, ne), I32),
                   jax.ShapeDtypeStruct((b, s, ne), F32), jax.ShapeDtypeStruct((b, ne, nb), I32)],
        scratch_shapes=[pltpu.VMEM((ne, s), F32)],
        compiler_params=_cparams(("arbitrary",)),
        name="route",
    )(logits)


SLOT_ALIGN = 16


def _slot_windows(base, cap, chunk, win):
    lo = base[:, :, ::chunk // LANES]
    hi = jnp.concatenate([lo[:, :, 1:], jnp.full_like(lo[:, :, :1], cap)], axis=2)
    start = (lo // SLOT_ALIGN) * SLOT_ALIGN
    nwin = jnp.max((hi - start + win - 1) // win, axis=1)
    return jnp.transpose(start, (0, 2, 1)).reshape(-1), nwin.reshape(-1)


def _window(st_ref, idx, w, win, cap):
    first = st_ref[idx] + w * win
    return first, pl.multiple_of(jnp.minimum(first, cap - win), SLOT_ALIGN)


def _gather_kernel(st_ref, nw_ref, pos_ref, h_ref, o_ref, *, win):
    b, j, nch = pl.program_id(0), pl.program_id(1), pl.num_programs(1)
    ne, cap, _ = o_ref.shape
    tk = h_ref.shape[0]

    @pl.when(j == 0)
    def _():
        o_ref[...] = jnp.zeros_like(o_ref)

    pos = pos_ref[...]
    h = h_ref[...]
    row = lax.broadcasted_iota(I32, (win, tk), 0)

    def window(w, carry):
        starts, lhs = [], []
        for e in range(ne):
            first, start = _window(st_ref, (b * nch + j) * ne + e, w, win, cap)
            slot = row + start
            hit = (pos[e:e + 1, :] == slot) & (slot >= first)
            lhs.append(jnp.where(hit, 1.0, 0.0).astype(BF16))
            starts.append(start)
        res = jnp.dot(jnp.concatenate(lhs, axis=0), h, preferred_element_type=F32)
        for e, start in enumerate(starts):
            o_ref[e, pl.ds(start, win), :] += res[e * win:(e + 1) * win].astype(o_ref.dtype)
        return carry

    lax.fori_loop(0, nw_ref[b * nch + j], window, 0)


def moe_gather(pos_em, h2, base, cap, tk=256, win=64):
    b, ne, s = pos_em.shape
    d = h2.shape[2]
    starts, nwin = _slot_windows(base, cap, tk, win)
    return pl.pallas_call(
        functools.partial(_gather_kernel, win=win),
        grid_spec=pltpu.PrefetchScalarGridSpec(
            num_scalar_prefetch=2,
            grid=(b, s // tk),
            in_specs=[pl.BlockSpec((None, ne, tk), lambda i, j, st, nw: (i, 0, j)),
                      pl.BlockSpec((None, tk, d), lambda i, j, st, nw: (i, j, 0))],
            out_specs=pl.BlockSpec((ne, cap, d), lambda i, j, st, nw: (0, i, 0))),
        out_shape=jax.ShapeDtypeStruct((ne, b * cap, d), BF16),
        compiler_params=_cparams(("arbitrary", "arbitrary"), VMEM_LIMIT_BYTES),
        name="moe_gather",
    )(starts, nwin, pos_em, h2)


def _ffn_kernel(x_ref, wg_ref, wu_ref, wd_ref, o_ref, acc_ref):
    f = pl.program_id(2)

    @pl.when(f == 0)
    def _():
        acc_ref[...] = jnp.zeros_like(acc_ref)

    x = x_ref[...]
    g = jnp.dot(x, wg_ref[...].astype(BF16), preferred_element_type=F32)
    u = jnp.dot(x, wu_ref[...].astype(BF16), preferred_element_type=F32)
    h = (_silu(g) * u).astype(BF16)
    acc_ref[...] += jnp.dot(h, wd_ref[...].astype(BF16), preferred_element_type=F32)

    @pl.when(f == pl.num_programs(2) - 1)
    def _():
        o_ref[...] = acc_ref[...].astype(o_ref.dtype)


def moe_ffn(xe, w_gate, w_up, w_down, tm=2048, tf=256):
    ne, m, d = xe.shape
    ff = w_gate.shape[2]
    tm = min(tm, m)
    return pl.pallas_call(
        _ffn_kernel,
        grid=(ne, m // tm, ff // tf),
        in_specs=[pl.BlockSpec((None, tm, d), lambda e, i, f: (e, i, 0)),
                  pl.BlockSpec((None, d, tf), lambda e, i, f: (e, 0, f)),
                  pl.BlockSpec((None, d, tf), lambda e, i, f: (e, 0, f)),
                  pl.BlockSpec((None, tf, d), lambda e, i, f: (e, f, 0))],
        out_specs=pl.BlockSpec((None, tm, d), lambda e, i, f: (e, i, 0)),
        out_shape=jax.ShapeDtypeStruct((ne, m, d), BF16),
        scratch_shapes=[pltpu.VMEM((tm, d), F32)],
        compiler_params=_cparams(("arbitrary", "arbitrary", "arbitrary"), VMEM_LIMIT_BYTES),
        name="moe_ffn",
    )(xe, w_gate, w_up, w_down)


def _combine_kernel(st_ref, nw_ref, pos_ref, gate_ref, ye_ref, x1_ref, g2_ref, fg_ref, o_ref, acc_ref,
                    *, win):
    b, j, nt = pl.program_id(0), pl.program_id(1), pl.num_programs(1)
    tt, ne = pos_ref.shape
    cap = ye_ref.shape[1]
    acc_ref[...] = jnp.zeros_like(acc_ref)
    pos = pos_ref[...]
    gate = gate_ref[...]
    col = lax.broadcasted_iota(I32, (tt, win), 1)

    def window(w, carry):
        acc = None
        for p in range(ne // 2):
            lhs, rhs = [], []
            for e in (2 * p, 2 * p + 1):
                first, start = _window(st_ref, (b * nt + j) * ne + e, w, win, cap)
                slot = col + start
                hit = (pos[:, e:e + 1] == slot) & (slot >= first)
                lhs.append(jnp.where(hit, gate[:, e:e + 1], 0.0).astype(BF16))
                rhs.append(ye_ref[e, pl.ds(start, win), :])
            part = jnp.dot(jnp.concatenate(lhs, axis=1), jnp.concatenate(rhs, axis=0),
                           preferred_element_type=F32)
            acc = part if acc is None else acc + part
        acc_ref[...] += acc
        return carry

    lax.fori_loop(0, nw_ref[b * nt + j], window, 0)
    xo = x1_ref[...] + g2_ref[...] * acc_ref[...]
    o_ref[...] = _rmsnorm(xo, fg_ref[...])


def moe_combine(pos_tm, gate_tm, base, ye, x1, g2, final_g, cap, tt=512, win=128):
    b, s, ne = pos_tm.shape
    d = x1.shape[2]
    starts, nwin = _slot_windows(base, cap, tt, win)
    tok = lambda n: pl.BlockSpec((None, tt, n), lambda i, j, st, nw: (i, j, 0))
    return pl.pallas_call(
        functools.partial(_combine_kernel, win=win),
        grid_spec=pltpu.PrefetchScalarGridSpec(
            num_scalar_prefetch=2,
            grid=(b, s // tt),
            in_specs=[tok(ne), tok(ne),
                      pl.BlockSpec((ne, cap, d), lambda i, j, st, nw: (0, i, 0)),
                      tok(d),
                      pl.BlockSpec((None, 1, d), lambda i, j, st, nw: (i, 0, 0)),
                      pl.BlockSpec((1, d), lambda i, j, st, nw: (0, 0))],
            out_specs=tok(d),
            scratch_shapes=[pltpu.VMEM((tt, d), F32)]),
        out_shape=jax.ShapeDtypeStruct((b, s, d), F32),
        compiler_params=_cparams(("arbitrary", "arbitrary"), VMEM_LIMIT_BYTES),
        name="moe_combine",
    )(starts, nwin, pos_tm, gate_tm, ye, x1, g2, final_g.reshape(1, d))


def _layer(x, ctx, mods, norm1_g, norm2_g, w_in, w_out, conv_w, conv_b, filt, hy_bias, s5p,
           s5_c_re, s5_c_im, s5_d, s5_glu_w, s5_glu_b, router_w, ex_w_gate, ex_w_up, ex_w_down,
           final_g):
    b, s, d = x.shape
    n_order, hw = hy_bias.shape
    hy_cols = (n_order + 1) * hw
    sw = w_in.shape[1] - hy_cols
    rows = s // GRID_W
    ne = router_w.shape[1]
    cap = CAPACITY_FACTOR * s // ne

    per_b = lambda k: mods[:b, k * d:(k + 1) * d].reshape(b, 1, d)
    ctx_v = lambda k: jnp.broadcast_to(mods[b:b + 1, k * d:(k + 1) * d].reshape(1, 1, d), (b, 1, d))
    sh1, sc1, g1, sh2, sc2, g2 = [per_b(k) for k in range(N_MOD)]

    w_in_bf = w_in.astype(BF16)
    (u_ctx,) = inproj(ctx, norm1_g, ctx_v(0), ctx_v(1), w_in_bf[:, hy_cols:], (sw,))
    z_hy, u = inproj(x, norm1_g, sh1, sc1, w_in_bf, (hy_cols, sw))

    hraw = hyena_filter_mlp(s, *filt)
    kf = hyena_filter_spectra(hraw, n_order, hw)
    cw = lambda k: conv_w[:, k * hw:(k + 1) * hw]
    cb = lambda k: conv_b[k * hw:(k + 1) * hw]
    y1 = hyena_conv(z_hy, 0, z_hy, 1, kf[0], hy_bias[0], cw(0), cb(0), cw(1), cb(1), rows, True)
    hy = hyena_conv(y1, 0, z_hy, 2, kf[1], hy_bias[1], cw(0), cb(0), cw(2), cb(2), rows, False)

    lam_r, lam_i, bb_r, bb_i = s5p
    tmaj = lambda a: jnp.transpose(a, (1, 0, 2)).reshape(-1, sw)
    yf, yb = s5_scan(tmaj(u_ctx), tmaj(u), lam_r, lam_i, bb_r, bb_i, s5_c_re, s5_c_im, b)
    ys = jnp.transpose((yf + yb).reshape(s, b, sw), (1, 0, 2))

    x1, h2, logits = mixer_tail(x, hy, ys, u, s5_d, s5_glu_w, s5_glu_b, w_out, g1, norm2_g,
                                sh2, sc2, router_w)
    pos_em, pos_tm, gate_tm, base = route(logits, cap)
    xe = moe_gather(pos_em, h2, base, cap)
    ye = moe_ffn(xe, ex_w_gate, ex_w_up, ex_w_down)
    return moe_combine(pos_tm, gate_tm, base, ye, x1, g2, final_g, cap)


def kernel(x, c, ctx, c_ctx, mod_w, mod_b, norm1_g, norm2_g, w_in, w_out, conv_w, conv_b, hy_w1, hy_b1, hy_freq, hy_w2, hy_b2, hy_w3, hy_b3, hy_bias, s5_lam_re, s5_lam_im, s5_log_step, s5_b_re, s5_b_im, s5_c_re, s5_c_im, s5_d, s5_glu_w, s5_glu_b, router_w, ex_w_gate, ex_w_up, ex_w_down, final_g):
    depth = mod_w.shape[0]
    assert depth == 1, "context-token updates of non-final layers are not implemented"
    b, _, d = x.shape
    l = 0
    pad = (-(b + 1)) % SUBLANES
    cond = jnp.concatenate([c, c_ctx[None], jnp.zeros((pad, d), F32)], axis=0)
    mods = adaln_mods(cond, mod_w[l], mod_b[l])
    filt = (hy_w1[l], hy_b1[l], hy_freq[l], hy_w2[l], hy_b2[l], hy_w3[l], hy_b3[l])
    s5p = s5_discretize(s5_lam_re[l], s5_lam_im[l], s5_log_step[l], s5_b_re[l], s5_b_im[l])
    return _layer(x, ctx, mods, norm1_g[l], norm2_g[l], w_in[l], w_out[l], conv_w[l], conv_b[l],
                  filt, hy_bias[l], s5p, s5_c_re[l], s5_c_im[l], s5_d[l], s5_glu_w[l], s5_glu_b[l],
                  router_w[l], ex_w_gate[l], ex_w_up[l], ex_w_down[l], final_g)
```

```python
import functools
import math

import numpy as np
import jax
import jax.numpy as jnp
from jax import lax
from jax.experimental import pallas as pl
from jax.experimental.pallas import tpu as pltpu

F32 = jnp.float32
BF16 = jnp.bfloat16
I32 = jnp.int32
HIGHEST = lax.Precision.HIGHEST

SUBLANES = 8
LANES = 128
VMEM_LIMIT_BYTES = 58 * 1024 * 1024

GRID_W = 64
N_MOD = 6
NORM_EPS = 1e-6
POS_BANDS = 16
DECAY_FAST = 0.3
DECAY_SLOW = 1.5
DECAY_TARGET = 1e-2
CAPACITY_FACTOR = 2

DFT_N1 = 64
DFT_N2 = 128
DFT_K1 = DFT_N1 // 2 + 1
DFT_ROWS = 2 * DFT_K1


def _cparams(sem, vmem=None):
    return pltpu.CompilerParams(dimension_semantics=sem, vmem_limit_bytes=vmem)


def _silu(x):
    return x * (1.0 / (1.0 + jnp.exp(-x)))


def _rmsnorm(x, g):
    ms = jnp.mean(x * x, axis=-1, keepdims=True)
    return x * lax.rsqrt(ms + NORM_EPS) * g


def _adaln_kernel(c_ref, w_ref, b_ref, o_ref):
    s = _silu(c_ref[...])
    o_ref[...] = jnp.dot(s, w_ref[...], precision=HIGHEST, preferred_element_type=F32) + b_ref[...]


def adaln_mods(cond, mod_w, mod_b, tn=1536):
    rows, d = cond.shape
    n = mod_w.shape[1]
    return pl.pallas_call(
        _adaln_kernel,
        grid=(n // tn,),
        in_specs=[pl.BlockSpec((rows, d), lambda j: (0, 0)),
                  pl.BlockSpec((d, tn), lambda j: (0, j)),
                  pl.BlockSpec((1, tn), lambda j: (0, j))],
        out_specs=pl.BlockSpec((rows, tn), lambda j: (0, j)),
        out_shape=jax.ShapeDtypeStruct((rows, n), F32),
        compiler_params=_cparams(("arbitrary",)),
        name="adaln",
    )(cond, mod_w, mod_b.reshape(1, n))


def _short_conv(z, w_ref, b_ref, row_len):
    length = z.shape[0]
    pos = lax.broadcasted_iota(I32, (length, 1), 0) % row_len
    zm = jnp.where(pos == 0, 0.0, pltpu.roll(z, 1, 0))
    zp = jnp.where(pos == row_len - 1, 0.0, pltpu.roll(z, length - 1, 0))
    w = w_ref[...]
    return zm * w[0:1, :] + z * w[1:2, :] + zp * w[2:3, :] + b_ref[...]


def _inproj_kernel(x_ref, g_ref, sh_ref, sc_ref, w_ref, *refs, splits, conv_split, row_len):
    conv_refs, o_refs = refs[:len(refs) - len(splits)], refs[len(refs) - len(splits):]
    h = _rmsnorm(x_ref[...], g_ref[...])
    h = h * (1.0 + sc_ref[...]) + sh_ref[...]
    z = jnp.dot(h.astype(BF16), w_ref[...], preferred_element_type=F32)
    off = 0
    for idx, (o_ref, n) in enumerate(zip(o_refs, splits)):
        part = z[:, off:off + n]
        if idx == conv_split:
            part = _short_conv(part, *conv_refs, row_len)
        o_ref[...] = part
        off += n


def inproj(x, g, shift, scale, w_bf16, splits, conv=None, ts=512):
    b, s, d = x.shape
    n = w_bf16.shape[1]
    ts = min(ts, s)
    conv_split, conv_args, conv_specs, row_len = None, [], [], 1
    if conv is not None:
        conv_split, cw, cb, row_len = conv
        assert ts % row_len == 0
        conv_args = [cw, cb.reshape(1, -1)]
        conv_specs = [pl.BlockSpec(a.shape, lambda i, j: (0, 0)) for a in conv_args]
    out_specs = [pl.BlockSpec((None, ts, m), lambda i, j: (i, j, 0)) for m in splits]
    out_shape = [jax.ShapeDtypeStruct((b, s, m), F32) for m in splits]
    return pl.pallas_call(
        functools.partial(_inproj_kernel, splits=splits, conv_split=conv_split, row_len=row_len),
        grid=(b, s // ts),
        in_specs=[pl.BlockSpec((None, ts, d), lambda i, j: (i, j, 0)),
                  pl.BlockSpec((1, d), lambda i, j: (0, 0)),
                  pl.BlockSpec((None, 1, d), lambda i, j: (i, 0, 0)),
                  pl.BlockSpec((None, 1, d), lambda i, j: (i, 0, 0)),
                  pl.BlockSpec((d, n), lambda i, j: (0, 0))] + conv_specs,
        out_specs=out_specs,
        out_shape=out_shape,
        compiler_params=_cparams(("arbitrary", "arbitrary")),
        name="inproj",
    )(x, g.reshape(1, d), shift, scale, w_bf16, *conv_args)


def _filt_mlp_kernel(w1t_ref, w1c_ref, w1s_ref, b1_ref, fr_ref, w2_ref, b2_ref, w3_ref, b3_ref,
                     o_ref, *, length, tl):
    i0 = pl.program_id(0) * tl
    idx = (lax.broadcasted_iota(I32, (tl, 1), 0) + i0).astype(F32)
    t = idx / float(length - 1)
    omega = (2.0 * math.pi) * idx / float(length)
    fstep = ((POS_BANDS - 1) - 1e-4) / (POS_BANDS - 1)
    f = 1e-4 + lax.broadcasted_iota(I32, (1, POS_BANDS), 1).astype(F32) * fstep
    arg = omega * f
    pre = (t * w1t_ref[...]
           + jnp.dot(jnp.cos(arg), w1c_ref[...], precision=HIGHEST, preferred_element_type=F32)
           - jnp.dot(jnp.sin(arg), w1s_ref[...], precision=HIGHEST, preferred_element_type=F32)
           + b1_ref[...])
    fr = fr_ref[...]
    h = jnp.sin(fr[0:1, :] * pre)
    h = jnp.sin(fr[1:2, :] * (jnp.dot(h, w2_ref[...], precision=HIGHEST,
                                      preferred_element_type=F32) + b2_ref[...]))
    o_ref[...] = jnp.dot(h, w3_ref[...], precision=HIGHEST, preferred_element_type=F32) + b3_ref[...]


def hyena_filter_mlp(length, w1, b1, freq, w2, b2, w3, b3, tl=512):
    fw = w1.shape[1]
    n = w3.shape[1]
    full = lambda shape: pl.BlockSpec(shape, lambda i: (0, 0))
    return pl.pallas_call(
        functools.partial(_filt_mlp_kernel, length=length, tl=tl),
        grid=(length // tl,),
        in_specs=[full((1, fw)), full((POS_BANDS, fw)), full((POS_BANDS, fw)), full((1, fw)),
                  full((2, fw)), full((fw, fw)), full((1, fw)), full((fw, n)), full((1, n))],
        out_specs=pl.BlockSpec((tl, n), lambda i: (i, 0)),
        out_shape=jax.ShapeDtypeStruct((length, n), F32),
        compiler_params=_cparams(("arbitrary",)),
        name="hyena_filter_mlp",
    )(w1[0:1], w1[1:1 + POS_BANDS], w1[1 + POS_BANDS:], b1.reshape(1, fw), freq, w2,
      b2.reshape(1, fw), w3, b3.reshape(1, n))


@functools.lru_cache(maxsize=None)
def _dft_tables():
    n1n, n2n, k1n = DFT_N1, DFT_N2, DFT_K1
    n = n1n * n2n
    half = n1n // 2
    k1 = np.arange(k1n)[:, None]
    n1 = np.arange(half)[None, :]
    th = 2.0 * np.pi * k1 * n1 / n1n
    f1 = np.zeros((DFT_ROWS, half))
    f1[0::2] = np.cos(th)
    f1[1::2] = -np.sin(th)
    wgt = np.where((k1 == 0) | (k1 == half), 1.0, 2.0)
    g1 = np.zeros((half, DFT_ROWS))
    g1[:, 0::2] = (wgt * np.cos(th)).T / n
    g1[:, 1::2] = (-wgt * np.sin(th)).T / n
    eye = np.eye(SUBLANES)
    fk = np.kron(f1, eye)
    gk = np.kron(g1, eye)
    k2 = np.arange(n2n)[:, None]
    n2 = np.arange(n2n)[None, :]
    f3 = np.zeros((k1n, 2 * n2n, 2 * n2n))
    for kk in range(k1n):
        ph = 2.0 * np.pi * n2 * (n1n * k2 + kk) / n
        tr, ti = np.cos(ph), -np.sin(ph)
        f3[kk, :n2n, :n2n] = tr
        f3[kk, :n2n, n2n:] = -ti
        f3[kk, n2n:, :n2n] = ti
        f3[kk, n2n:, n2n:] = tr
    g3 = np.transpose(f3, (0, 2, 1))
    to = lambda a: jnp.asarray(a, dtype=F32).astype(BF16)
    return to(fk), to(gk), to(f3), to(g3)


def _dft_stage1(src_ref, a_ref, fk_ref):
    half = DFT_N1 // 2

    def body(m, carry):
        sub = pl.ds(pl.multiple_of(m * SUBLANES, SUBLANES), SUBLANES)
        rows = [src_ref.at[pl.ds(DFT_N2 * n1, DFT_N2)][sub, :] for n1 in range(half)]
        rhs = jnp.concatenate(rows, axis=0).astype(BF16)
        out = jnp.dot(fk_ref[...], rhs, preferred_element_type=F32)
        for j in range(DFT_ROWS):
            a_ref.at[pl.ds(DFT_N2 * j, DFT_N2)][sub, :] = out[SUBLANES * j:SUBLANES * (j + 1)]
        return carry

    lax.fori_loop(0, DFT_N2 // SUBLANES, body, 0, unroll=2)


def _dft_stage3(a_ref, f3_ref, k1):
    r0 = pl.multiple_of(k1 * (2 * DFT_N2), 2 * DFT_N2)
    a = a_ref[pl.ds(r0, 2 * DFT_N2), :].astype(BF16)
    x = jnp.dot(f3_ref[k1], a, preferred_element_type=F32)
    return x[:DFT_N2], x[DFT_N2:]


def _filt_spec_kernel(hf_ref, hb_ref, fk_ref, f3_ref, o_ref, src_ref, af_ref, ab_ref, *, length):
    c = hf_ref.shape[1]
    cb = pl.program_id(1)
    hw = pl.num_programs(1) * c
    row = lax.broadcasted_iota(I32, (length, 1), 0)
    t = row.astype(F32) / float(length - 1)
    ch = (lax.broadcasted_iota(I32, (1, c), 1) + cb * c).astype(F32)
    d0 = math.log(DECAY_TARGET) / DECAY_FAST
    d1 = math.log(DECAY_TARGET) / DECAY_SLOW
    deltas = jnp.abs(d0 + ch * ((d1 - d0) / float(hw - 1)))
    decay = jnp.exp(-t * deltas)
    fwd = hf_ref[...] * decay
    bwd = jnp.where(row == 0, 0.0, hb_ref[...] * decay)
    inv = 1.0 / (jnp.sum(jnp.abs(fwd), axis=0, keepdims=True)
                 + jnp.sum(jnp.abs(bwd), axis=0, keepdims=True))
    src_ref[...] = fwd
    _dft_stage1(src_ref, af_ref, fk_ref)
    src_ref[...] = bwd
    _dft_stage1(src_ref, ab_ref, fk_ref)

    def body(k1, carry):
        fr, fi = _dft_stage3(af_ref, f3_ref, k1)
        br, bi = _dft_stage3(ab_ref, f3_ref, k1)
        o_ref[k1, 0] = ((fr + br) * inv).astype(o_ref.dtype)
        o_ref[k1, 1] = ((fi - bi) * inv).astype(o_ref.dtype)
        return carry

    lax.fori_loop(0, DFT_K1, body, 0)


def hyena_filter_spectra(hraw, n_order, width, c_blk=256):
    length = hraw.shape[0]
    assert 2 * length == DFT_N1 * DFT_N2
    fk, _, f3, _ = _dft_tables()
    ncb = width // c_blk
    return pl.pallas_call(
        functools.partial(_filt_spec_kernel, length=length),
        grid=(n_order, ncb),
        in_specs=[pl.BlockSpec((length, c_blk), lambda o, j: (0, o * 2 * ncb + j)),
                  pl.BlockSpec((length, c_blk), lambda o, j: (0, o * 2 * ncb + ncb + j)),
                  pl.BlockSpec(fk.shape, lambda o, j: (0, 0)),
                  pl.BlockSpec(f3.shape, lambda o, j: (0, 0, 0))],
        out_specs=pl.BlockSpec((None, DFT_K1, 2, DFT_N2, c_blk), lambda o, j: (o, 0, 0, 0, j)),
        out_shape=jax.ShapeDtypeStruct((n_order, DFT_K1, 2, DFT_N2, width), BF16),
        scratch_shapes=[pltpu.VMEM((length, c_blk), F32),
                        pltpu.VMEM((DFT_ROWS * DFT_N2, c_blk), F32),
                        pltpu.VMEM((DFT_ROWS * DFT_N2, c_blk), F32)],
        compiler_params=_cparams(("arbitrary", "arbitrary"), VMEM_LIMIT_BYTES),
        name="hyena_filter_spectrum",
    )(hraw, hraw, fk, f3)


def _hyena_conv_kernel(s_ref, m_ref, kf_ref, bias_ref, fk_ref, f3_ref, g3_ref, gk_ref, o_ref, a_ref,
                       *, group):
    half = DFT_N1 // 2
    blk_rows = 2 * DFT_N2
    _dft_stage1(s_ref, a_ref, fk_ref)

    def body3(i, carry):
        k1s = [i * group + q for q in range(group)]
        r0s = [pl.multiple_of(k1 * blk_rows, blk_rows) for k1 in k1s]
        blocks = [a_ref[pl.ds(r0, blk_rows), :].astype(BF16) for r0 in r0s]
        outs = []
        for k1, a in zip(k1s, blocks):
            x = jnp.dot(f3_ref[k1], a, preferred_element_type=F32)
            xr, xi = x[:DFT_N2], x[DFT_N2:]
            kr = kf_ref[k1, 0].astype(F32)
            ki = kf_ref[k1, 1].astype(F32)
            y = jnp.concatenate([xr * kr - xi * ki, xr * ki + xi * kr], axis=0).astype(BF16)
            outs.append(jnp.dot(g3_ref[k1], y, preferred_element_type=F32))
        for r0, o in zip(r0s, outs):
            a_ref[pl.ds(r0, blk_rows), :] = o
        return carry

    lax.fori_loop(0, DFT_K1 // group, body3, 0)
    bias = bias_ref[...]

    def body1(m, carry):
        sub = pl.ds(pl.multiple_of(m * SUBLANES, SUBLANES), SUBLANES)
        blk = [a_ref.at[pl.ds(DFT_N2 * j, DFT_N2)][sub, :] for j in range(DFT_ROWS)]
        rhs = jnp.concatenate(blk, axis=0).astype(BF16)
        out = jnp.dot(gk_ref[...], rhs, preferred_element_type=F32)
        for n1 in range(half):
            blk_n1 = pl.ds(DFT_N2 * n1, DFT_N2)
            conv = out[SUBLANES * n1:SUBLANES * (n1 + 1)]
            o_ref.at[blk_n1][sub, :] = m_ref.at[blk_n1][sub, :] * (
                conv + s_ref.at[blk_n1][sub, :] * bias)
        return carry

    lax.fori_loop(0, DFT_N2 // SUBLANES, body1, 0, unroll=2)


def hyena_conv(sig, sig_col, mul, mul_col, kf, bias, c_blk=256, group=11):
    b, length, _ = sig.shape
    width = kf.shape[-1]
    ncb = width // c_blk
    assert DFT_K1 % group == 0
    fk, gk, f3, g3 = _dft_tables()
    once = pl.Buffered(1)
    const2 = lambda a: pl.BlockSpec(a.shape, lambda j, i: (0, 0), pipeline_mode=once)
    const3 = lambda a: pl.BlockSpec(a.shape, lambda j, i: (0, 0, 0), pipeline_mode=once)
    return pl.pallas_call(
        functools.partial(_hyena_conv_kernel, group=group),
        grid=(ncb, b),
        in_specs=[pl.BlockSpec((None, length, c_blk), lambda j, i: (i, 0, sig_col * ncb + j)),
                  pl.BlockSpec((None, length, c_blk), lambda j, i: (i, 0, mul_col * ncb + j)),
                  pl.BlockSpec((DFT_K1, 2, DFT_N2, c_blk), lambda j, i: (0, 0, 0, j),
                               pipeline_mode=once),
                  pl.BlockSpec((1, c_blk), lambda j, i: (0, j), pipeline_mode=once),
                  const2(fk), const3(f3), const3(g3), const2(gk)],
        out_specs=pl.BlockSpec((None, length, c_blk), lambda j, i: (i, 0, j)),
        out_shape=jax.ShapeDtypeStruct((b, length, width), F32),
        scratch_shapes=[pltpu.VMEM((DFT_ROWS * DFT_N2, c_blk), F32)],
        compiler_params=_cparams(("arbitrary", "arbitrary"), VMEM_LIMIT_BYTES),
        name="hyena_conv",
    )(sig, mul, kf, bias.reshape(1, width), fk, f3, g3, gk)


def _s5_disc_kernel(lr_ref, li_ref, dt_ref, br_ref, bi_ref, ar_ref, ai_ref, bbr_ref, bbi_ref):
    lr, li, dt = lr_ref[...], li_ref[...], jnp.exp(dt_ref[...])
    mag = jnp.exp(lr * dt)
    ar = mag * jnp.cos(li * dt)
    ai = mag * jnp.sin(li * dt)
    den = 1.0 / (lr * lr + li * li)
    qr = ((ar - 1.0) * lr + ai * li) * den
    qi = (ai * lr - (ar - 1.0) * li) * den
    br, bi = br_ref[...], bi_ref[...]
    ar_ref[...] = ar
    ai_ref[...] = ai
    bbr_ref[...] = qr * br - qi * bi
    bbi_ref[...] = qr * bi + qi * br


def s5_discretize(lam_re, lam_im, log_step, b_re, b_im):
    nd, g, p, h = b_re.shape
    rep = lambda a: jnp.broadcast_to(a[..., None], (nd, g, p, h)).reshape(nd * g, p * h)
    dt = jnp.broadcast_to(log_step[:, :, None, None], (nd, g, p, h)).reshape(nd * g, p * h)
    flat = lambda a: a.reshape(nd * g, p * h)
    shp = jax.ShapeDtypeStruct((nd * g, p * h), F32)
    ar, ai, bbr, bbi = pl.pallas_call(
        _s5_disc_kernel, out_shape=[shp] * 4, name="s5_discretize",
    )(rep(lam_re), rep(lam_im), dt, flat(b_re), flat(b_im))
    un = lambda a: a.reshape(nd, g, p, h)
    return un(ar)[..., 0], un(ai)[..., 0], un(bbr), un(bbi)


def _s5_scan_kernel(ucf_ref, uf_ref, ucb_ref, ub_ref, wf_ref, wb_ref, cf_ref, cb_ref, lam_ref,
                    yf_ref, yb_ref, xf_ref, xb_ref, st_ref, *, steps, nc, kpp):
    nk = xf_ref.shape[1]
    ck = uf_ref.shape[1] // nk
    i = pl.program_id(0)
    is_ctx = i < nc

    @pl.when(i == 0)
    def _():
        st_ref[...] = jnp.zeros_like(st_ref)

    uf = jnp.where(is_ctx, ucf_ref[...], uf_ref[...])
    ub = jnp.where(is_ctx, ucb_ref[...], ub_ref[...])

    def scan(x_ref, d, reverse):
        for k0 in range(0, nk, kpp):
            ks = slice(k0, k0 + kpp)
            lr, li = lam_ref[d, 0, ks], lam_ref[d, 1, ks]

            def body(t, carry, ks=ks, lr=lr, li=li):
                sr, si = carry
                tt = (steps - 1 - t) if reverse else t
                r = pl.ds(pl.multiple_of(tt * SUBLANES, SUBLANES), SUBLANES)
                nr = lr * sr - li * si + x_ref[0, ks, r, :]
                ni = lr * si + li * sr + x_ref[1, ks, r, :]
                x_ref[0, ks, r, :] = nr
                x_ref[1, ks, r, :] = ni
                return nr, ni

            st_ref[d, 0, ks], st_ref[d, 1, ks] = lax.fori_loop(
                0, steps, body, (st_ref[d, 0, ks], st_ref[d, 1, ks]))

    for u, w_ref, c_ref, x_ref, y_ref, d in ((uf, wf_ref, cf_ref, xf_ref, yf_ref, 0),
                                             (ub, wb_ref, cb_ref, xb_ref, yb_ref, 1)):
        for k in range(nk):
            uk = u[:, k * ck:(k + 1) * ck].astype(BF16)
            for ri in range(2):
                x_ref[ri, k] = jnp.dot(uk, w_ref[ri, k], preferred_element_type=F32)
        scan(x_ref, d, d == 1)
        for k in range(nk):
            y_ref[:, k * ck:(k + 1) * ck] = sum(
                jnp.dot(x_ref[ri, k].astype(BF16), c_ref[ri, k], preferred_element_type=F32)
                for ri in range(2))


def _block_diag_in(bb, nk):
    g, p, h = bb.shape
    gpk = g // nk
    eye = jnp.eye(gpk, dtype=bb.dtype)
    w = jnp.einsum('kaph,ab->kahbp', bb.reshape(nk, gpk, p, h), eye)
    return w.reshape(nk, gpk * h, gpk * p)


def _block_diag_out(cc, nk):
    g, h, p = cc.shape
    gpk = g // nk
    eye = jnp.eye(gpk, dtype=cc.dtype)
    w = jnp.einsum('kahp,ab->kapbh', cc.reshape(nk, gpk, h, p), eye)
    return w.reshape(nk, gpk * p, gpk * h)


def s5_scan(uc, u, lam_r, lam_i, bb_r, bb_i, c_re, c_im, batch, steps=64, kpp=2):
    width = u.shape[1]
    nk = width // LANES
    _, g, p = lam_r.shape
    sk = g * p // nk
    rpc = steps * batch
    nc, nl = uc.shape[0] // rpc, u.shape[0] // rpc
    assert steps % (2 * nk) == 0 and uc.shape[0] % rpc == 0 and u.shape[0] % rpc == 0
    w_in = jnp.stack([jnp.stack([_block_diag_in(bb_r[d], nk), _block_diag_in(bb_i[d], nk)])
                      for d in range(2)]).astype(BF16)
    w_out = jnp.stack([jnp.stack([_block_diag_out(c_re[d], nk), -_block_diag_out(c_im[d], nk)])
                       for d in range(2)]).astype(BF16)
    lam = jnp.stack([lam_r.reshape(2, nk, 1, sk), lam_i.reshape(2, nk, 1, sk)], axis=1)
    lam = jnp.broadcast_to(lam, (2, 2, nk, batch, sk))
    blk = lambda f: pl.BlockSpec((rpc, width), lambda i: (f(i), 0))
    lat_f = lambda i: jnp.maximum(i - nc, 0)
    lat_b = lambda i: jnp.clip(nl - 1 - i + nc, 0, nl - 1)
    par = lambda a, d: pl.BlockSpec((None,) + a.shape[1:], lambda i: (d, 0, 0, 0, 0))
    shp = jax.ShapeDtypeStruct(u.shape, F32)
    return pl.pallas_call(
        functools.partial(_s5_scan_kernel, steps=steps, nc=nc, kpp=kpp),
        grid=(nc + nl,),
        in_specs=[blk(lambda i: jnp.minimum(i, nc - 1)), blk(lat_f),
                  blk(lambda i: jnp.maximum(nc - 1 - i, 0)), blk(lat_b),
                  par(w_in, 0), par(w_in, 1), par(w_out, 0), par(w_out, 1),
                  pl.BlockSpec(lam.shape, lambda i: (0, 0, 0, 0, 0))],
        out_specs=[blk(lat_f), blk(lat_b)],
        out_shape=[shp, shp],
        scratch_shapes=[pltpu.VMEM((2, nk, rpc, sk), F32), pltpu.VMEM((2, nk, rpc, sk), F32),
                        pltpu.VMEM((2, 2, nk, batch, sk), F32)],
        compiler_params=_cparams(("arbitrary",), VMEM_LIMIT_BYTES),
        name="s5_scan",
    )(uc, u, uc, u, w_in, w_in, w_out, w_out, lam)


def _mixer_tail_kernel(x_ref, hy_ref, ys_ref, u_ref, d_ref, gw_ref, gb_ref, woh_ref, wos_ref,
                       g1_ref, n2_ref, sh2_ref, sc2_ref, rwt_ref, x1_ref, h2_ref, lg_ref):
    y = ys_ref[...] + d_ref[...] * u_ref[...]
    y = 0.5 * y * (1.0 + jnp.tanh(math.sqrt(2.0 / math.pi) * (y + 0.044715 * (y * y * y))))
    gate = jnp.dot(y.astype(BF16), gw_ref[...], preferred_element_type=F32) + gb_ref[...]
    s5 = y * (1.0 / (1.0 + jnp.exp(-gate)))
    mix = (jnp.dot(hy_ref[...].astype(BF16), woh_ref[...], preferred_element_type=F32)
           + jnp.dot(s5.astype(BF16), wos_ref[...], preferred_element_type=F32))
    x1 = x_ref[...] + g1_ref[...] * mix
    x1_ref[...] = x1
    h2 = _rmsnorm(x1, n2_ref[...]) * (1.0 + sc2_ref[...]) + sh2_ref[...]
    h2_ref[...] = h2.astype(BF16)
    lg_ref[...] = lax.dot_general(rwt_ref[...], h2, (((1,), (1,)), ((), ())), precision=HIGHEST,
                                  preferred_element_type=F32)


def mixer_tail(x, hy, ys, u, s5_d, glu_w, glu_b, w_out, g1, norm2_g, sh2, sc2, router_w, ts=512):
    b, s, d = x.shape
    hw = hy.shape[2]
    sw = u.shape[2]
    ne = router_w.shape[1]
    tok = lambda n: pl.BlockSpec((None, ts, n), lambda i, j: (i, j, 0))
    vec = lambda n: pl.BlockSpec((1, n), lambda i, j: (0, 0))
    per_b = pl.BlockSpec((None, 1, d), lambda i, j: (i, 0, 0))
    mat = lambda r, c: pl.BlockSpec((r, c), lambda i, j: (0, 0))
    return pl.pallas_call(
        _mixer_tail_kernel,
        grid=(b, s // ts),
        in_specs=[tok(d), tok(hw), tok(sw), tok(sw), vec(sw), mat(sw, sw), vec(sw),
                  mat(hw, d), mat(sw, d), per_b, vec(d), per_b, per_b, mat(ne, d)],
        out_specs=[tok(d), tok(d), pl.BlockSpec((None, ne, ts), lambda i, j: (i, 0, j))],
        out_shape=[jax.ShapeDtypeStruct((b, s, d), F32), jax.ShapeDtypeStruct((b, s, d), BF16),
                   jax.ShapeDtypeStruct((b, ne, s), F32)],
        compiler_params=_cparams(("arbitrary", "arbitrary")),
        name="mixer_tail",
    )(x, hy, ys, u, s5_d.reshape(1, sw), glu_w.astype(BF16), glu_b.reshape(1, sw),
      w_out[:hw].astype(BF16), w_out[hw:].astype(BF16), g1, norm2_g.reshape(1, d), sh2, sc2,
      router_w.T)


def _lane_cumsum_exclusive(x):
    rows, s = x.shape
    ii = lax.broadcasted_iota(I32, (LANES, LANES), 0)
    jj = lax.broadcasted_iota(I32, (LANES, LANES), 1)
    tri = jnp.where(ii < jj, 1.0, 0.0).astype(BF16)
    carry = jnp.zeros((rows, 1), F32)
    out, base = [], []
    for blk in range(s // LANES):
        xb = x[:, blk * LANES:(blk + 1) * LANES]
        out.append(jnp.dot(xb.astype(BF16), tri, preferred_element_type=F32) + carry)
        base.append(carry)
        carry = carry + jnp.sum(xb, axis=1, keepdims=True)
    return jnp.concatenate(out, axis=1), jnp.concatenate(base, axis=1)


def _route_kernel(lg_ref, pos_em_ref, pos_tm_ref, gate_tm_ref, base_ref, aff_ref, *, cap):
    lg = lg_ref[...]
    ne, s = lg.shape
    ex = jnp.exp(lg - jnp.max(lg, axis=0, keepdims=True))
    aff_ref[...] = ex / jnp.sum(ex, axis=0, keepdims=True)
    aff = aff_ref[...]
    count_ge = lambda v, t: jnp.sum(jnp.where(v >= t, 1.0, 0.0), axis=1, keepdims=True)

    def coarse(i, tb):
        cand = tb | jnp.left_shift(jnp.int32(1), 30 - i)
        return jnp.where(count_ge(aff, pltpu.bitcast(cand, F32)) >= cap, cand, tb)

    tb = lax.fori_loop(0, 31, coarse, jnp.zeros((ne, 1), I32))
    t_hi = pltpu.bitcast(tb, F32)
    ulp = pltpu.bitcast(tb + 1, F32) - t_hi
    resid = aff - t_hi

    def fine(j, carry):
        c, step = carry
        cand = c + step
        return jnp.where(count_ge(resid, cand) >= cap, cand, c), step * 0.5

    t_lo, _ = lax.fori_loop(0, 24, fine, (jnp.zeros((ne, 1), F32), ulp * 0.5))
    gt = resid > t_lo
    eq = resid == t_lo
    need = cap - jnp.sum(jnp.where(gt, 1.0, 0.0), axis=1, keepdims=True)
    eq_rank, _ = _lane_cumsum_exclusive(jnp.where(eq, 1.0, 0.0))
    sel = gt | (eq & (eq_rank < need))
    pos, base = _lane_cumsum_exclusive(jnp.where(sel, 1.0, 0.0))
    posf = jnp.where(sel, pos + 1.0, 0.0)
    gate = jnp.where(sel, aff, 0.0)
    pos_em_ref[...] = posf.astype(I32) - 1
    base_ref[...] = base.astype(I32)
    hi = jnp.floor(posf * (1.0 / 16.0))
    lo = posf - 16.0 * hi
    ii = lax.broadcasted_iota(I32, (LANES, LANES), 0)
    jj = lax.broadcasted_iota(I32, (LANES, LANES), 1)
    eye = jnp.where(ii == jj, 1.0, 0.0)
    nt = (((1,), (1,)), ((), ()))
    for blk in range(s // LANES):
        sl = slice(blk * LANES, (blk + 1) * LANES)
        t_hi = lax.dot_general(eye.astype(BF16), hi[:, sl].astype(BF16), nt, preferred_element_type=F32)
        t_lo = lax.dot_general(eye.astype(BF16), lo[:, sl].astype(BF16), nt, preferred_element_type=F32)
        pos_tm_ref[sl, :] = (16.0 * t_hi + t_lo).astype(I32) - 1
        gate_tm_ref[sl, :] = lax.dot_general(eye, gate[:, sl], nt, precision=HIGHEST,
                                             preferred_element_type=F32)


def route(logits, cap):
    b, ne, s = logits.shape
    nb = s // LANES
    return pl.pallas_call(
        functools.partial(_route_kernel, cap=cap),
        grid=(b,),
        in_specs=[pl.BlockSpec((None, ne, s), lambda i: (i, 0, 0))],
        out_specs=[pl.BlockSpec((None, ne, s), lambda i: (i, 0, 0)),
                   pl.BlockSpec((None, s, ne), lambda i: (i, 0, 0)),
                   pl.BlockSpec((None, s, ne), lambda i: (i, 0, 0)),
                   pl.BlockSpec((None, ne, nb), lambda i: (i, 0, 0))],
        out_shape=[jax.ShapeDtypeStruct((b, ne, s), I32), jax.ShapeDtypeStruct((b, s, ne), I32),
                   jax.ShapeDtypeStruct((b, s, ne), F32), jax.ShapeDtypeStruct((b, ne, nb), I32)],
        scratch_shapes=[pltpu.VMEM((ne, s), F32)],
        compiler_params=_cparams(("arbitrary",)),
        name="route",
    )(logits)


SLOT_ALIGN = 16


def _slot_windows(base, cap, chunk, win):
    lo = base[:, :, ::chunk // LANES]
    hi = jnp.concatenate([lo[:, :, 1:], jnp.full_like(lo[:, :, :1], cap)], axis=2)
    start = (lo // SLOT_ALIGN) * SLOT_ALIGN
    nwin = jnp.max((hi - start + win - 1) // win, axis=1)
    return jnp.transpose(start, (0, 2, 1)).reshape(-1), nwin.reshape(-1)


def _window(st_ref, idx, w, win, cap):
    first = st_ref[idx] + w * win
    return first, pl.multiple_of(jnp.minimum(first, cap - win), SLOT_ALIGN)


def _gather_kernel(st_ref, nw_ref, pos_ref, h_ref, o_ref, *, win):
    b, j, nch = pl.program_id(0), pl.program_id(1), pl.num_programs(1)
    ne, cap, _ = o_ref.shape
    tk = h_ref.shape[0]

    @pl.when(j == 0)
    def _():
        o_ref[...] = jnp.zeros_like(o_ref)

    pos = pos_ref[...]
    h = h_ref[...]
    row = lax.broadcasted_iota(I32, (win, tk), 0)

    def window(w, carry):
        starts, lhs = [], []
        for e in range(ne):
            first, start = _window(st_ref, (b * nch + j) * ne + e, w, win, cap)
            slot = row + start
            hit = (pos[e:e + 1, :] == slot) & (slot >= first)
            lhs.append(jnp.where(hit, 1.0, 0.0).astype(BF16))
            starts.append(start)
        res = jnp.dot(jnp.concatenate(lhs, axis=0), h, preferred_element_type=F32)
        for e, start in enumerate(starts):
            o_ref[e, pl.ds(start, win), :] += res[e * win:(e + 1) * win].astype(o_ref.dtype)
        return carry

    lax.fori_loop(0, nw_ref[b * nch + j], window, 0)


def moe_gather(pos_em, h2, base, cap, tk=256, win=64):
    b, ne, s = pos_em.shape
    d = h2.shape[2]
    starts, nwin = _slot_windows(base, cap, tk, win)
    return pl.pallas_call(
        functools.partial(_gather_kernel, win=win),
        grid_spec=pltpu.PrefetchScalarGridSpec(
            num_scalar_prefetch=2,
            grid=(b, s // tk),
            in_specs=[pl.BlockSpec((None, ne, tk), lambda i, j, st, nw: (i, 0, j)),
                      pl.BlockSpec((None, tk, d), lambda i, j, st, nw: (i, j, 0))],
            out_specs=pl.BlockSpec((ne, cap, d), lambda i, j, st, nw: (0, i, 0))),
        out_shape=jax.ShapeDtypeStruct((ne, b * cap, d), BF16),
        compiler_params=_cparams(("arbitrary", "arbitrary"), VMEM_LIMIT_BYTES),
        name="moe_gather",
    )(starts, nwin, pos_em, h2)


def _ffn_kernel(x_ref, wg_ref, wu_ref, wd_ref, o_ref, acc_ref, *, sub):
    f = pl.program_id(2)

    @pl.when(f == 0)
    def _():
        acc_ref[...] = jnp.zeros_like(acc_ref)

    wg = wg_ref[...].astype(BF16)
    wu = wu_ref[...].astype(BF16)
    wd = wd_ref[...].astype(BF16)
    for r in range(x_ref.shape[0] // sub):
        rows = pl.ds(r * sub, sub)
        x = x_ref[rows, :]
        g = jnp.dot(x, wg, preferred_element_type=F32)
        u = jnp.dot(x, wu, preferred_element_type=F32)
        h = (_silu(g) * u).astype(BF16)
        acc_ref[rows, :] += jnp.dot(h, wd, preferred_element_type=F32)

    @pl.when(f == pl.num_programs(2) - 1)
    def _():
        o_ref[...] = acc_ref[...].astype(o_ref.dtype)


def moe_ffn(xe, w_gate, w_up, w_down, tm=2048, tf=256, sub=512):
    ne, m, d = xe.shape
    ff = w_gate.shape[2]
    tm = min(tm, m)
    return pl.pallas_call(
        functools.partial(_ffn_kernel, sub=min(sub, tm)),
        grid=(ne, m // tm, ff // tf),
        in_specs=[pl.BlockSpec((None, tm, d), lambda e, i, f: (e, i, 0)),
                  pl.BlockSpec((None, d, tf), lambda e, i, f: (e, 0, f)),
                  pl.BlockSpec((None, d, tf), lambda e, i, f: (e, 0, f)),
                  pl.BlockSpec((None, tf, d), lambda e, i, f: (e, f, 0))],
        out_specs=pl.BlockSpec((None, tm, d), lambda e, i, f: (e, i, 0)),
        out_shape=jax.ShapeDtypeStruct((ne, m, d), BF16),
        scratch_shapes=[pltpu.VMEM((tm, d), F32)],
        compiler_params=_cparams(("arbitrary", "arbitrary", "arbitrary"), VMEM_LIMIT_BYTES),
        name="moe_ffn",
    )(xe, w_gate, w_up, w_down)


def _combine_kernel(st_ref, nw_ref, pos_ref, gate_ref, ye_ref, x1_ref, g2_ref, fg_ref, o_ref, acc_ref,
                    *, win):
    b, j, nt = pl.program_id(0), pl.program_id(1), pl.num_programs(1)
    tt, ne = pos_ref.shape
    cap = ye_ref.shape[1]
    acc_ref[...] = jnp.zeros_like(acc_ref)
    pos = pos_ref[...]
    gate = gate_ref[...]
    col = lax.broadcasted_iota(I32, (tt, win), 1)

    def window(w, carry):
        acc = None
        for p in range(ne // 2):
            lhs, rhs = [], []
            for e in (2 * p, 2 * p + 1):
                first, start = _window(st_ref, (b * nt + j) * ne + e, w, win, cap)
                slot = col + start
                hit = (pos[:, e:e + 1] == slot) & (slot >= first)
                lhs.append(jnp.where(hit, gate[:, e:e + 1], 0.0).astype(BF16))
                rhs.append(ye_ref[e, pl.ds(start, win), :])
            part = jnp.dot(jnp.concatenate(lhs, axis=1), jnp.concatenate(rhs, axis=0),
                           preferred_element_type=F32)
            acc = part if acc is None else acc + part
        acc_ref[...] += acc
        return carry

    lax.fori_loop(0, nw_ref[b * nt + j], window, 0)
    xo = x1_ref[...] + g2_ref[...] * acc_ref[...]
    o_ref[...] = _rmsnorm(xo, fg_ref[...])


def moe_combine(pos_tm, gate_tm, base, ye, x1, g2, final_g, cap, tt=512, win=128):
    b, s, ne = pos_tm.shape
    d = x1.shape[2]
    starts, nwin = _slot_windows(base, cap, tt, win)
    tok = lambda n: pl.BlockSpec((None, tt, n), lambda i, j, st, nw: (i, j, 0))
    return pl.pallas_call(
        functools.partial(_combine_kernel, win=win),
        grid_spec=pltpu.PrefetchScalarGridSpec(
            num_scalar_prefetch=2,
            grid=(b, s // tt),
            in_specs=[tok(ne), tok(ne),
                      pl.BlockSpec((ne, cap, d), lambda i, j, st, nw: (0, i, 0)),
                      tok(d),
                      pl.BlockSpec((None, 1, d), lambda i, j, st, nw: (i, 0, 0)),
                      pl.BlockSpec((1, d), lambda i, j, st, nw: (0, 0))],
            out_specs=tok(d),
            scratch_shapes=[pltpu.VMEM((tt, d), F32)]),
        out_shape=jax.ShapeDtypeStruct((b, s, d), F32),
        compiler_params=_cparams(("arbitrary", "arbitrary"), VMEM_LIMIT_BYTES),
        name="moe_combine",
    )(starts, nwin, pos_tm, gate_tm, ye, x1, g2, final_g.reshape(1, d))


def _layer(x, ctx, mods, norm1_g, norm2_g, w_in, w_out, conv_w, conv_b, filt, hy_bias, s5p,
           s5_c_re, s5_c_im, s5_d, s5_glu_w, s5_glu_b, router_w, ex_w_gate, ex_w_up, ex_w_down,
           final_g):
    b, s, d = x.shape
    n_order, hw = hy_bias.shape
    hy_cols = (n_order + 1) * hw
    sw = w_in.shape[1] - hy_cols
    rows = s // GRID_W
    ne = router_w.shape[1]
    cap = CAPACITY_FACTOR * s // ne

    per_b = lambda k: mods[:b, k * d:(k + 1) * d].reshape(b, 1, d)
    ctx_v = lambda k: jnp.broadcast_to(mods[b:b + 1, k * d:(k + 1) * d].reshape(1, 1, d), (b, 1, d))
    sh1, sc1, g1, sh2, sc2, g2 = [per_b(k) for k in range(N_MOD)]

    w_in_bf = w_in.astype(BF16)
    (u_ctx,) = inproj(ctx, norm1_g, ctx_v(0), ctx_v(1), w_in_bf[:, hy_cols:], (sw,))
    z_hy, u = inproj(x, norm1_g, sh1, sc1, w_in_bf, (hy_cols, sw), conv=(0, conv_w, conv_b, GRID_W))

    hraw = hyena_filter_mlp(s, *filt)
    kf = hyena_filter_spectra(hraw, n_order, hw)
    y1 = hyena_conv(z_hy, 0, z_hy, 1, kf[0], hy_bias[0])
    hy = hyena_conv(y1, 0, z_hy, 2, kf[1], hy_bias[1])

    lam_r, lam_i, bb_r, bb_i = s5p
    tmaj = lambda a: jnp.transpose(a, (1, 0, 2)).reshape(-1, sw)
    yf, yb = s5_scan(tmaj(u_ctx), tmaj(u), lam_r, lam_i, bb_r, bb_i, s5_c_re, s5_c_im, b)
    ys = jnp.transpose((yf + yb).reshape(s, b, sw), (1, 0, 2))

    x1, h2, logits = mixer_tail(x, hy, ys, u, s5_d, s5_glu_w, s5_glu_b, w_out, g1, norm2_g,
                                sh2, sc2, router_w)
    pos_em, pos_tm, gate_tm, base = route(logits, cap)
    xe = moe_gather(pos_em, h2, base, cap)
    ye = moe_ffn(xe, ex_w_gate, ex_w_up, ex_w_down)
    return moe_combine(pos_tm, gate_tm, base, ye, x1, g2, final_g, cap)


def kernel(x, c, ctx, c_ctx, mod_w, mod_b, norm1_g, norm2_g, w_in, w_out, conv_w, conv_b, hy_w1, hy_b1, hy_freq, hy_w2, hy_b2, hy_w3, hy_b3, hy_bias, s5_lam_re, s5_lam_im, s5_log_step, s5_b_re, s5_b_im, s5_c_re, s5_c_im, s5_d, s5_glu_w, s5_glu_b, router_w, ex_w_gate, ex_w_up, ex_w_down, final_g):
    depth = mod_w.shape[0]
    assert depth == 1, "context-token updates of non-final layers are not implemented"
    b, _, d = x.shape
    l = 0
    pad = (-(b + 1)) % SUBLANES
    cond = jnp.concatenate([c, c_ctx[None], jnp.zeros((pad, d), F32)], axis=0)
    mods = adaln_mods(cond, mod_w[l], mod_b[l])
    filt = (hy_w1[l], hy_b1[l], hy_freq[l], hy_w2[l], hy_b2[l], hy_w3[l], hy_b3[l])
    s5p = s5_discretize(s5_lam_re[l], s5_lam_im[l], s5_log_step[l], s5_b_re[l], s5_b_im[l])
    return _layer(x, ctx, mods, norm1_g[l], norm2_g[l], w_in[l], w_out[l], conv_w[l], conv_b[l],
                  filt, hy_bias[l], s5p, s5_c_re[l], s5_c_im[l], s5_d[l], s5_glu_w[l], s5_glu_b[l],
                  router_w[l], ex_w_gate[l], ex_w_up[l], ex_w_down[l], final_g)
```

```python
import functools
import math

import numpy as np
import jax
import jax.numpy as jnp
from jax import lax
from jax.experimental import pallas as pl
from jax.experimental.pallas import tpu as pltpu

F32 = jnp.float32
BF16 = jnp.bfloat16
I32 = jnp.int32
HIGHEST = lax.Precision.HIGHEST

SUBLANES = 8
LANES = 128
VMEM_LIMIT_BYTES = 58 * 1024 * 1024

GRID_W = 64
N_MOD = 6
NORM_EPS = 1e-6
POS_BANDS = 16
DECAY_FAST = 0.3
DECAY_SLOW = 1.5
DECAY_TARGET = 1e-2
CAPACITY_FACTOR = 2

DFT_N1 = 64
DFT_N2 = 128
DFT_K1 = DFT_N1 // 2 + 1
DFT_ROWS = 2 * DFT_K1

S5_CHUNK = 4


def _cparams(sem, vmem=None):
    return pltpu.CompilerParams(dimension_semantics=sem, vmem_limit_bytes=vmem)


def _silu(x):
    return x * (1.0 / (1.0 + jnp.exp(-x)))


def _rmsnorm(x, g):
    ms = jnp.mean(x * x, axis=-1, keepdims=True)
    return x * lax.rsqrt(ms + NORM_EPS) * g


def _adaln_kernel(c_ref, w_ref, b_ref, o_ref):
    s = _silu(c_ref[...])
    o_ref[...] = jnp.dot(s, w_ref[...], precision=HIGHEST, preferred_element_type=F32) + b_ref[...]


def adaln_mods(cond, mod_w, mod_b, tn=1536):
    rows, d = cond.shape
    n = mod_w.shape[1]
    return pl.pallas_call(
        _adaln_kernel,
        grid=(n // tn,),
        in_specs=[pl.BlockSpec((rows, d), lambda j: (0, 0)),
                  pl.BlockSpec((d, tn), lambda j: (0, j)),
                  pl.BlockSpec((1, tn), lambda j: (0, j))],
        out_specs=pl.BlockSpec((rows, tn), lambda j: (0, j)),
        out_shape=jax.ShapeDtypeStruct((rows, n), F32),
        compiler_params=_cparams(("arbitrary",)),
        name="adaln",
    )(cond, mod_w, mod_b.reshape(1, n))


def _short_conv(z, w_ref, b_ref, row_len):
    length = z.shape[0]
    pos = lax.broadcasted_iota(I32, (length, 1), 0) % row_len
    zm = jnp.where(pos == 0, 0.0, pltpu.roll(z, 1, 0))
    zp = jnp.where(pos == row_len - 1, 0.0, pltpu.roll(z, length - 1, 0))
    w = w_ref[...]
    return zm * w[0:1, :] + z * w[1:2, :] + zp * w[2:3, :] + b_ref[...]


def _inproj_kernel(x_ref, g_ref, sh_ref, sc_ref, w_ref, *refs, splits, conv_split, row_len):
    conv_refs, o_refs = refs[:len(refs) - len(splits)], refs[len(refs) - len(splits):]
    h = _rmsnorm(x_ref[...], g_ref[...])
    h = h * (1.0 + sc_ref[...]) + sh_ref[...]
    z = jnp.dot(h.astype(BF16), w_ref[...], preferred_element_type=F32)
    off = 0
    for idx, (o_ref, n) in enumerate(zip(o_refs, splits)):
        part = z[:, off:off + n]
        if idx == conv_split:
            part = _short_conv(part, *conv_refs, row_len)
        o_ref[...] = part
        off += n


def inproj(x, g, shift, scale, w_bf16, splits, conv=None, ts=512):
    b, s, d = x.shape
    n = w_bf16.shape[1]
    ts = min(ts, s)
    conv_split, conv_args, conv_specs, row_len = None, [], [], 1
    if conv is not None:
        conv_split, cw, cb, row_len = conv
        assert ts % row_len == 0
        conv_args = [cw, cb.reshape(1, -1)]
        conv_specs = [pl.BlockSpec(a.shape, lambda i, j: (0, 0)) for a in conv_args]
    out_specs = [pl.BlockSpec((None, ts, m), lambda i, j: (i, j, 0)) for m in splits]
    out_shape = [jax.ShapeDtypeStruct((b, s, m), F32) for m in splits]
    return pl.pallas_call(
        functools.partial(_inproj_kernel, splits=splits, conv_split=conv_split, row_len=row_len),
        grid=(b, s // ts),
        in_specs=[pl.BlockSpec((None, ts, d), lambda i, j: (i, j, 0)),
                  pl.BlockSpec((1, d), lambda i, j: (0, 0)),
                  pl.BlockSpec((None, 1, d), lambda i, j: (i, 0, 0)),
                  pl.BlockSpec((None, 1, d), lambda i, j: (i, 0, 0)),
                  pl.BlockSpec((d, n), lambda i, j: (0, 0))] + conv_specs,
        out_specs=out_specs,
        out_shape=out_shape,
        compiler_params=_cparams(("arbitrary", "arbitrary")),
        name="inproj",
    )(x, g.reshape(1, d), shift, scale, w_bf16, *conv_args)


def _filt_mlp_kernel(w1t_ref, w1c_ref, w1s_ref, b1_ref, fr_ref, w2_ref, b2_ref, w3_ref, b3_ref,
                     o_ref, *, length, tl):
    i0 = pl.program_id(0) * tl
    idx = (lax.broadcasted_iota(I32, (tl, 1), 0) + i0).astype(F32)
    t = idx / float(length - 1)
    omega = (2.0 * math.pi) * idx / float(length)
    fstep = ((POS_BANDS - 1) - 1e-4) / (POS_BANDS - 1)
    f = 1e-4 + lax.broadcasted_iota(I32, (1, POS_BANDS), 1).astype(F32) * fstep
    arg = omega * f
    pre = (t * w1t_ref[...]
           + jnp.dot(jnp.cos(arg), w1c_ref[...], precision=HIGHEST, preferred_element_type=F32)
           - jnp.dot(jnp.sin(arg), w1s_ref[...], precision=HIGHEST, preferred_element_type=F32)
           + b1_ref[...])
    fr = fr_ref[...]
    h = jnp.sin(fr[0:1, :] * pre)
    h = jnp.sin(fr[1:2, :] * (jnp.dot(h, w2_ref[...], precision=HIGHEST,
                                      preferred_element_type=F32) + b2_ref[...]))
    o_ref[...] = jnp.dot(h, w3_ref[...], precision=HIGHEST, preferred_element_type=F32) + b3_ref[...]


def hyena_filter_mlp(length, w1, b1, freq, w2, b2, w3, b3, tl=512):
    fw = w1.shape[1]
    n = w3.shape[1]
    full = lambda shape: pl.BlockSpec(shape, lambda i: (0, 0))
    return pl.pallas_call(
        functools.partial(_filt_mlp_kernel, length=length, tl=tl),
        grid=(length // tl,),
        in_specs=[full((1, fw)), full((POS_BANDS, fw)), full((POS_BANDS, fw)), full((1, fw)),
                  full((2, fw)), full((fw, fw)), full((1, fw)), full((fw, n)), full((1, n))],
        out_specs=pl.BlockSpec((tl, n), lambda i: (i, 0)),
        out_shape=jax.ShapeDtypeStruct((length, n), F32),
        compiler_params=_cparams(("arbitrary",)),
        name="hyena_filter_mlp",
    )(w1[0:1], w1[1:1 + POS_BANDS], w1[1 + POS_BANDS:], b1.reshape(1, fw), freq, w2,
      b2.reshape(1, fw), w3, b3.reshape(1, n))


@functools.lru_cache(maxsize=None)
def _dft_tables():
    n1n, n2n, k1n = DFT_N1, DFT_N2, DFT_K1
    n = n1n * n2n
    half = n1n // 2
    k1 = np.arange(k1n)[:, None]
    n1 = np.arange(half)[None, :]
    th = 2.0 * np.pi * k1 * n1 / n1n
    f1 = np.zeros((DFT_ROWS, half))
    f1[0::2] = np.cos(th)
    f1[1::2] = -np.sin(th)
    wgt = np.where((k1 == 0) | (k1 == half), 1.0, 2.0)
    g1 = np.zeros((half, DFT_ROWS))
    g1[:, 0::2] = (wgt * np.cos(th)).T / n
    g1[:, 1::2] = (-wgt * np.sin(th)).T / n
    eye = np.eye(SUBLANES)
    fk = np.kron(f1, eye)
    gk = np.kron(g1, eye)
    k2 = np.arange(n2n)[:, None]
    n2 = np.arange(n2n)[None, :]
    f3 = np.zeros((k1n, 2 * n2n, 2 * n2n))
    for kk in range(k1n):
        ph = 2.0 * np.pi * n2 * (n1n * k2 + kk) / n
        tr, ti = np.cos(ph), -np.sin(ph)
        f3[kk, :n2n, :n2n] = tr
        f3[kk, :n2n, n2n:] = -ti
        f3[kk, n2n:, :n2n] = ti
        f3[kk, n2n:, n2n:] = tr
    g3 = np.transpose(f3, (0, 2, 1))
    to = lambda a: jnp.asarray(a, dtype=F32).astype(BF16)
    return to(fk), to(gk), to(f3), to(g3)


def _dft_stage1(src_ref, a_ref, fk_ref):
    half = DFT_N1 // 2

    def body(m, carry):
        sub = pl.ds(pl.multiple_of(m * SUBLANES, SUBLANES), SUBLANES)
        rows = [src_ref.at[pl.ds(DFT_N2 * n1, DFT_N2)][sub, :] for n1 in range(half)]
        rhs = jnp.concatenate(rows, axis=0).astype(BF16)
        out = jnp.dot(fk_ref[...], rhs, preferred_element_type=F32)
        for j in range(DFT_ROWS):
            a_ref.at[pl.ds(DFT_N2 * j, DFT_N2)][sub, :] = out[SUBLANES * j:SUBLANES * (j + 1)]
        return carry

    lax.fori_loop(0, DFT_N2 // SUBLANES, body, 0, unroll=2)


def _dft_stage3(a_ref, f3_ref, k1):
    r0 = pl.multiple_of(k1 * (2 * DFT_N2), 2 * DFT_N2)
    a = a_ref[pl.ds(r0, 2 * DFT_N2), :].astype(BF16)
    x = jnp.dot(f3_ref[k1], a, preferred_element_type=F32)
    return x[:DFT_N2], x[DFT_N2:]


def _filt_spec_kernel(hf_ref, hb_ref, fk_ref, f3_ref, o_ref, src_ref, af_ref, ab_ref, *, length):
    c = hf_ref.shape[1]
    cb = pl.program_id(1)
    hw = pl.num_programs(1) * c
    row = lax.broadcasted_iota(I32, (length, 1), 0)
    t = row.astype(F32) / float(length - 1)
    ch = (lax.broadcasted_iota(I32, (1, c), 1) + cb * c).astype(F32)
    d0 = math.log(DECAY_TARGET) / DECAY_FAST
    d1 = math.log(DECAY_TARGET) / DECAY_SLOW
    deltas = jnp.abs(d0 + ch * ((d1 - d0) / float(hw - 1)))
    decay = jnp.exp(-t * deltas)
    fwd = hf_ref[...] * decay
    bwd = jnp.where(row == 0, 0.0, hb_ref[...] * decay)
    inv = 1.0 / (jnp.sum(jnp.abs(fwd), axis=0, keepdims=True)
                 + jnp.sum(jnp.abs(bwd), axis=0, keepdims=True))
    src_ref[...] = fwd
    _dft_stage1(src_ref, af_ref, fk_ref)
    src_ref[...] = bwd
    _dft_stage1(src_ref, ab_ref, fk_ref)

    def body(k1, carry):
        fr, fi = _dft_stage3(af_ref, f3_ref, k1)
        br, bi = _dft_stage3(ab_ref, f3_ref, k1)
        o_ref[k1, 0] = ((fr + br) * inv).astype(o_ref.dtype)
        o_ref[k1, 1] = ((fi - bi) * inv).astype(o_ref.dtype)
        return carry

    lax.fori_loop(0, DFT_K1, body, 0)


def hyena_filter_spectra(hraw, n_order, width, c_blk=256):
    length = hraw.shape[0]
    assert 2 * length == DFT_N1 * DFT_N2
    fk, _, f3, _ = _dft_tables()
    ncb = width // c_blk
    return pl.pallas_call(
        functools.partial(_filt_spec_kernel, length=length),
        grid=(n_order, ncb),
        in_specs=[pl.BlockSpec((length, c_blk), lambda o, j: (0, o * 2 * ncb + j)),
                  pl.BlockSpec((length, c_blk), lambda o, j: (0, o * 2 * ncb + ncb + j)),
                  pl.BlockSpec(fk.shape, lambda o, j: (0, 0)),
                  pl.BlockSpec(f3.shape, lambda o, j: (0, 0, 0))],
        out_specs=pl.BlockSpec((None, DFT_K1, 2, DFT_N2, c_blk), lambda o, j: (o, 0, 0, 0, j)),
        out_shape=jax.ShapeDtypeStruct((n_order, DFT_K1, 2, DFT_N2, width), BF16),
        scratch_shapes=[pltpu.VMEM((length, c_blk), F32),
                        pltpu.VMEM((DFT_ROWS * DFT_N2, c_blk), F32),
                        pltpu.VMEM((DFT_ROWS * DFT_N2, c_blk), F32)],
        compiler_params=_cparams(("arbitrary", "arbitrary"), VMEM_LIMIT_BYTES),
        name="hyena_filter_spectrum",
    )(hraw, hraw, fk, f3)


def _hyena_conv_kernel(s_ref, m_ref, kf_ref, bias_ref, fk_ref, f3_ref, g3_ref, gk_ref, o_ref, a_ref,
                       *, group):
    half = DFT_N1 // 2
    blk_rows = 2 * DFT_N2
    _dft_stage1(s_ref, a_ref, fk_ref)

    def body3(i, carry):
        k1s = [i * group + q for q in range(group)]
        r0s = [pl.multiple_of(k1 * blk_rows, blk_rows) for k1 in k1s]
        blocks = [a_ref[pl.ds(r0, blk_rows), :].astype(BF16) for r0 in r0s]
        outs = []
        for k1, a in zip(k1s, blocks):
            x = jnp.dot(f3_ref[k1], a, preferred_element_type=F32)
            xr, xi = x[:DFT_N2], x[DFT_N2:]
            kr = kf_ref[k1, 0].astype(F32)
            ki = kf_ref[k1, 1].astype(F32)
            y = jnp.concatenate([xr * kr - xi * ki, xr * ki + xi * kr], axis=0).astype(BF16)
            outs.append(jnp.dot(g3_ref[k1], y, preferred_element_type=F32))
        for r0, o in zip(r0s, outs):
            a_ref[pl.ds(r0, blk_rows), :] = o
        return carry

    lax.fori_loop(0, DFT_K1 // group, body3, 0)
    bias = bias_ref[...]

    def body1(m, carry):
        sub = pl.ds(pl.multiple_of(m * SUBLANES, SUBLANES), SUBLANES)
        blk = [a_ref.at[pl.ds(DFT_N2 * j, DFT_N2)][sub, :] for j in range(DFT_ROWS)]
        rhs = jnp.concatenate(blk, axis=0).astype(BF16)
        out = jnp.dot(gk_ref[...], rhs, preferred_element_type=F32)
        for n1 in range(half):
            blk_n1 = pl.ds(DFT_N2 * n1, DFT_N2)
            conv = out[SUBLANES * n1:SUBLANES * (n1 + 1)]
            o_ref.at[blk_n1][sub, :] = m_ref.at[blk_n1][sub, :] * (
                conv + s_ref.at[blk_n1][sub, :] * bias)
        return carry

    lax.fori_loop(0, DFT_N2 // SUBLANES, body1, 0, unroll=2)


def hyena_conv(sig, sig_col, mul, mul_col, kf, bias, c_blk=256, group=11):
    b, length, _ = sig.shape
    width = kf.shape[-1]
    ncb = width // c_blk
    assert DFT_K1 % group == 0
    fk, gk, f3, g3 = _dft_tables()
    once = pl.Buffered(1)
    const2 = lambda a: pl.BlockSpec(a.shape, lambda j, i: (0, 0), pipeline_mode=once)
    const3 = lambda a: pl.BlockSpec(a.shape, lambda j, i: (0, 0, 0), pipeline_mode=once)
    return pl.pallas_call(
        functools.partial(_hyena_conv_kernel, group=group),
        grid=(ncb, b),
        in_specs=[pl.BlockSpec((None, length, c_blk), lambda j, i: (i, 0, sig_col * ncb + j)),
                  pl.BlockSpec((None, length, c_blk), lambda j, i: (i, 0, mul_col * ncb + j)),
                  pl.BlockSpec((DFT_K1, 2, DFT_N2, c_blk), lambda j, i: (0, 0, 0, j),
                               pipeline_mode=once),
                  pl.BlockSpec((1, c_blk), lambda j, i: (0, j), pipeline_mode=once),
                  const2(fk), const3(f3), const3(g3), const2(gk)],
        out_specs=pl.BlockSpec((None, length, c_blk), lambda j, i: (i, 0, j)),
        out_shape=jax.ShapeDtypeStruct((b, length, width), F32),
        scratch_shapes=[pltpu.VMEM((DFT_ROWS * DFT_N2, c_blk), F32)],
        compiler_params=_cparams(("arbitrary", "arbitrary"), VMEM_LIMIT_BYTES),
        name="hyena_conv",
    )(sig, mul, kf, bias.reshape(1, width), fk, f3, g3, gk)


def _s5_powers_kernel(lr_ref, li_ref, dt_ref, vr_ref, vi_ref, or_ref, oi_ref, *, n_pow, zoh):
    lr, li, dt = lr_ref[...], li_ref[...], jnp.exp(dt_ref[...])
    mag = jnp.exp(lr * dt)
    ar = mag * jnp.cos(li * dt)
    ai = mag * jnp.sin(li * dt)
    vr, vi = vr_ref[...], vi_ref[...]
    if zoh:
        den = 1.0 / (lr * lr + li * li)
        qr = ((ar - 1.0) * lr + ai * li) * den
        qi = (ai * lr - (ar - 1.0) * li) * den
        vr, vi = qr * vr - qi * vi, qr * vi + qi * vr
    for j in range(n_pow):
        or_ref[j] = vr
        oi_ref[j] = vi
        vr, vi = ar * vr - ai * vi, ar * vi + ai * vr


def s5_powers(lam_re, lam_im, log_step, v_re, v_im, state_axis, n_pow, zoh):
    nd, g, a, b = v_re.shape
    expand = (lambda x: x[..., :, None]) if state_axis == 2 else (lambda x: x[..., None, :])
    rep = lambda x: jnp.broadcast_to(expand(x), v_re.shape).reshape(nd * g, a * b)
    dt = jnp.broadcast_to(log_step[:, :, None, None], v_re.shape).reshape(nd * g, a * b)
    flat = lambda x: x.reshape(nd * g, a * b)
    shp = jax.ShapeDtypeStruct((n_pow, nd * g, a * b), F32)
    o_r, o_i = pl.pallas_call(
        functools.partial(_s5_powers_kernel, n_pow=n_pow, zoh=zoh), out_shape=[shp, shp],
        name="s5_powers",
    )(rep(lam_re), rep(lam_im), dt, flat(v_re), flat(v_im))
    un = lambda x: x.reshape(n_pow, nd, g, a, b)
    return un(o_r), un(o_i)


def _s5_taps_kernel(cr_ref, ci_ref, br_ref, bi_ref, o_ref):
    o_ref[...] = (jnp.dot(cr_ref[...], br_ref[...], precision=HIGHEST, preferred_element_type=F32)
                  - jnp.dot(ci_ref[...], bi_ref[...], precision=HIGHEST, preferred_element_type=F32))


def s5_taps(c_re, c_im, bbp_r, bbp_i):
    nd, g, h, p = c_re.shape
    n_pow = bbp_r.shape[0]
    eye = jnp.eye(g, dtype=F32)
    bdiag = lambda c: jnp.einsum('dghp,gk->dghkp', c, eye).reshape(nd, g * h, g * p)
    flat = lambda x: x.reshape(n_pow, nd, g * p, h)
    cspec = pl.BlockSpec((None, g * h, g * p), lambda d, j: (d, 0, 0))
    bspec = pl.BlockSpec((None, None, g * p, h), lambda d, j: (j, d, 0, 0))
    out = pl.pallas_call(
        _s5_taps_kernel,
        grid=(nd, n_pow),
        in_specs=[cspec, cspec, bspec, bspec],
        out_specs=pl.BlockSpec((None, None, g * h, h), lambda d, j: (j, d, 0, 0)),
        out_shape=jax.ShapeDtypeStruct((n_pow, nd, g * h, h), F32),
        compiler_params=_cparams(("arbitrary", "arbitrary")),
        name="s5_taps",
    )(bdiag(c_re), bdiag(c_im), flat(bbp_r), flat(bbp_i))
    return out.reshape(n_pow, nd, g, h, h)


def _s5_scan_kernel(ucf_ref, uf_ref, ucb_ref, ub_ref, qf_ref, qb_ref, pf_ref, pb_ref, mf_ref, mb_ref,
                    lam_ref, yf_ref, yb_ref, xf_ref, xb_ref, st_ref, *, nc, kpp):
    cpg, tlen, batch, width = uf_ref.shape
    nk = xf_ref.shape[1]
    ck = width // nk
    rows = cpg * batch
    i = pl.program_id(0)
    is_ctx = i < nc

    @pl.when(i == 0)
    def _():
        st_ref[...] = jnp.zeros_like(st_ref)

    uf = jnp.where(is_ctx, ucf_ref[...], uf_ref[...])
    ub = jnp.where(is_ctx, ucb_ref[...], ub_ref[...])

    def scan(x_ref, d, reverse):
        for k0 in range(0, nk, kpp):
            ks = slice(k0, k0 + kpp)
            lr, li = lam_ref[d, 0, ks], lam_ref[d, 1, ks]

            def body(c, carry, ks=ks, lr=lr, li=li):
                sr, si = carry
                cc = (cpg - 1 - c) if reverse else c
                r = pl.ds(pl.multiple_of(cc * batch, batch), batch)
                qr, qi = x_ref[0, ks, r, :], x_ref[1, ks, r, :]
                x_ref[0, ks, r, :] = sr
                x_ref[1, ks, r, :] = si
                return lr * sr - li * si + qr, lr * si + li * sr + qi

            st_ref[d, 0, ks], st_ref[d, 1, ks] = lax.fori_loop(
                0, cpg, body, (st_ref[d, 0, ks], st_ref[d, 1, ks]))

    for u, q_ref, p_ref, m_ref, x_ref, y_ref, d in ((uf, qf_ref, pf_ref, mf_ref, xf_ref, yf_ref, 0),
                                                    (ub, qb_ref, pb_ref, mb_ref, xb_ref, yb_ref, 1)):
        uks = []
        for k in range(nk):
            uk = jnp.concatenate([u[:, t, :, k * ck:(k + 1) * ck].reshape(rows, ck)
                                  for t in range(tlen)], axis=1).astype(BF16)
            uks.append(uk)
            for ri in range(2):
                x_ref[ri, k] = jnp.dot(uk, q_ref[ri, k], preferred_element_type=F32)
        scan(x_ref, d, d == 1)
        for k in range(nk):
            yk = (jnp.dot(uks[k], m_ref[k], preferred_element_type=F32)
                  + jnp.dot(x_ref[0, k].astype(BF16), p_ref[0, k], preferred_element_type=F32)
                  + jnp.dot(x_ref[1, k].astype(BF16), p_ref[1, k], preferred_element_type=F32))
            for t in range(tlen):
                y_ref[:, t, :, k * ck:(k + 1) * ck] = yk[:, t * ck:(t + 1) * ck].reshape(
                    cpg, batch, ck)


def _block_diag_in(bb, nk):
    g, p, h = bb.shape
    gpk = g // nk
    eye = jnp.eye(gpk, dtype=bb.dtype)
    w = jnp.einsum('kaph,ab->kahbp', bb.reshape(nk, gpk, p, h), eye)
    return w.reshape(nk, gpk * h, gpk * p)


def _block_diag_out(cc, nk):
    g, h, p = cc.shape
    gpk = g // nk
    eye = jnp.eye(gpk, dtype=cc.dtype)
    w = jnp.einsum('kahp,ab->kapbh', cc.reshape(nk, gpk, h, p), eye)
    return w.reshape(nk, gpk * p, gpk * h)


def _block_diag_taps(kk, nk):
    g, h, _ = kk.shape
    gpk = g // nk
    eye = jnp.eye(gpk, dtype=kk.dtype)
    w = jnp.einsum('kahq,ab->kaqbh', kk.reshape(nk, gpk, h, h), eye)
    return w.reshape(nk, gpk * h, gpk * h)


def s5_chunk_weights(lam_re, lam_im, log_step, b_re, b_im, c_re, c_im, nk, tlen):
    nd, g, p, h = b_re.shape
    bb_r, bb_i = s5_powers(lam_re, lam_im, log_step, b_re, b_im, 2, tlen, True)
    cl_r, cl_i = s5_powers(lam_re, lam_im, log_step, c_re, c_im, 3, tlen + 1, False)
    ones = jnp.ones((nd, g, p, 2), F32)
    pw_r, pw_i = s5_powers(lam_re, lam_im, log_step, ones, jnp.zeros_like(ones), 2, tlen + 1, False)
    taps = s5_taps(c_re, c_im, bb_r, bb_i)
    sk = g * p // nk
    q, pm, m = [], [], []
    for d in range(nd):
        jq = (lambda s: tlen - 1 - s) if d == 0 else (lambda s: s)
        jp = (lambda t: t + 1) if d == 0 else (lambda t: tlen - t)
        q.append(jnp.stack([jnp.concatenate([_block_diag_in(x[jq(s), d], nk) for s in range(tlen)],
                                            axis=1) for x in (bb_r, bb_i)]))
        pm.append(jnp.stack([jnp.concatenate([_block_diag_out(sgn * x[jp(t), d], nk)
                                              for t in range(tlen)], axis=2)
                             for x, sgn in ((cl_r, 1.0), (cl_i, -1.0))]))
        zero = jnp.zeros((nk, g * h // nk, g * h // nk), F32)
        lag = (lambda s, t: t - s) if d == 0 else (lambda s, t: s - t)
        m.append(jnp.concatenate(
            [jnp.concatenate([_block_diag_taps(taps[lag(s, t), d], nk) if lag(s, t) >= 0 else zero
                              for t in range(tlen)], axis=2) for s in range(tlen)], axis=1))
    decay = jnp.stack([pw_r[tlen, ..., 0], pw_i[tlen, ..., 0]], axis=1).reshape(nd, 2, nk, 1, sk)
    return (jnp.stack(q).astype(BF16), jnp.stack(pm).astype(BF16), jnp.stack(m).astype(BF16), decay)


def s5_scan(uc, u, weights, batch, steps=64, kpp=2):
    q, pm, m, decay = weights
    width = u.shape[1]
    nk, sk = q.shape[2], q.shape[4]
    tlen = m.shape[2] * nk // width
    cpg = steps // tlen
    rpc = steps * batch
    nc, nl = uc.shape[0] // rpc, u.shape[0] // rpc
    assert steps % tlen == 0 and uc.shape[0] % rpc == 0 and u.shape[0] % rpc == 0
    lam = jnp.broadcast_to(decay, (2, 2, nk, batch, sk))
    chunked = lambda a: a.reshape(-1, tlen, batch, width)
    blk = lambda f: pl.BlockSpec((cpg, tlen, batch, width), lambda i: (f(i), 0, 0, 0))
    lat_f = lambda i: jnp.maximum(i - nc, 0)
    lat_b = lambda i: jnp.clip(nl - 1 - i + nc, 0, nl - 1)
    once = pl.Buffered(1)
    par = lambda a, d: pl.BlockSpec((None,) + a.shape[1:], lambda i: (d,) + (0,) * (a.ndim - 1),
                                    pipeline_mode=once)
    shp = jax.ShapeDtypeStruct((u.shape[0] // (tlen * batch), tlen, batch, width), F32)
    yf, yb = pl.pallas_call(
        functools.partial(_s5_scan_kernel, nc=nc, kpp=kpp),
        grid=(nc + nl,),
        in_specs=[blk(lambda i: jnp.minimum(i, nc - 1)), blk(lat_f),
                  blk(lambda i: jnp.maximum(nc - 1 - i, 0)), blk(lat_b),
                  par(q, 0), par(q, 1), par(pm, 0), par(pm, 1), par(m, 0), par(m, 1),
                  pl.BlockSpec(lam.shape, lambda i: (0, 0, 0, 0, 0), pipeline_mode=once)],
        out_specs=[blk(lat_f), blk(lat_b)],
        out_shape=[shp, shp],
        scratch_shapes=[pltpu.VMEM((2, nk, cpg * batch, sk), F32),
                        pltpu.VMEM((2, nk, cpg * batch, sk), F32),
                        pltpu.VMEM((2, 2, nk, batch, sk), F32)],
        compiler_params=_cparams(("arbitrary",), VMEM_LIMIT_BYTES),
        name="s5_scan",
    )(chunked(uc), chunked(u), chunked(uc), chunked(u), q, q, pm, pm, m, m, lam)
    return yf.reshape(u.shape), yb.reshape(u.shape)


def _mixer_tail_kernel(x_ref, hy_ref, ys_ref, u_ref, d_ref, gw_ref, gb_ref, woh_ref, wos_ref,
                       g1_ref, n2_ref, sh2_ref, sc2_ref, rwt_ref, x1_ref, h2_ref, lg_ref):
    y = ys_ref[...] + d_ref[...] * u_ref[...]
    y = 0.5 * y * (1.0 + jnp.tanh(math.sqrt(2.0 / math.pi) * (y + 0.044715 * (y * y * y))))
    gate = jnp.dot(y.astype(BF16), gw_ref[...], preferred_element_type=F32) + gb_ref[...]
    s5 = y * (1.0 / (1.0 + jnp.exp(-gate)))
    mix = (jnp.dot(hy_ref[...].astype(BF16), woh_ref[...], preferred_element_type=F32)
           + jnp.dot(s5.astype(BF16), wos_ref[...], preferred_element_type=F32))
    x1 = x_ref[...] + g1_ref[...] * mix
    x1_ref[...] = x1
    h2 = _rmsnorm(x1, n2_ref[...]) * (1.0 + sc2_ref[...]) + sh2_ref[...]
    h2_ref[...] = h2.astype(BF16)
    lg_ref[...] = lax.dot_general(rwt_ref[...], h2, (((1,), (1,)), ((), ())), precision=HIGHEST,
                                  preferred_element_type=F32)


def mixer_tail(x, hy, ys, u, s5_d, glu_w, glu_b, w_out, g1, norm2_g, sh2, sc2, router_w, ts=512):
    b, s, d = x.shape
    hw = hy.shape[2]
    sw = u.shape[2]
    ne = router_w.shape[1]
    tok = lambda n: pl.BlockSpec((None, ts, n), lambda i, j: (i, j, 0))
    vec = lambda n: pl.BlockSpec((1, n), lambda i, j: (0, 0))
    per_b = pl.BlockSpec((None, 1, d), lambda i, j: (i, 0, 0))
    mat = lambda r, c: pl.BlockSpec((r, c), lambda i, j: (0, 0))
    return pl.pallas_call(
        _mixer_tail_kernel,
        grid=(b, s // ts),
        in_specs=[tok(d), tok(hw), tok(sw), tok(sw), vec(sw), mat(sw, sw), vec(sw),
                  mat(hw, d), mat(sw, d), per_b, vec(d), per_b, per_b, mat(ne, d)],
        out_specs=[tok(d), tok(d), pl.BlockSpec((None, ne, ts), lambda i, j: (i, 0, j))],
        out_shape=[jax.ShapeDtypeStruct((b, s, d), F32), jax.ShapeDtypeStruct((b, s, d), BF16),
                   jax.ShapeDtypeStruct((b, ne, s), F32)],
        compiler_params=_cparams(("arbitrary", "arbitrary")),
        name="mixer_tail",
    )(x, hy, ys, u, s5_d.reshape(1, sw), glu_w.astype(BF16), glu_b.reshape(1, sw),
      w_out[:hw].astype(BF16), w_out[hw:].astype(BF16), g1, norm2_g.reshape(1, d), sh2, sc2,
      router_w.T)


def _lane_cumsum_exclusive(x):
    rows, s = x.shape
    ii = lax.broadcasted_iota(I32, (LANES, LANES), 0)
    jj = lax.broadcasted_iota(I32, (LANES, LANES), 1)
    tri = jnp.where(ii < jj, 1.0, 0.0).astype(BF16)
    carry = jnp.zeros((rows, 1), F32)
    out, base = [], []
    for blk in range(s // LANES):
        xb = x[:, blk * LANES:(blk + 1) * LANES]
        out.append(jnp.dot(xb.astype(BF16), tri, preferred_element_type=F32) + carry)
        base.append(carry)
        carry = carry + jnp.sum(xb, axis=1, keepdims=True)
    return jnp.concatenate(out, axis=1), jnp.concatenate(base, axis=1)


def _route_kernel(lg_ref, pos_em_ref, pos_tm_ref, gate_tm_ref, base_ref, aff_ref, *, cap):
    lg = lg_ref[...]
    ne, s = lg.shape
    ex = jnp.exp(lg - jnp.max(lg, axis=0, keepdims=True))
    aff_ref[...] = ex / jnp.sum(ex, axis=0, keepdims=True)
    aff = aff_ref[...]
    count_ge = lambda v, t: jnp.sum(jnp.where(v >= t, 1.0, 0.0), axis=1, keepdims=True)

    def coarse(i, tb):
        cand = tb | jnp.left_shift(jnp.int32(1), 30 - i)
        return jnp.where(count_ge(aff, pltpu.bitcast(cand, F32)) >= cap, cand, tb)

    tb = lax.fori_loop(0, 31, coarse, jnp.zeros((ne, 1), I32))
    t_hi = pltpu.bitcast(tb, F32)
    ulp = pltpu.bitcast(tb + 1, F32) - t_hi
    resid = aff - t_hi

    def fine(j, carry):
        c, step = carry
        cand = c + step
        return jnp.where(count_ge(resid, cand) >= cap, cand, c), step * 0.5

    t_lo, _ = lax.fori_loop(0, 24, fine, (jnp.zeros((ne, 1), F32), ulp * 0.5))
    gt = resid > t_lo
    eq = resid == t_lo
    need = cap - jnp.sum(jnp.where(gt, 1.0, 0.0), axis=1, keepdims=True)
    eq_rank, _ = _lane_cumsum_exclusive(jnp.where(eq, 1.0, 0.0))
    sel = gt | (eq & (eq_rank < need))
    pos, base = _lane_cumsum_exclusive(jnp.where(sel, 1.0, 0.0))
    posf = jnp.where(sel, pos + 1.0, 0.0)
    gate = jnp.where(sel, aff, 0.0)
    pos_em_ref[...] = posf.astype(I32) - 1
    base_ref[...] = base.astype(I32)
    hi = jnp.floor(posf * (1.0 / 16.0))
    lo = posf - 16.0 * hi
    ii = lax.broadcasted_iota(I32, (LANES, LANES), 0)
    jj = lax.broadcasted_iota(I32, (LANES, LANES), 1)
    eye = jnp.where(ii == jj, 1.0, 0.0)
    nt = (((1,), (1,)), ((), ()))
    for blk in range(s // LANES):
        sl = slice(blk * LANES, (blk + 1) * LANES)
        t_hi = lax.dot_general(eye.astype(BF16), hi[:, sl].astype(BF16), nt, preferred_element_type=F32)
        t_lo = lax.dot_general(eye.astype(BF16), lo[:, sl].astype(BF16), nt, preferred_element_type=F32)
        pos_tm_ref[sl, :] = (16.0 * t_hi + t_lo).astype(I32) - 1
        gate_tm_ref[sl, :] = lax.dot_general(eye, gate[:, sl], nt, precision=HIGHEST,
                                             preferred_element_type=F32)


def route(logits, cap):
    b, ne, s = logits.shape
    nb = s // LANES
    return pl.pallas_call(
        functools.partial(_route_kernel, cap=cap),
        grid=(b,),
        in_specs=[pl.BlockSpec((None, ne, s), lambda i: (i, 0, 0))],
        out_specs=[pl.BlockSpec((None, ne, s), lambda i: (i, 0, 0)),
                   pl.BlockSpec((None, s, ne), lambda i: (i, 0, 0)),
                   pl.BlockSpec((None, s, ne), lambda i: (i, 0, 0)),
                   pl.BlockSpec((None, ne, nb), lambda i: (i, 0, 0))],
        out_shape=[jax.ShapeDtypeStruct((b, ne, s), I32), jax.ShapeDtypeStruct((b, s, ne), I32),
                   jax.ShapeDtypeStruct((b, s, ne), F32), jax.ShapeDtypeStruct((b, ne, nb), I32)],
        scratch_shapes=[pltpu.VMEM((ne, s), F32)],
        compiler_params=_cparams(("arbitrary",)),
        name="route",
    )(logits)


SLOT_ALIGN = 16


def _slot_windows(base, cap, chunk, win):
    lo = base[:, :, ::chunk // LANES]
    hi = jnp.concatenate([lo[:, :, 1:], jnp.full_like(lo[:, :, :1], cap)], axis=2)
    start = (lo // SLOT_ALIGN) * SLOT_ALIGN
    nwin = jnp.max((hi - start + win - 1) // win, axis=1)
    return jnp.transpose(start, (0, 2, 1)).reshape(-1), nwin.reshape(-1)


def _window(st_ref, idx, w, win, cap):
    first = st_ref[idx] + w * win
    return first, pl.multiple_of(jnp.minimum(first, cap - win), SLOT_ALIGN)


def _gather_kernel(st_ref, nw_ref, pos_ref, h_ref, o_ref, *, win):
    b, j, nch = pl.program_id(0), pl.program_id(1), pl.num_programs(1)
    ne, cap, _ = o_ref.shape
    tk = h_ref.shape[0]

    @pl.when(j == 0)
    def _():
        o_ref[...] = jnp.zeros_like(o_ref)

    pos = pos_ref[...]
    h = h_ref[...]
    row = lax.broadcasted_iota(I32, (win, tk), 0)

    def window(w, carry):
        starts, lhs = [], []
        for e in range(ne):
            first, start = _window(st_ref, (b * nch + j) * ne + e, w, win, cap)
            slot = row + start
            hit = (pos[e:e + 1, :] == slot) & (slot >= first)
            lhs.append(jnp.where(hit, 1.0, 0.0).astype(BF16))
            starts.append(start)
        res = jnp.dot(jnp.concatenate(lhs, axis=0), h, preferred_element_type=F32)
        for e, start in enumerate(starts):
            o_ref[e, pl.ds(start, win), :] += res[e * win:(e + 1) * win].astype(o_ref.dtype)
        return carry

    lax.fori_loop(0, nw_ref[b * nch + j], window, 0)


def moe_gather(pos_em, h2, base, cap, tk=256, win=64):
    b, ne, s = pos_em.shape
    d = h2.shape[2]
    starts, nwin = _slot_windows(base, cap, tk, win)
    return pl.pallas_call(
        functools.partial(_gather_kernel, win=win),
        grid_spec=pltpu.PrefetchScalarGridSpec(
            num_scalar_prefetch=2,
            grid=(b, s // tk),
            in_specs=[pl.BlockSpec((None, ne, tk), lambda i, j, st, nw: (i, 0, j)),
                      pl.BlockSpec((None, tk, d), lambda i, j, st, nw: (i, j, 0))],
            out_specs=pl.BlockSpec((ne, cap, d), lambda i, j, st, nw: (0, i, 0))),
        out_shape=jax.ShapeDtypeStruct((ne, b * cap, d), BF16),
        compiler_params=_cparams(("arbitrary", "arbitrary"), VMEM_LIMIT_BYTES),
        name="moe_gather",
    )(starts, nwin, pos_em, h2)


def _ffn_kernel(x_ref, wg_ref, wu_ref, wd_ref, o_ref, acc_ref, *, sub):
    f = pl.program_id(2)

    @pl.when(f == 0)
    def _():
        acc_ref[...] = jnp.zeros_like(acc_ref)

    wg = wg_ref[...].astype(BF16)
    wu = wu_ref[...].astype(BF16)
    wd = wd_ref[...].astype(BF16)
    for r in range(x_ref.shape[0] // sub):
        rows = pl.ds(r * sub, sub)
        x = x_ref[rows, :]
        g = jnp.dot(x, wg, preferred_element_type=F32)
        u = jnp.dot(x, wu, preferred_element_type=F32)
        h = (_silu(g) * u).astype(BF16)
        acc_ref[rows, :] += jnp.dot(h, wd, preferred_element_type=F32)

    @pl.when(f == pl.num_programs(2) - 1)
    def _():
        o_ref[...] = acc_ref[...].astype(o_ref.dtype)


def moe_ffn(xe, w_gate, w_up, w_down, tm=2048, tf=256, sub=512):
    ne, m, d = xe.shape
    ff = w_gate.shape[2]
    tm = min(tm, m)
    return pl.pallas_call(
        functools.partial(_ffn_kernel, sub=min(sub, tm)),
        grid=(ne, m // tm, ff // tf),
        in_specs=[pl.BlockSpec((None, tm, d), lambda e, i, f: (e, i, 0)),
                  pl.BlockSpec((None, d, tf), lambda e, i, f: (e, 0, f)),
                  pl.BlockSpec((None, d, tf), lambda e, i, f: (e, 0, f)),
                  pl.BlockSpec((None, tf, d), lambda e, i, f: (e, f, 0))],
        out_specs=pl.BlockSpec((None, tm, d), lambda e, i, f: (e, i, 0)),
        out_shape=jax.ShapeDtypeStruct((ne, m, d), BF16),
        scratch_shapes=[pltpu.VMEM((tm, d), F32)],
        compiler_params=_cparams(("arbitrary", "arbitrary", "arbitrary"), VMEM_LIMIT_BYTES),
        name="moe_ffn",
    )(xe, w_gate, w_up, w_down)


def _combine_kernel(st_ref, nw_ref, pos_ref, gate_ref, ye_ref, x1_ref, g2_ref, fg_ref, o_ref, acc_ref,
                    *, win):
    b, j, nt = pl.program_id(0), pl.program_id(1), pl.num_programs(1)
    tt, ne = pos_ref.shape
    cap = ye_ref.shape[1]
    acc_ref[...] = jnp.zeros_like(acc_ref)
    pos = pos_ref[...]
    gate = gate_ref[...]
    col = lax.broadcasted_iota(I32, (tt, win), 1)

    def window(w, carry):
        acc = None
        for p in range(ne // 2):
            lhs, rhs = [], []
            for e in (2 * p, 2 * p + 1):
                first, start = _window(st_ref, (b * nt + j) * ne + e, w, win, cap)
                slot = col + start
                hit = (pos[:, e:e + 1] == slot) & (slot >= first)
                lhs.append(jnp.where(hit, gate[:, e:e + 1], 0.0).astype(BF16))
                rhs.append(ye_ref[e, pl.ds(start, win), :])
            part = jnp.dot(jnp.concatenate(lhs, axis=1), jnp.concatenate(rhs, axis=0),
                           preferred_element_type=F32)
            acc = part if acc is None else acc + part
        acc_ref[...] += acc
        return carry

    lax.fori_loop(0, nw_ref[b * nt + j], window, 0)
    xo = x1_ref[...] + g2_ref[...] * acc_ref[...]
    o_ref[...] = _rmsnorm(xo, fg_ref[...])


def moe_combine(pos_tm, gate_tm, base, ye, x1, g2, final_g, cap, tt=512, win=128):
    b, s, ne = pos_tm.shape
    d = x1.shape[2]
    starts, nwin = _slot_windows(base, cap, tt, win)
    tok = lambda n: pl.BlockSpec((None, tt, n), lambda i, j, st, nw: (i, j, 0))
    return pl.pallas_call(
        functools.partial(_combine_kernel, win=win),
        grid_spec=pltpu.PrefetchScalarGridSpec(
            num_scalar_prefetch=2,
            grid=(b, s // tt),
            in_specs=[tok(ne), tok(ne),
                      pl.BlockSpec((ne, cap, d), lambda i, j, st, nw: (0, i, 0)),
                      tok(d),
                      pl.BlockSpec((None, 1, d), lambda i, j, st, nw: (i, 0, 0)),
                      pl.BlockSpec((1, d), lambda i, j, st, nw: (0, 0))],
            out_specs=tok(d),
            scratch_shapes=[pltpu.VMEM((tt, d), F32)]),
        out_shape=jax.ShapeDtypeStruct((b, s, d), F32),
        compiler_params=_cparams(("arbitrary", "arbitrary"), VMEM_LIMIT_BYTES),
        name="moe_combine",
    )(starts, nwin, pos_tm, gate_tm, ye, x1, g2, final_g.reshape(1, d))


def _layer(x, ctx, mods, norm1_g, norm2_g, w_in, w_out, conv_w, conv_b, filt, hy_bias, s5p,
           s5_c_re, s5_c_im, s5_d, s5_glu_w, s5_glu_b, router_w, ex_w_gate, ex_w_up, ex_w_down,
           final_g):
    b, s, d = x.shape
    n_order, hw = hy_bias.shape
    hy_cols = (n_order + 1) * hw
    sw = w_in.shape[1] - hy_cols
    rows = s // GRID_W
    ne = router_w.shape[1]
    cap = CAPACITY_FACTOR * s // ne

    per_b = lambda k: mods[:b, k * d:(k + 1) * d].reshape(b, 1, d)
    ctx_v = lambda k: jnp.broadcast_to(mods[b:b + 1, k * d:(k + 1) * d].reshape(1, 1, d), (b, 1, d))
    sh1, sc1, g1, sh2, sc2, g2 = [per_b(k) for k in range(N_MOD)]

    w_in_bf = w_in.astype(BF16)
    (u_ctx,) = inproj(ctx, norm1_g, ctx_v(0), ctx_v(1), w_in_bf[:, hy_cols:], (sw,))
    z_hy, u = inproj(x, norm1_g, sh1, sc1, w_in_bf, (hy_cols, sw), conv=(0, conv_w, conv_b, GRID_W))

    hraw = hyena_filter_mlp(s, *filt)
    kf = hyena_filter_spectra(hraw, n_order, hw)
    y1 = hyena_conv(z_hy, 0, z_hy, 1, kf[0], hy_bias[0])
    hy = hyena_conv(y1, 0, z_hy, 2, kf[1], hy_bias[1])

    s5w = s5_chunk_weights(*s5p, s5_c_re, s5_c_im, sw // LANES, S5_CHUNK)
    tmaj = lambda a: jnp.transpose(a, (1, 0, 2)).reshape(-1, sw)
    yf, yb = s5_scan(tmaj(u_ctx), tmaj(u), s5w, b)
    ys = jnp.transpose((yf + yb).reshape(s, b, sw), (1, 0, 2))

    x1, h2, logits = mixer_tail(x, hy, ys, u, s5_d, s5_glu_w, s5_glu_b, w_out, g1, norm2_g,
                                sh2, sc2, router_w)
    pos_em, pos_tm, gate_tm, base = route(logits, cap)
    xe = moe_gather(pos_em, h2, base, cap)
    ye = moe_ffn(xe, ex_w_gate, ex_w_up, ex_w_down)
    return moe_combine(pos_tm, gate_tm, base, ye, x1, g2, final_g, cap)


def kernel(x, c, ctx, c_ctx, mod_w, mod_b, norm1_g, norm2_g, w_in, w_out, conv_w, conv_b, hy_w1, hy_b1, hy_freq, hy_w2, hy_b2, hy_w3, hy_b3, hy_bias, s5_lam_re, s5_lam_im, s5_log_step, s5_b_re, s5_b_im, s5_c_re, s5_c_im, s5_d, s5_glu_w, s5_glu_b, router_w, ex_w_gate, ex_w_up, ex_w_down, final_g):
    depth = mod_w.shape[0]
    assert depth == 1, "context-token updates of non-final layers are not implemented"
    b, _, d = x.shape
    l = 0
    pad = (-(b + 1)) % SUBLANES
    cond = jnp.concatenate([c, c_ctx[None], jnp.zeros((pad, d), F32)], axis=0)
    mods = adaln_mods(cond, mod_w[l], mod_b[l])
    filt = (hy_w1[l], hy_b1[l], hy_freq[l], hy_w2[l], hy_b2[l], hy_w3[l], hy_b3[l])
    s5p = (s5_lam_re[l], s5_lam_im[l], s5_log_step[l], s5_b_re[l], s5_b_im[l])
    return _layer(x, ctx, mods, norm1_g[l], norm2_g[l], w_in[l], w_out[l], conv_w[l], conv_b[l],
                  filt, hy_bias[l], s5p, s5_c_re[l], s5_c_im[l], s5_d[l], s5_glu_w[l], s5_glu_b[l],
                  router_w[l], ex_w_gate[l], ex_w_up[l], ex_w_down[l], final_g)
```

```python
import functools
import math

import numpy as np
import jax
import jax.numpy as jnp
from jax import lax
from jax.experimental import pallas as pl
from jax.experimental.pallas import tpu as pltpu

F32 = jnp.float32
BF16 = jnp.bfloat16
I32 = jnp.int32
HIGHEST = lax.Precision.HIGHEST

SUBLANES = 8
LANES = 128
VMEM_LIMIT_BYTES = 58 * 1024 * 1024

GRID_W = 64
N_MOD = 6
NORM_EPS = 1e-6
POS_BANDS = 16
DECAY_FAST = 0.3
DECAY_SLOW = 1.5
DECAY_TARGET = 1e-2
CAPACITY_FACTOR = 2

DFT_N1 = 64
DFT_N2 = 128
DFT_K1 = DFT_N1 // 2 + 1
DFT_ROWS = 2 * DFT_K1

S5_CHUNK = 4


def _cparams(sem, vmem=None):
    return pltpu.CompilerParams(dimension_semantics=sem, vmem_limit_bytes=vmem)


def _silu(x):
    return x * (1.0 / (1.0 + jnp.exp(-x)))


def _rmsnorm(x, g):
    ms = jnp.mean(x * x, axis=-1, keepdims=True)
    return x * lax.rsqrt(ms + NORM_EPS) * g


def _adaln_kernel(c_ref, w_ref, b_ref, o_ref):
    s = _silu(c_ref[...])
    o_ref[...] = jnp.dot(s, w_ref[...], precision=HIGHEST, preferred_element_type=F32) + b_ref[...]


def adaln_mods(cond, mod_w, mod_b, tn=1536):
    rows, d = cond.shape
    n = mod_w.shape[1]
    return pl.pallas_call(
        _adaln_kernel,
        grid=(n // tn,),
        in_specs=[pl.BlockSpec((rows, d), lambda j: (0, 0)),
                  pl.BlockSpec((d, tn), lambda j: (0, j)),
                  pl.BlockSpec((1, tn), lambda j: (0, j))],
        out_specs=pl.BlockSpec((rows, tn), lambda j: (0, j)),
        out_shape=jax.ShapeDtypeStruct((rows, n), F32),
        compiler_params=_cparams(("arbitrary",)),
        name="adaln",
    )(cond, mod_w, mod_b.reshape(1, n))


def _short_conv(z, w_ref, b_ref, row_len):
    length = z.shape[0]
    pos = lax.broadcasted_iota(I32, (length, 1), 0) % row_len
    zm = jnp.where(pos == 0, 0.0, pltpu.roll(z, 1, 0))
    zp = jnp.where(pos == row_len - 1, 0.0, pltpu.roll(z, length - 1, 0))
    w = w_ref[...]
    return zm * w[0:1, :] + z * w[1:2, :] + zp * w[2:3, :] + b_ref[...]


def _inproj_kernel(x_ref, g_ref, sh_ref, sc_ref, w_ref, *refs, splits, conv_split, row_len):
    conv_refs, o_refs = refs[:len(refs) - len(splits)], refs[len(refs) - len(splits):]
    h = _rmsnorm(x_ref[...], g_ref[...])
    h = h * (1.0 + sc_ref[...]) + sh_ref[...]
    z = jnp.dot(h.astype(BF16), w_ref[...], preferred_element_type=F32)
    off = 0
    for idx, (o_ref, n) in enumerate(zip(o_refs, splits)):
        part = z[:, off:off + n]
        if idx == conv_split:
            part = _short_conv(part, *conv_refs, row_len)
        o_ref[...] = part
        off += n


def inproj(x, g, shift, scale, w_bf16, splits, conv=None, ts=512):
    b, s, d = x.shape
    n = w_bf16.shape[1]
    ts = min(ts, s)
    conv_split, conv_args, conv_specs, row_len = None, [], [], 1
    if conv is not None:
        conv_split, cw, cb, row_len = conv
        assert ts % row_len == 0
        conv_args = [cw, cb.reshape(1, -1)]
        conv_specs = [pl.BlockSpec(a.shape, lambda i, j: (0, 0)) for a in conv_args]
    out_specs = [pl.BlockSpec((None, ts, m), lambda i, j: (i, j, 0)) for m in splits]
    out_shape = [jax.ShapeDtypeStruct((b, s, m), F32) for m in splits]
    return pl.pallas_call(
        functools.partial(_inproj_kernel, splits=splits, conv_split=conv_split, row_len=row_len),
        grid=(b, s // ts),
        in_specs=[pl.BlockSpec((None, ts, d), lambda i, j: (i, j, 0)),
                  pl.BlockSpec((1, d), lambda i, j: (0, 0)),
                  pl.BlockSpec((None, 1, d), lambda i, j: (i, 0, 0)),
                  pl.BlockSpec((None, 1, d), lambda i, j: (i, 0, 0)),
                  pl.BlockSpec((d, n), lambda i, j: (0, 0))] + conv_specs,
        out_specs=out_specs,
        out_shape=out_shape,
        compiler_params=_cparams(("arbitrary", "arbitrary")),
        name="inproj",
    )(x, g.reshape(1, d), shift, scale, w_bf16, *conv_args)


def _filt_mlp_kernel(w1t_ref, w1c_ref, w1s_ref, b1_ref, fr_ref, w2_ref, b2_ref, w3_ref, b3_ref,
                     o_ref, *, length, tl):
    i0 = pl.program_id(0) * tl
    idx = (lax.broadcasted_iota(I32, (tl, 1), 0) + i0).astype(F32)
    t = idx / float(length - 1)
    omega = (2.0 * math.pi) * idx / float(length)
    fstep = ((POS_BANDS - 1) - 1e-4) / (POS_BANDS - 1)
    f = 1e-4 + lax.broadcasted_iota(I32, (1, POS_BANDS), 1).astype(F32) * fstep
    arg = omega * f
    pre = (t * w1t_ref[...]
           + jnp.dot(jnp.cos(arg), w1c_ref[...], precision=HIGHEST, preferred_element_type=F32)
           - jnp.dot(jnp.sin(arg), w1s_ref[...], precision=HIGHEST, preferred_element_type=F32)
           + b1_ref[...])
    fr = fr_ref[...]
    h = jnp.sin(fr[0:1, :] * pre)
    h = jnp.sin(fr[1:2, :] * (jnp.dot(h, w2_ref[...], precision=HIGHEST,
                                      preferred_element_type=F32) + b2_ref[...]))
    o_ref[...] = jnp.dot(h, w3_ref[...], precision=HIGHEST, preferred_element_type=F32) + b3_ref[...]


def hyena_filter_mlp(length, w1, b1, freq, w2, b2, w3, b3, tl=512):
    fw = w1.shape[1]
    n = w3.shape[1]
    full = lambda shape: pl.BlockSpec(shape, lambda i: (0, 0))
    return pl.pallas_call(
        functools.partial(_filt_mlp_kernel, length=length, tl=tl),
        grid=(length // tl,),
        in_specs=[full((1, fw)), full((POS_BANDS, fw)), full((POS_BANDS, fw)), full((1, fw)),
                  full((2, fw)), full((fw, fw)), full((1, fw)), full((fw, n)), full((1, n))],
        out_specs=pl.BlockSpec((tl, n), lambda i: (i, 0)),
        out_shape=jax.ShapeDtypeStruct((length, n), F32),
        compiler_params=_cparams(("arbitrary",)),
        name="hyena_filter_mlp",
    )(w1[0:1], w1[1:1 + POS_BANDS], w1[1 + POS_BANDS:], b1.reshape(1, fw), freq, w2,
      b2.reshape(1, fw), w3, b3.reshape(1, n))


@functools.lru_cache(maxsize=None)
def _dft_tables():
    n1n, n2n, k1n = DFT_N1, DFT_N2, DFT_K1
    n = n1n * n2n
    half = n1n // 2
    k1 = np.arange(k1n)[:, None]
    n1 = np.arange(half)[None, :]
    th = 2.0 * np.pi * k1 * n1 / n1n
    f1 = np.zeros((DFT_ROWS, half))
    f1[0::2] = np.cos(th)
    f1[1::2] = -np.sin(th)
    wgt = np.where((k1 == 0) | (k1 == half), 1.0, 2.0)
    g1 = np.zeros((half, DFT_ROWS))
    g1[:, 0::2] = (wgt * np.cos(th)).T / n
    g1[:, 1::2] = (-wgt * np.sin(th)).T / n
    eye = np.eye(SUBLANES)
    fk = np.kron(f1, eye)
    gk = np.kron(g1, eye)
    k2 = np.arange(n2n)[:, None]
    n2 = np.arange(n2n)[None, :]
    f3 = np.zeros((k1n, 2 * n2n, 2 * n2n))
    for kk in range(k1n):
        ph = 2.0 * np.pi * n2 * (n1n * k2 + kk) / n
        tr, ti = np.cos(ph), -np.sin(ph)
        f3[kk, :n2n, :n2n] = tr
        f3[kk, :n2n, n2n:] = -ti
        f3[kk, n2n:, :n2n] = ti
        f3[kk, n2n:, n2n:] = tr
    g3 = np.transpose(f3, (0, 2, 1))
    to = lambda a: jnp.asarray(a, dtype=F32).astype(BF16)
    return to(fk), to(gk), to(f3), to(g3)


def _dft_stage1(src_ref, a_ref, fk_ref):
    half = DFT_N1 // 2

    def body(m, carry):
        sub = pl.ds(pl.multiple_of(m * SUBLANES, SUBLANES), SUBLANES)
        rows = [src_ref.at[pl.ds(DFT_N2 * n1, DFT_N2)][sub, :] for n1 in range(half)]
        rhs = jnp.concatenate(rows, axis=0).astype(BF16)
        out = jnp.dot(fk_ref[...], rhs, preferred_element_type=F32)
        for j in range(DFT_ROWS):
            a_ref.at[pl.ds(DFT_N2 * j, DFT_N2)][sub, :] = out[SUBLANES * j:SUBLANES * (j + 1)]
        return carry

    lax.fori_loop(0, DFT_N2 // SUBLANES, body, 0, unroll=2)


def _dft_stage3(a_ref, f3_ref, k1):
    r0 = pl.multiple_of(k1 * (2 * DFT_N2), 2 * DFT_N2)
    a = a_ref[pl.ds(r0, 2 * DFT_N2), :].astype(BF16)
    x = jnp.dot(f3_ref[k1], a, preferred_element_type=F32)
    return x[:DFT_N2], x[DFT_N2:]


def _filt_spec_kernel(hf_ref, hb_ref, fk_ref, f3_ref, o_ref, src_ref, af_ref, ab_ref, *, length):
    c = hf_ref.shape[1]
    cb = pl.program_id(1)
    hw = pl.num_programs(1) * c
    row = lax.broadcasted_iota(I32, (length, 1), 0)
    t = row.astype(F32) / float(length - 1)
    ch = (lax.broadcasted_iota(I32, (1, c), 1) + cb * c).astype(F32)
    d0 = math.log(DECAY_TARGET) / DECAY_FAST
    d1 = math.log(DECAY_TARGET) / DECAY_SLOW
    deltas = jnp.abs(d0 + ch * ((d1 - d0) / float(hw - 1)))
    decay = jnp.exp(-t * deltas)
    fwd = hf_ref[...] * decay
    bwd = jnp.where(row == 0, 0.0, hb_ref[...] * decay)
    inv = 1.0 / (jnp.sum(jnp.abs(fwd), axis=0, keepdims=True)
                 + jnp.sum(jnp.abs(bwd), axis=0, keepdims=True))
    src_ref[...] = fwd
    _dft_stage1(src_ref, af_ref, fk_ref)
    src_ref[...] = bwd
    _dft_stage1(src_ref, ab_ref, fk_ref)

    def body(k1, carry):
        fr, fi = _dft_stage3(af_ref, f3_ref, k1)
        br, bi = _dft_stage3(ab_ref, f3_ref, k1)
        o_ref[k1, 0] = ((fr + br) * inv).astype(o_ref.dtype)
        o_ref[k1, 1] = ((fi - bi) * inv).astype(o_ref.dtype)
        return carry

    lax.fori_loop(0, DFT_K1, body, 0)


def hyena_filter_spectra(hraw, n_order, width, c_blk=256):
    length = hraw.shape[0]
    assert 2 * length == DFT_N1 * DFT_N2
    fk, _, f3, _ = _dft_tables()
    ncb = width // c_blk
    return pl.pallas_call(
        functools.partial(_filt_spec_kernel, length=length),
        grid=(n_order, ncb),
        in_specs=[pl.BlockSpec((length, c_blk), lambda o, j: (0, o * 2 * ncb + j)),
                  pl.BlockSpec((length, c_blk), lambda o, j: (0, o * 2 * ncb + ncb + j)),
                  pl.BlockSpec(fk.shape, lambda o, j: (0, 0)),
                  pl.BlockSpec(f3.shape, lambda o, j: (0, 0, 0))],
        out_specs=pl.BlockSpec((None, DFT_K1, 2, DFT_N2, c_blk), lambda o, j: (o, 0, 0, 0, j)),
        out_shape=jax.ShapeDtypeStruct((n_order, DFT_K1, 2, DFT_N2, width), BF16),
        scratch_shapes=[pltpu.VMEM((length, c_blk), F32),
                        pltpu.VMEM((DFT_ROWS * DFT_N2, c_blk), F32),
                        pltpu.VMEM((DFT_ROWS * DFT_N2, c_blk), F32)],
        compiler_params=_cparams(("arbitrary", "arbitrary"), VMEM_LIMIT_BYTES),
        name="hyena_filter_spectrum",
    )(hraw, hraw, fk, f3)


def _hyena_conv_kernel(s_ref, m_ref, kf_ref, bias_ref, fk_ref, f3_ref, g3_ref, gk_ref, o_ref, a_ref,
                       *, group):
    half = DFT_N1 // 2
    blk_rows = 2 * DFT_N2
    _dft_stage1(s_ref, a_ref, fk_ref)

    def body3(i, carry):
        k1s = [i * group + q for q in range(group)]
        r0s = [pl.multiple_of(k1 * blk_rows, blk_rows) for k1 in k1s]
        blocks = [a_ref[pl.ds(r0, blk_rows), :].astype(BF16) for r0 in r0s]
        outs = []
        for k1, a in zip(k1s, blocks):
            x = jnp.dot(f3_ref[k1], a, preferred_element_type=F32)
            xr, xi = x[:DFT_N2], x[DFT_N2:]
            kr = kf_ref[k1, 0].astype(F32)
            ki = kf_ref[k1, 1].astype(F32)
            y = jnp.concatenate([xr * kr - xi * ki, xr * ki + xi * kr], axis=0).astype(BF16)
            outs.append(jnp.dot(g3_ref[k1], y, preferred_element_type=F32))
        for r0, o in zip(r0s, outs):
            a_ref[pl.ds(r0, blk_rows), :] = o
        return carry

    lax.fori_loop(0, DFT_K1 // group, body3, 0)
    bias = bias_ref[...]

    def body1(m, carry):
        sub = pl.ds(pl.multiple_of(m * SUBLANES, SUBLANES), SUBLANES)
        blk = [a_ref.at[pl.ds(DFT_N2 * j, DFT_N2)][sub, :] for j in range(DFT_ROWS)]
        rhs = jnp.concatenate(blk, axis=0).astype(BF16)
        out = jnp.dot(gk_ref[...], rhs, preferred_element_type=F32)
        for n1 in range(half):
            blk_n1 = pl.ds(DFT_N2 * n1, DFT_N2)
            conv = out[SUBLANES * n1:SUBLANES * (n1 + 1)]
            o_ref.at[blk_n1][sub, :] = m_ref.at[blk_n1][sub, :] * (
                conv + s_ref.at[blk_n1][sub, :] * bias)
        return carry

    lax.fori_loop(0, DFT_N2 // SUBLANES, body1, 0, unroll=2)


def hyena_conv(sig, sig_col, mul, mul_col, kf, bias, c_blk=256, group=11):
    b, length, _ = sig.shape
    width = kf.shape[-1]
    ncb = width // c_blk
    assert DFT_K1 % group == 0
    fk, gk, f3, g3 = _dft_tables()
    once = pl.Buffered(1)
    const2 = lambda a: pl.BlockSpec(a.shape, lambda j, i: (0, 0), pipeline_mode=once)
    const3 = lambda a: pl.BlockSpec(a.shape, lambda j, i: (0, 0, 0), pipeline_mode=once)
    return pl.pallas_call(
        functools.partial(_hyena_conv_kernel, group=group),
        grid=(ncb, b),
        in_specs=[pl.BlockSpec((None, length, c_blk), lambda j, i: (i, 0, sig_col * ncb + j)),
                  pl.BlockSpec((None, length, c_blk), lambda j, i: (i, 0, mul_col * ncb + j)),
                  pl.BlockSpec((DFT_K1, 2, DFT_N2, c_blk), lambda j, i: (0, 0, 0, j),
                               pipeline_mode=once),
                  pl.BlockSpec((1, c_blk), lambda j, i: (0, j), pipeline_mode=once),
                  const2(fk), const3(f3), const3(g3), const2(gk)],
        out_specs=pl.BlockSpec((None, length, c_blk), lambda j, i: (i, 0, j)),
        out_shape=jax.ShapeDtypeStruct((b, length, width), F32),
        scratch_shapes=[pltpu.VMEM((DFT_ROWS * DFT_N2, c_blk), F32)],
        compiler_params=_cparams(("arbitrary", "arbitrary"), VMEM_LIMIT_BYTES),
        name="hyena_conv",
    )(sig, mul, kf, bias.reshape(1, width), fk, f3, g3, gk)


def _s5_powers_kernel(lr_ref, li_ref, dt_ref, vr_ref, vi_ref, or_ref, oi_ref, *, n_pow, zoh):
    lr, li, dt = lr_ref[...], li_ref[...], jnp.exp(dt_ref[...])
    mag = jnp.exp(lr * dt)
    ar = mag * jnp.cos(li * dt)
    ai = mag * jnp.sin(li * dt)
    vr, vi = vr_ref[...], vi_ref[...]
    if zoh:
        den = 1.0 / (lr * lr + li * li)
        qr = ((ar - 1.0) * lr + ai * li) * den
        qi = (ai * lr - (ar - 1.0) * li) * den
        vr, vi = qr * vr - qi * vi, qr * vi + qi * vr
    for j in range(n_pow):
        or_ref[j] = vr
        oi_ref[j] = vi
        vr, vi = ar * vr - ai * vi, ar * vi + ai * vr


def s5_powers(lam_re, lam_im, log_step, v_re, v_im, state_axis, n_pow, zoh):
    nd, g, a, b = v_re.shape
    expand = (lambda x: x[..., :, None]) if state_axis == 2 else (lambda x: x[..., None, :])
    rep = lambda x: jnp.broadcast_to(expand(x), v_re.shape).reshape(nd * g, a * b)
    dt = jnp.broadcast_to(log_step[:, :, None, None], v_re.shape).reshape(nd * g, a * b)
    flat = lambda x: x.reshape(nd * g, a * b)
    shp = jax.ShapeDtypeStruct((n_pow, nd * g, a * b), F32)
    o_r, o_i = pl.pallas_call(
        functools.partial(_s5_powers_kernel, n_pow=n_pow, zoh=zoh), out_shape=[shp, shp],
        name="s5_powers",
    )(rep(lam_re), rep(lam_im), dt, flat(v_re), flat(v_im))
    un = lambda x: x.reshape(n_pow, nd, g, a, b)
    return un(o_r), un(o_i)


def _s5_taps_kernel(cr_ref, ci_ref, br_ref, bi_ref, o_ref):
    n_pow, nk = br_ref.shape[0], br_ref.shape[1]
    for j in range(n_pow):
        for k in range(nk):
            o_ref[j, k] = (jnp.dot(cr_ref[k], br_ref[j, k], preferred_element_type=F32)
                           - jnp.dot(ci_ref[k], bi_ref[j, k], preferred_element_type=F32))


def s5_taps(c_re, c_im, bbp_r, bbp_i, nk):
    nd, g, h, p = c_re.shape
    n_pow = bbp_r.shape[0]
    gpk = g // nk
    eye = jnp.eye(gpk, dtype=F32)
    bdiag = lambda c: jnp.einsum('dkahp,ab->dkahbp', c.reshape(nd, nk, gpk, h, p), eye).reshape(
        nd, nk, gpk * h, gpk * p).astype(BF16)
    flat = lambda x: jnp.transpose(x, (1, 0, 2, 3, 4)).reshape(nd, n_pow, nk, gpk * p, h).astype(BF16)
    per_d = lambda a: pl.BlockSpec((None,) + a.shape[1:], lambda d: (d,) + (0,) * (a.ndim - 1))
    args = (bdiag(c_re), bdiag(c_im), flat(bbp_r), flat(bbp_i))
    shp = jax.ShapeDtypeStruct((nd, n_pow, nk, gpk * h, h), F32)
    out = pl.pallas_call(
        _s5_taps_kernel,
        grid=(nd,),
        in_specs=[per_d(a) for a in args],
        out_specs=per_d(shp),
        out_shape=shp,
        compiler_params=_cparams(("arbitrary",)),
        name="s5_taps",
    )(*args)
    return out.reshape(nd, n_pow, g, h, h)


def _s5_scan_kernel(ucf_ref, uf_ref, ucb_ref, ub_ref, qf_ref, qb_ref, pf_ref, pb_ref, mf_ref, mb_ref,
                    lam_ref, yf_ref, yb_ref, xf_ref, xb_ref, st_ref, *, nc, kpp):
    cpg, tlen, batch, width = uf_ref.shape
    nk = xf_ref.shape[1]
    ck = width // nk
    rows = cpg * batch
    i = pl.program_id(0)
    is_ctx = i < nc

    @pl.when(i == 0)
    def _():
        st_ref[...] = jnp.zeros_like(st_ref)

    uf = jnp.where(is_ctx, ucf_ref[...], uf_ref[...])
    ub = jnp.where(is_ctx, ucb_ref[...], ub_ref[...])

    def scan(x_ref, d, reverse):
        for k0 in range(0, nk, kpp):
            ks = slice(k0, k0 + kpp)
            lr, li = lam_ref[d, 0, ks], lam_ref[d, 1, ks]

            def body(c, carry, ks=ks, lr=lr, li=li):
                sr, si = carry
                cc = (cpg - 1 - c) if reverse else c
                r = pl.ds(pl.multiple_of(cc * batch, batch), batch)
                qr, qi = x_ref[0, ks, r, :], x_ref[1, ks, r, :]
                x_ref[0, ks, r, :] = sr
                x_ref[1, ks, r, :] = si
                return lr * sr - li * si + qr, lr * si + li * sr + qi

            st_ref[d, 0, ks], st_ref[d, 1, ks] = lax.fori_loop(
                0, cpg, body, (st_ref[d, 0, ks], st_ref[d, 1, ks]))

    for u, q_ref, p_ref, m_ref, x_ref, y_ref, d in ((uf, qf_ref, pf_ref, mf_ref, xf_ref, yf_ref, 0),
                                                    (ub, qb_ref, pb_ref, mb_ref, xb_ref, yb_ref, 1)):
        uks = []
        for k in range(nk):
            uk = jnp.concatenate([u[:, t, :, k * ck:(k + 1) * ck].reshape(rows, ck)
                                  for t in range(tlen)], axis=1).astype(BF16)
            uks.append(uk)
            for ri in range(2):
                x_ref[ri, k] = jnp.dot(uk, q_ref[ri, k], preferred_element_type=F32)
        scan(x_ref, d, d == 1)
        for k in range(nk):
            yk = (jnp.dot(uks[k], m_ref[k], preferred_element_type=F32)
                  + jnp.dot(x_ref[0, k].astype(BF16), p_ref[0, k], preferred_element_type=F32)
                  + jnp.dot(x_ref[1, k].astype(BF16), p_ref[1, k], preferred_element_type=F32))
            for t in range(tlen):
                y_ref[:, t, :, k * ck:(k + 1) * ck] = yk[:, t * ck:(t + 1) * ck].reshape(
                    cpg, batch, ck)


def s5_chunk_weights(lam_re, lam_im, log_step, b_re, b_im, c_re, c_im, nk, tlen):
    nd, g, p, h = b_re.shape
    gpk, sk = g // nk, g * p // nk
    bb_r, bb_i = s5_powers(lam_re, lam_im, log_step, b_re, b_im, 2, tlen, True)
    cl_r, cl_i = s5_powers(lam_re, lam_im, log_step, c_re, c_im, 3, tlen + 1, False)
    ones = jnp.ones((nd, g, p, 2), F32)
    pw_r, pw_i = s5_powers(lam_re, lam_im, log_step, ones, jnp.zeros_like(ones), 2, tlen + 1, False)
    taps = s5_taps(c_re, c_im, bb_r, bb_i, nk)
    dirs = np.arange(nd)[:, None]
    step = np.arange(tlen)[None, :]
    jq = np.where(dirs == 0, tlen - 1 - step, step)
    jp = np.where(dirs == 0, step + 1, tlen - step)
    lag = step[0][None, None, :] - step[0][None, :, None]
    lag = np.where(dirs[:, :, None] == 0, lag, -lag)
    eye = jnp.eye(gpk, dtype=F32)
    qsel = jnp.stack([bb_r, bb_i])[:, jq, dirs].reshape(2, nd, tlen, nk, gpk, p, h)
    q = jnp.einsum('rdskaph,ab->drksahbp', qsel, eye).reshape(nd, 2, nk, tlen * gpk * h, sk)
    psel = jnp.stack([cl_r, -cl_i])[:, jp, dirs].reshape(2, nd, tlen, nk, gpk, h, p)
    pm = jnp.einsum('rdtkahp,ab->drkaptbh', psel, eye).reshape(nd, 2, nk, sk, tlen * gpk * h)
    tsel = jnp.where((lag >= 0)[..., None, None, None], taps[dirs[:, :, None], np.maximum(lag, 0)], 0.0)
    tsel = tsel.reshape(nd, tlen, tlen, nk, gpk, h, h)
    m = jnp.einsum('dstkahq,ab->dksaqtbh', tsel, eye).reshape(nd, nk, tlen * gpk * h, tlen * gpk * h)
    decay = jnp.stack([pw_r[tlen, ..., 0], pw_i[tlen, ..., 0]], axis=1).reshape(nd, 2, nk, 1, sk)
    return q.astype(BF16), pm.astype(BF16), m.astype(BF16), decay


def s5_scan(uc, u, weights, batch, steps=64, kpp=2):
    q, pm, m, decay = weights
    width = u.shape[1]
    nk, sk = q.shape[2], q.shape[4]
    tlen = m.shape[2] * nk // width
    cpg = steps // tlen
    rpc = steps * batch
    nc, nl = uc.shape[0] // rpc, u.shape[0] // rpc
    assert steps % tlen == 0 and uc.shape[0] % rpc == 0 and u.shape[0] % rpc == 0
    lam = jnp.broadcast_to(decay, (2, 2, nk, batch, sk))
    chunked = lambda a: a.reshape(-1, tlen, batch, width)
    blk = lambda f: pl.BlockSpec((cpg, tlen, batch, width), lambda i: (f(i), 0, 0, 0))
    lat_f = lambda i: jnp.maximum(i - nc, 0)
    lat_b = lambda i: jnp.clip(nl - 1 - i + nc, 0, nl - 1)
    once = pl.Buffered(1)
    par = lambda a, d: pl.BlockSpec((None,) + a.shape[1:], lambda i: (d,) + (0,) * (a.ndim - 1),
                                    pipeline_mode=once)
    shp = jax.ShapeDtypeStruct((u.shape[0] // (tlen * batch), tlen, batch, width), F32)
    yf, yb = pl.pallas_call(
        functools.partial(_s5_scan_kernel, nc=nc, kpp=kpp),
        grid=(nc + nl,),
        in_specs=[blk(lambda i: jnp.minimum(i, nc - 1)), blk(lat_f),
                  blk(lambda i: jnp.maximum(nc - 1 - i, 0)), blk(lat_b),
                  par(q, 0), par(q, 1), par(pm, 0), par(pm, 1), par(m, 0), par(m, 1),
                  pl.BlockSpec(lam.shape, lambda i: (0, 0, 0, 0, 0), pipeline_mode=once)],
        out_specs=[blk(lat_f), blk(lat_b)],
        out_shape=[shp, shp],
        scratch_shapes=[pltpu.VMEM((2, nk, cpg * batch, sk), F32),
                        pltpu.VMEM((2, nk, cpg * batch, sk), F32),
                        pltpu.VMEM((2, 2, nk, batch, sk), F32)],
        compiler_params=_cparams(("arbitrary",), VMEM_LIMIT_BYTES),
        name="s5_scan",
    )(chunked(uc), chunked(u), chunked(uc), chunked(u), q, q, pm, pm, m, m, lam)
    return yf.reshape(u.shape), yb.reshape(u.shape)


def _mixer_tail_kernel(x_ref, hy_ref, ys_ref, u_ref, d_ref, gw_ref, gb_ref, woh_ref, wos_ref,
                       g1_ref, n2_ref, sh2_ref, sc2_ref, rwt_ref, x1_ref, h2_ref, lg_ref):
    y = ys_ref[...] + d_ref[...] * u_ref[...]
    y = 0.5 * y * (1.0 + jnp.tanh(math.sqrt(2.0 / math.pi) * (y + 0.044715 * (y * y * y))))
    gate = jnp.dot(y.astype(BF16), gw_ref[...], preferred_element_type=F32) + gb_ref[...]
    s5 = y * (1.0 / (1.0 + jnp.exp(-gate)))
    mix = (jnp.dot(hy_ref[...].astype(BF16), woh_ref[...], preferred_element_type=F32)
           + jnp.dot(s5.astype(BF16), wos_ref[...], preferred_element_type=F32))
    x1 = x_ref[...] + g1_ref[...] * mix
    x1_ref[...] = x1
    h2 = _rmsnorm(x1, n2_ref[...]) * (1.0 + sc2_ref[...]) + sh2_ref[...]
    h2_ref[...] = h2.astype(BF16)
    lg_ref[...] = lax.dot_general(rwt_ref[...], h2, (((1,), (1,)), ((), ())), precision=HIGHEST,
                                  preferred_element_type=F32)


def mixer_tail(x, hy, ys, u, s5_d, glu_w, glu_b, w_out, g1, norm2_g, sh2, sc2, router_w, ts=512):
    b, s, d = x.shape
    hw = hy.shape[2]
    sw = u.shape[2]
    ne = router_w.shape[1]
    tok = lambda n: pl.BlockSpec((None, ts, n), lambda i, j: (i, j, 0))
    vec = lambda n: pl.BlockSpec((1, n), lambda i, j: (0, 0))
    per_b = pl.BlockSpec((None, 1, d), lambda i, j: (i, 0, 0))
    mat = lambda r, c: pl.BlockSpec((r, c), lambda i, j: (0, 0))
    return pl.pallas_call(
        _mixer_tail_kernel,
        grid=(b, s // ts),
        in_specs=[tok(d), tok(hw), tok(sw), tok(sw), vec(sw), mat(sw, sw), vec(sw),
                  mat(hw, d), mat(sw, d), per_b, vec(d), per_b, per_b, mat(ne, d)],
        out_specs=[tok(d), tok(d), pl.BlockSpec((None, ne, ts), lambda i, j: (i, 0, j))],
        out_shape=[jax.ShapeDtypeStruct((b, s, d), F32), jax.ShapeDtypeStruct((b, s, d), BF16),
                   jax.ShapeDtypeStruct((b, ne, s), F32)],
        compiler_params=_cparams(("arbitrary", "arbitrary")),
        name="mixer_tail",
    )(x, hy, ys, u, s5_d.reshape(1, sw), glu_w.astype(BF16), glu_b.reshape(1, sw),
      w_out[:hw].astype(BF16), w_out[hw:].astype(BF16), g1, norm2_g.reshape(1, d), sh2, sc2,
      router_w.T)


def _lane_cumsum_exclusive(x):
    rows, s = x.shape
    ii = lax.broadcasted_iota(I32, (LANES, LANES), 0)
    jj = lax.broadcasted_iota(I32, (LANES, LANES), 1)
    tri = jnp.where(ii < jj, 1.0, 0.0).astype(BF16)
    carry = jnp.zeros((rows, 1), F32)
    out, base = [], []
    for blk in range(s // LANES):
        xb = x[:, blk * LANES:(blk + 1) * LANES]
        out.append(jnp.dot(xb.astype(BF16), tri, preferred_element_type=F32) + carry)
        base.append(carry)
        carry = carry + jnp.sum(xb, axis=1, keepdims=True)
    return jnp.concatenate(out, axis=1), jnp.concatenate(base, axis=1)


def _route_kernel(lg_ref, pos_em_ref, pos_tm_ref, gate_tm_ref, base_ref, aff_ref, *, cap):
    lg = lg_ref[...]
    ne, s = lg.shape
    ex = jnp.exp(lg - jnp.max(lg, axis=0, keepdims=True))
    aff_ref[...] = ex / jnp.sum(ex, axis=0, keepdims=True)
    aff = aff_ref[...]
    count_ge = lambda v, t: jnp.sum(jnp.where(v >= t, 1.0, 0.0), axis=1, keepdims=True)

    def coarse(i, tb):
        cand = tb | jnp.left_shift(jnp.int32(1), 30 - i)
        return jnp.where(count_ge(aff, pltpu.bitcast(cand, F32)) >= cap, cand, tb)

    tb = lax.fori_loop(0, 31, coarse, jnp.zeros((ne, 1), I32))
    t_hi = pltpu.bitcast(tb, F32)
    ulp = pltpu.bitcast(tb + 1, F32) - t_hi
    resid = aff - t_hi

    def fine(j, carry):
        c, step = carry
        cand = c + step
        return jnp.where(count_ge(resid, cand) >= cap, cand, c), step * 0.5

    t_lo, _ = lax.fori_loop(0, 24, fine, (jnp.zeros((ne, 1), F32), ulp * 0.5))
    gt = resid > t_lo
    eq = resid == t_lo
    need = cap - jnp.sum(jnp.where(gt, 1.0, 0.0), axis=1, keepdims=True)
    eq_rank, _ = _lane_cumsum_exclusive(jnp.where(eq, 1.0, 0.0))
    sel = gt | (eq & (eq_rank < need))
    pos, base = _lane_cumsum_exclusive(jnp.where(sel, 1.0, 0.0))
    posf = jnp.where(sel, pos + 1.0, 0.0)
    gate = jnp.where(sel, aff, 0.0)
    pos_em_ref[...] = posf.astype(I32) - 1
    base_ref[...] = base.astype(I32)
    hi = jnp.floor(posf * (1.0 / 16.0))
    lo = posf - 16.0 * hi
    ii = lax.broadcasted_iota(I32, (LANES, LANES), 0)
    jj = lax.broadcasted_iota(I32, (LANES, LANES), 1)
    eye = jnp.where(ii == jj, 1.0, 0.0)
    nt = (((1,), (1,)), ((), ()))
    for blk in range(s // LANES):
        sl = slice(blk * LANES, (blk + 1) * LANES)
        t_hi = lax.dot_general(eye.astype(BF16), hi[:, sl].astype(BF16), nt, preferred_element_type=F32)
        t_lo = lax.dot_general(eye.astype(BF16), lo[:, sl].astype(BF16), nt, preferred_element_type=F32)
        pos_tm_ref[sl, :] = (16.0 * t_hi + t_lo).astype(I32) - 1
        gate_tm_ref[sl, :] = lax.dot_general(eye, gate[:, sl], nt, precision=HIGHEST,
                                             preferred_element_type=F32)


def route(logits, cap):
    b, ne, s = logits.shape
    nb = s // LANES
    return pl.pallas_call(
        functools.partial(_route_kernel, cap=cap),
        grid=(b,),
        in_specs=[pl.BlockSpec((None, ne, s), lambda i: (i, 0, 0))],
        out_specs=[pl.BlockSpec((None, ne, s), lambda i: (i, 0, 0)),
                   pl.BlockSpec((None, s, ne), lambda i: (i, 0, 0)),
                   pl.BlockSpec((None, s, ne), lambda i: (i, 0, 0)),
                   pl.BlockSpec((None, ne, nb), lambda i: (i, 0, 0))],
        out_shape=[jax.ShapeDtypeStruct((b, ne, s), I32), jax.ShapeDtypeStruct((b, s, ne), I32),
                   jax.ShapeDtypeStruct((b, s, ne), F32), jax.ShapeDtypeStruct((b, ne, nb), I32)],
        scratch_shapes=[pltpu.VMEM((ne, s), F32)],
        compiler_params=_cparams(("arbitrary",)),
        name="route",
    )(logits)


SLOT_ALIGN = 16


def _slot_windows(base, cap, chunk, win):
    lo = base[:, :, ::chunk // LANES]
    hi = jnp.concatenate([lo[:, :, 1:], jnp.full_like(lo[:, :, :1], cap)], axis=2)
    start = (lo // SLOT_ALIGN) * SLOT_ALIGN
    nwin = jnp.max((hi - start + win - 1) // win, axis=1)
    return jnp.transpose(start, (0, 2, 1)).reshape(-1), nwin.reshape(-1)


def _window(st_ref, idx, w, win, cap):
    first = st_ref[idx] + w * win
    return first, pl.multiple_of(jnp.minimum(first, cap - win), SLOT_ALIGN)


def _gather_kernel(st_ref, nw_ref, pos_ref, h_ref, o_ref, *, win):
    b, j, nch = pl.program_id(0), pl.program_id(1), pl.num_programs(1)
    ne, cap, _ = o_ref.shape
    tk = h_ref.shape[0]

    @pl.when(j == 0)
    def _():
        o_ref[...] = jnp.zeros_like(o_ref)

    pos = pos_ref[...]
    h = h_ref[...]
    row = lax.broadcasted_iota(I32, (win, tk), 0)

    def window(w, carry):
        starts, lhs = [], []
        for e in range(ne):
            first, start = _window(st_ref, (b * nch + j) * ne + e, w, win, cap)
            slot = row + start
            hit = (pos[e:e + 1, :] == slot) & (slot >= first)
            lhs.append(jnp.where(hit, 1.0, 0.0).astype(BF16))
            starts.append(start)
        res = jnp.dot(jnp.concatenate(lhs, axis=0), h, preferred_element_type=F32)
        for e, start in enumerate(starts):
            o_ref[e, pl.ds(start, win), :] += res[e * win:(e + 1) * win].astype(o_ref.dtype)
        return carry

    lax.fori_loop(0, nw_ref[b * nch + j], window, 0)


def moe_gather(pos_em, h2, base, cap, tk=256, win=64):
    b, ne, s = pos_em.shape
    d = h2.shape[2]
    starts, nwin = _slot_windows(base, cap, tk, win)
    return pl.pallas_call(
        functools.partial(_gather_kernel, win=win),
        grid_spec=pltpu.PrefetchScalarGridSpec(
            num_scalar_prefetch=2,
            grid=(b, s // tk),
            in_specs=[pl.BlockSpec((None, ne, tk), lambda i, j, st, nw: (i, 0, j)),
                      pl.BlockSpec((None, tk, d), lambda i, j, st, nw: (i, j, 0))],
            out_specs=pl.BlockSpec((ne, cap, d), lambda i, j, st, nw: (0, i, 0))),
        out_shape=jax.ShapeDtypeStruct((ne, b * cap, d), BF16),
        compiler_params=_cparams(("arbitrary", "arbitrary"), VMEM_LIMIT_BYTES),
        name="moe_gather",
    )(starts, nwin, pos_em, h2)


def _ffn_kernel(x_ref, wg_ref, wu_ref, wd_ref, o_ref, acc_ref, *, sub):
    f = pl.program_id(2)

    @pl.when(f == 0)
    def _():
        acc_ref[...] = jnp.zeros_like(acc_ref)

    wg = wg_ref[...].astype(BF16)
    wu = wu_ref[...].astype(BF16)
    wd = wd_ref[...].astype(BF16)
    for r in range(x_ref.shape[0] // sub):
        rows = pl.ds(r * sub, sub)
        x = x_ref[rows, :]
        g = jnp.dot(x, wg, preferred_element_type=F32)
        u = jnp.dot(x, wu, preferred_element_type=F32)
        h = (_silu(g) * u).astype(BF16)
        acc_ref[rows, :] += jnp.dot(h, wd, preferred_element_type=F32)

    @pl.when(f == pl.num_programs(2) - 1)
    def _():
        o_ref[...] = acc_ref[...].astype(o_ref.dtype)


def moe_ffn(xe, w_gate, w_up, w_down, tm=2048, tf=256, sub=512):
    ne, m, d = xe.shape
    ff = w_gate.shape[2]
    tm = min(tm, m)
    return pl.pallas_call(
        functools.partial(_ffn_kernel, sub=min(sub, tm)),
        grid=(ne, m // tm, ff // tf),
        in_specs=[pl.BlockSpec((None, tm, d), lambda e, i, f: (e, i, 0)),
                  pl.BlockSpec((None, d, tf), lambda e, i, f: (e, 0, f)),
                  pl.BlockSpec((None, d, tf), lambda e, i, f: (e, 0, f)),
                  pl.BlockSpec((None, tf, d), lambda e, i, f: (e, f, 0))],
        out_specs=pl.BlockSpec((None, tm, d), lambda e, i, f: (e, i, 0)),
        out_shape=jax.ShapeDtypeStruct((ne, m, d), BF16),
        scratch_shapes=[pltpu.VMEM((tm, d), F32)],
        compiler_params=_cparams(("arbitrary", "arbitrary", "arbitrary"), VMEM_LIMIT_BYTES),
        name="moe_ffn",
    )(xe, w_gate, w_up, w_down)


def _combine_kernel(st_ref, nw_ref, pos_ref, gate_ref, ye_ref, x1_ref, g2_ref, fg_ref, o_ref, acc_ref,
                    *, win):
    b, j, nt = pl.program_id(0), pl.program_id(1), pl.num_programs(1)
    tt, ne = pos_ref.shape
    cap = ye_ref.shape[1]
    acc_ref[...] = jnp.zeros_like(acc_ref)
    pos = pos_ref[...]
    gate = gate_ref[...]
    col = lax.broadcasted_iota(I32, (tt, win), 1)

    def window(w, carry):
        acc = None
        for p in range(ne // 2):
            lhs, rhs = [], []
            for e in (2 * p, 2 * p + 1):
                first, start = _window(st_ref, (b * nt + j) * ne + e, w, win, cap)
                slot = col + start
                hit = (pos[:, e:e + 1] == slot) & (slot >= first)
                lhs.append(jnp.where(hit, gate[:, e:e + 1], 0.0).astype(BF16))
                rhs.append(ye_ref[e, pl.ds(start, win), :])
            part = jnp.dot(jnp.concatenate(lhs, axis=1), jnp.concatenate(rhs, axis=0),
                           preferred_element_type=F32)
            acc = part if acc is None else acc + part
        acc_ref[...] += acc
        return carry

    lax.fori_loop(0, nw_ref[b * nt + j], window, 0)
    xo = x1_ref[...] + g2_ref[...] * acc_ref[...]
    o_ref[...] = _rmsnorm(xo, fg_ref[...])


def moe_combine(pos_tm, gate_tm, base, ye, x1, g2, final_g, cap, tt=512, win=128):
    b, s, ne = pos_tm.shape
    d = x1.shape[2]
    starts, nwin = _slot_windows(base, cap, tt, win)
    tok = lambda n: pl.BlockSpec((None, tt, n), lambda i, j, st, nw: (i, j, 0))
    return pl.pallas_call(
        functools.partial(_combine_kernel, win=win),
        grid_spec=pltpu.PrefetchScalarGridSpec(
            num_scalar_prefetch=2,
            grid=(b, s // tt),
            in_specs=[tok(ne), tok(ne),
                      pl.BlockSpec((ne, cap, d), lambda i, j, st, nw: (0, i, 0)),
                      tok(d),
                      pl.BlockSpec((None, 1, d), lambda i, j, st, nw: (i, 0, 0)),
                      pl.BlockSpec((1, d), lambda i, j, st, nw: (0, 0))],
            out_specs=tok(d),
            scratch_shapes=[pltpu.VMEM((tt, d), F32)]),
        out_shape=jax.ShapeDtypeStruct((b, s, d), F32),
        compiler_params=_cparams(("arbitrary", "arbitrary"), VMEM_LIMIT_BYTES),
        name="moe_combine",
    )(starts, nwin, pos_tm, gate_tm, ye, x1, g2, final_g.reshape(1, d))


def _layer(x, ctx, mods, norm1_g, norm2_g, w_in, w_out, conv_w, conv_b, filt, hy_bias, s5p,
           s5_c_re, s5_c_im, s5_d, s5_glu_w, s5_glu_b, router_w, ex_w_gate, ex_w_up, ex_w_down,
           final_g):
    b, s, d = x.shape
    n_order, hw = hy_bias.shape
    hy_cols = (n_order + 1) * hw
    sw = w_in.shape[1] - hy_cols
    rows = s // GRID_W
    ne = router_w.shape[1]
    cap = CAPACITY_FACTOR * s // ne

    per_b = lambda k: mods[:b, k * d:(k + 1) * d].reshape(b, 1, d)
    ctx_v = lambda k: jnp.broadcast_to(mods[b:b + 1, k * d:(k + 1) * d].reshape(1, 1, d), (b, 1, d))
    sh1, sc1, g1, sh2, sc2, g2 = [per_b(k) for k in range(N_MOD)]

    w_in_bf = w_in.astype(BF16)
    (u_ctx,) = inproj(ctx, norm1_g, ctx_v(0), ctx_v(1), w_in_bf[:, hy_cols:], (sw,))
    z_hy, u = inproj(x, norm1_g, sh1, sc1, w_in_bf, (hy_cols, sw), conv=(0, conv_w, conv_b, GRID_W))

    hraw = hyena_filter_mlp(s, *filt)
    kf = hyena_filter_spectra(hraw, n_order, hw)
    y1 = hyena_conv(z_hy, 0, z_hy, 1, kf[0], hy_bias[0])
    hy = hyena_conv(y1, 0, z_hy, 2, kf[1], hy_bias[1])

    s5w = s5_chunk_weights(*s5p, s5_c_re, s5_c_im, sw // LANES, S5_CHUNK)
    tmaj = lambda a: jnp.transpose(a, (1, 0, 2)).reshape(-1, sw)
    yf, yb = s5_scan(tmaj(u_ctx), tmaj(u), s5w, b)
    ys = jnp.transpose((yf + yb).reshape(s, b, sw), (1, 0, 2))

    x1, h2, logits = mixer_tail(x, hy, ys, u, s5_d, s5_glu_w, s5_glu_b, w_out, g1, norm2_g,
                                sh2, sc2, router_w)
    pos_em, pos_tm, gate_tm, base = route(logits, cap)
    xe = moe_gather(pos_em, h2, base, cap)
    ye = moe_ffn(xe, ex_w_gate, ex_w_up, ex_w_down)
    return moe_combine(pos_tm, gate_tm, base, ye, x1, g2, final_g, cap)


def kernel(x, c, ctx, c_ctx, mod_w, mod_b, norm1_g, norm2_g, w_in, w_out, conv_w, conv_b, hy_w1, hy_b1, hy_freq, hy_w2, hy_b2, hy_w3, hy_b3, hy_bias, s5_lam_re, s5_lam_im, s5_log_step, s5_b_re, s5_b_im, s5_c_re, s5_c_im, s5_d, s5_glu_w, s5_glu_b, router_w, ex_w_gate, ex_w_up, ex_w_down, final_g):
    depth = mod_w.shape[0]
    assert depth == 1, "context-token updates of non-final layers are not implemented"
    b, _, d = x.shape
    l = 0
    pad = (-(b + 1)) % SUBLANES
    cond = jnp.concatenate([c, c_ctx[None], jnp.zeros((pad, d), F32)], axis=0)
    mods = adaln_mods(cond, mod_w[l], mod_b[l])
    filt = (hy_w1[l], hy_b1[l], hy_freq[l], hy_w2[l], hy_b2[l], hy_w3[l], hy_b3[l])
    s5p = (s5_lam_re[l], s5_lam_im[l], s5_log_step[l], s5_b_re[l], s5_b_im[l])
    return _layer(x, ctx, mods, norm1_g[l], norm2_g[l], w_in[l], w_out[l], conv_w[l], conv_b[l],
                  filt, hy_bias[l], s5p, s5_c_re[l], s5_c_im[l], s5_d[l], s5_glu_w[l], s5_glu_b[l],
                  router_w[l], ex_w_gate[l], ex_w_up[l], ex_w_down[l], final_g)
```

```python
import functools
import math

import numpy as np
import jax
import jax.numpy as jnp
from jax import lax
from jax.experimental import pallas as pl
from jax.experimental.pallas import tpu as pltpu

F32 = jnp.float32
BF16 = jnp.bfloat16
I32 = jnp.int32
HIGHEST = lax.Precision.HIGHEST

SUBLANES = 8
LANES = 128
VMEM_LIMIT_BYTES = 58 * 1024 * 1024

GRID_W = 64
N_MOD = 6
NORM_EPS = 1e-6
POS_BANDS = 16
DECAY_FAST = 0.3
DECAY_SLOW = 1.5
DECAY_TARGET = 1e-2
CAPACITY_FACTOR = 2

DFT_N1 = 64
DFT_N2 = 128
DFT_K1 = DFT_N1 // 2 + 1
DFT_ROWS = 2 * DFT_K1

S5_CHUNK = 4


def _cparams(sem, vmem=None):
    return pltpu.CompilerParams(dimension_semantics=sem, vmem_limit_bytes=vmem)


def _silu(x):
    return x * (1.0 / (1.0 + jnp.exp(-x)))


def _rmsnorm(x, g):
    ms = jnp.mean(x * x, axis=-1, keepdims=True)
    return x * lax.rsqrt(ms + NORM_EPS) * g


def _adaln_kernel(c_ref, w_ref, b_ref, o_ref):
    s = _silu(c_ref[...])
    o_ref[...] = jnp.dot(s, w_ref[...], precision=HIGHEST, preferred_element_type=F32) + b_ref[...]


def adaln_mods(cond, mod_w, mod_b, tn=1536):
    rows, d = cond.shape
    n = mod_w.shape[1]
    return pl.pallas_call(
        _adaln_kernel,
        grid=(n // tn,),
        in_specs=[pl.BlockSpec((rows, d), lambda j: (0, 0)),
                  pl.BlockSpec((d, tn), lambda j: (0, j)),
                  pl.BlockSpec((1, tn), lambda j: (0, j))],
        out_specs=pl.BlockSpec((rows, tn), lambda j: (0, j)),
        out_shape=jax.ShapeDtypeStruct((rows, n), F32),
        compiler_params=_cparams(("arbitrary",)),
        name="adaln",
    )(cond, mod_w, mod_b.reshape(1, n))


def _short_conv(z, w_ref, b_ref, row_len):
    length = z.shape[0]
    pos = lax.broadcasted_iota(I32, (length, 1), 0) % row_len
    zm = jnp.where(pos == 0, 0.0, pltpu.roll(z, 1, 0))
    zp = jnp.where(pos == row_len - 1, 0.0, pltpu.roll(z, length - 1, 0))
    w = w_ref[...]
    return zm * w[0:1, :] + z * w[1:2, :] + zp * w[2:3, :] + b_ref[...]


@functools.lru_cache(maxsize=None)
def _perm_time_major(batch, tt):
    n = batch * tt
    p = np.zeros((n, n), np.float32)
    t, b = np.meshgrid(np.arange(tt), np.arange(batch), indexing='ij')
    p[(t * batch + b).ravel(), (b * tt + t).ravel()] = 1.0
    return jnp.asarray(p, dtype=BF16)


def _inproj_kernel(x_ref, g_ref, sh_ref, sc_ref, w_ref, perm_ref, *refs, n_tok, row_len):
    bsz, tt, d = x_ref.shape
    h = _rmsnorm(x_ref[...], g_ref[...])
    h = h * (1.0 + sc_ref[...]) + sh_ref[...]
    z = jnp.dot(h.reshape(bsz * tt, d).astype(BF16), w_ref[...], preferred_element_type=F32)
    if n_tok:
        cw_ref, cb_ref, tok_ref, tm_ref = refs
        tok_ref[...] = _short_conv(z[:, :n_tok], cw_ref, cb_ref, row_len).reshape(bsz, tt, n_tok)
    else:
        (tm_ref,) = refs
    u = z[:, n_tok:].astype(BF16)
    tm_ref[...] = jnp.dot(perm_ref[...], u, preferred_element_type=F32).astype(BF16)


def inproj(x, g, shift, scale, w_bf16, n_tok, conv=None, tt=64):
    b, s, d = x.shape
    n = w_bf16.shape[1]
    perm = _perm_time_major(b, tt)
    const = lambda a: pl.BlockSpec(a.shape, lambda j: (0,) * a.ndim)
    args = [x, g.reshape(1, d), shift, scale, w_bf16, perm]
    out_specs = [pl.BlockSpec((tt * b, n - n_tok), lambda j: (j, 0))]
    out_shape = [jax.ShapeDtypeStruct((s * b, n - n_tok), BF16)]
    row_len = 1
    if n_tok:
        cw, cb, row_len = conv
        assert tt % row_len == 0
        args += [cw, cb.reshape(1, -1)]
        out_specs.insert(0, pl.BlockSpec((b, tt, n_tok), lambda j: (0, j, 0)))
        out_shape.insert(0, jax.ShapeDtypeStruct((b, s, n_tok), F32))
    return pl.pallas_call(
        functools.partial(_inproj_kernel, n_tok=n_tok, row_len=row_len),
        grid=(s // tt,),
        in_specs=[pl.BlockSpec((b, tt, d), lambda j: (0, j, 0))] + [const(a) for a in args[1:]],
        out_specs=out_specs,
        out_shape=out_shape,
        compiler_params=_cparams(("arbitrary",)),
        name="inproj",
    )(*args)


def _filt_mlp_kernel(w1t_ref, w1c_ref, w1s_ref, b1_ref, fr_ref, w2_ref, b2_ref, w3_ref, b3_ref,
                     o_ref, *, length, tl):
    i0 = pl.program_id(0) * tl
    idx = (lax.broadcasted_iota(I32, (tl, 1), 0) + i0).astype(F32)
    t = idx / float(length - 1)
    omega = (2.0 * math.pi) * idx / float(length)
    fstep = ((POS_BANDS - 1) - 1e-4) / (POS_BANDS - 1)
    f = 1e-4 + lax.broadcasted_iota(I32, (1, POS_BANDS), 1).astype(F32) * fstep
    arg = omega * f
    pre = (t * w1t_ref[...]
           + jnp.dot(jnp.cos(arg), w1c_ref[...], precision=HIGHEST, preferred_element_type=F32)
           - jnp.dot(jnp.sin(arg), w1s_ref[...], precision=HIGHEST, preferred_element_type=F32)
           + b1_ref[...])
    fr = fr_ref[...]
    h = jnp.sin(fr[0:1, :] * pre)
    h = jnp.sin(fr[1:2, :] * (jnp.dot(h, w2_ref[...], precision=HIGHEST,
                                      preferred_element_type=F32) + b2_ref[...]))
    o_ref[...] = jnp.dot(h, w3_ref[...], precision=HIGHEST, preferred_element_type=F32) + b3_ref[...]


def hyena_filter_mlp(length, w1, b1, freq, w2, b2, w3, b3, tl=512):
    fw = w1.shape[1]
    n = w3.shape[1]
    full = lambda shape: pl.BlockSpec(shape, lambda i: (0, 0))
    return pl.pallas_call(
        functools.partial(_filt_mlp_kernel, length=length, tl=tl),
        grid=(length // tl,),
        in_specs=[full((1, fw)), full((POS_BANDS, fw)), full((POS_BANDS, fw)), full((1, fw)),
                  full((2, fw)), full((fw, fw)), full((1, fw)), full((fw, n)), full((1, n))],
        out_specs=pl.BlockSpec((tl, n), lambda i: (i, 0)),
        out_shape=jax.ShapeDtypeStruct((length, n), F32),
        compiler_params=_cparams(("arbitrary",)),
        name="hyena_filter_mlp",
    )(w1[0:1], w1[1:1 + POS_BANDS], w1[1 + POS_BANDS:], b1.reshape(1, fw), freq, w2,
      b2.reshape(1, fw), w3, b3.reshape(1, n))


@functools.lru_cache(maxsize=None)
def _dft_tables():
    n1n, n2n, k1n = DFT_N1, DFT_N2, DFT_K1
    n = n1n * n2n
    half = n1n // 2
    k1 = np.arange(k1n)[:, None]
    n1 = np.arange(half)[None, :]
    th = 2.0 * np.pi * k1 * n1 / n1n
    f1 = np.zeros((DFT_ROWS, half))
    f1[0::2] = np.cos(th)
    f1[1::2] = -np.sin(th)
    wgt = np.where((k1 == 0) | (k1 == half), 1.0, 2.0)
    g1 = np.zeros((half, DFT_ROWS))
    g1[:, 0::2] = (wgt * np.cos(th)).T / n
    g1[:, 1::2] = (-wgt * np.sin(th)).T / n
    eye = np.eye(SUBLANES)
    fk = np.kron(f1, eye)
    gk = np.kron(g1, eye)
    k2 = np.arange(n2n)[:, None]
    n2 = np.arange(n2n)[None, :]
    f3 = np.zeros((k1n, 2 * n2n, 2 * n2n))
    for kk in range(k1n):
        ph = 2.0 * np.pi * n2 * (n1n * k2 + kk) / n
        tr, ti = np.cos(ph), -np.sin(ph)
        f3[kk, :n2n, :n2n] = tr
        f3[kk, :n2n, n2n:] = -ti
        f3[kk, n2n:, :n2n] = ti
        f3[kk, n2n:, n2n:] = tr
    g3 = np.transpose(f3, (0, 2, 1))
    to = lambda a: jnp.asarray(a, dtype=F32).astype(BF16)
    return to(fk), to(gk), to(f3), to(g3)


def _dft_stage1(src_ref, a_ref, fk_ref):
    half = DFT_N1 // 2

    def body(m, carry):
        sub = pl.ds(pl.multiple_of(m * SUBLANES, SUBLANES), SUBLANES)
        rows = [src_ref.at[pl.ds(DFT_N2 * n1, DFT_N2)][sub, :] for n1 in range(half)]
        rhs = jnp.concatenate(rows, axis=0).astype(BF16)
        out = jnp.dot(fk_ref[...], rhs, preferred_element_type=F32)
        for j in range(DFT_ROWS):
            a_ref.at[pl.ds(DFT_N2 * j, DFT_N2)][sub, :] = out[SUBLANES * j:SUBLANES * (j + 1)]
        return carry

    lax.fori_loop(0, DFT_N2 // SUBLANES, body, 0, unroll=2)


def _dft_stage3(a_ref, f3_ref, k1):
    r0 = pl.multiple_of(k1 * (2 * DFT_N2), 2 * DFT_N2)
    a = a_ref[pl.ds(r0, 2 * DFT_N2), :].astype(BF16)
    x = jnp.dot(f3_ref[k1], a, preferred_element_type=F32)
    return x[:DFT_N2], x[DFT_N2:]


def _filt_spec_kernel(hf_ref, hb_ref, fk_ref, f3_ref, o_ref, src_ref, af_ref, ab_ref, *, length):
    c = hf_ref.shape[1]
    cb = pl.program_id(1)
    hw = pl.num_programs(1) * c
    row = lax.broadcasted_iota(I32, (length, 1), 0)
    t = row.astype(F32) / float(length - 1)
    ch = (lax.broadcasted_iota(I32, (1, c), 1) + cb * c).astype(F32)
    d0 = math.log(DECAY_TARGET) / DECAY_FAST
    d1 = math.log(DECAY_TARGET) / DECAY_SLOW
    deltas = jnp.abs(d0 + ch * ((d1 - d0) / float(hw - 1)))
    decay = jnp.exp(-t * deltas)
    fwd = hf_ref[...] * decay
    bwd = jnp.where(row == 0, 0.0, hb_ref[...] * decay)
    inv = 1.0 / (jnp.sum(jnp.abs(fwd), axis=0, keepdims=True)
                 + jnp.sum(jnp.abs(bwd), axis=0, keepdims=True))
    src_ref[...] = fwd
    _dft_stage1(src_ref, af_ref, fk_ref)
    src_ref[...] = bwd
    _dft_stage1(src_ref, ab_ref, fk_ref)

    def body(k1, carry):
        fr, fi = _dft_stage3(af_ref, f3_ref, k1)
        br, bi = _dft_stage3(ab_ref, f3_ref, k1)
        o_ref[k1, 0] = ((fr + br) * inv).astype(o_ref.dtype)
        o_ref[k1, 1] = ((fi - bi) * inv).astype(o_ref.dtype)
        return carry

    lax.fori_loop(0, DFT_K1, body, 0)


def hyena_filter_spectra(hraw, n_order, width, c_blk=256):
    length = hraw.shape[0]
    assert 2 * length == DFT_N1 * DFT_N2
    fk, _, f3, _ = _dft_tables()
    ncb = width // c_blk
    return pl.pallas_call(
        functools.partial(_filt_spec_kernel, length=length),
        grid=(n_order, ncb),
        in_specs=[pl.BlockSpec((length, c_blk), lambda o, j: (0, o * 2 * ncb + j)),
                  pl.BlockSpec((length, c_blk), lambda o, j: (0, o * 2 * ncb + ncb + j)),
                  pl.BlockSpec(fk.shape, lambda o, j: (0, 0)),
                  pl.BlockSpec(f3.shape, lambda o, j: (0, 0, 0))],
        out_specs=pl.BlockSpec((None, DFT_K1, 2, DFT_N2, c_blk), lambda o, j: (o, 0, 0, 0, j)),
        out_shape=jax.ShapeDtypeStruct((n_order, DFT_K1, 2, DFT_N2, width), BF16),
        scratch_shapes=[pltpu.VMEM((length, c_blk), F32),
                        pltpu.VMEM((DFT_ROWS * DFT_N2, c_blk), F32),
                        pltpu.VMEM((DFT_ROWS * DFT_N2, c_blk), F32)],
        compiler_params=_cparams(("arbitrary", "arbitrary"), VMEM_LIMIT_BYTES),
        name="hyena_filter_spectrum",
    )(hraw, hraw, fk, f3)


def _hyena_conv_kernel(s_ref, m_ref, kf_ref, bias_ref, fk_ref, f3_ref, g3_ref, gk_ref, o_ref, a_ref,
                       *, group):
    half = DFT_N1 // 2
    blk_rows = 2 * DFT_N2
    _dft_stage1(s_ref, a_ref, fk_ref)

    def body3(i, carry):
        k1s = [i * group + q for q in range(group)]
        r0s = [pl.multiple_of(k1 * blk_rows, blk_rows) for k1 in k1s]
        blocks = [a_ref[pl.ds(r0, blk_rows), :].astype(BF16) for r0 in r0s]
        outs = []
        for k1, a in zip(k1s, blocks):
            x = jnp.dot(f3_ref[k1], a, preferred_element_type=F32)
            xr, xi = x[:DFT_N2], x[DFT_N2:]
            kr = kf_ref[k1, 0].astype(F32)
            ki = kf_ref[k1, 1].astype(F32)
            y = jnp.concatenate([xr * kr - xi * ki, xr * ki + xi * kr], axis=0).astype(BF16)
            outs.append(jnp.dot(g3_ref[k1], y, preferred_element_type=F32))
        for r0, o in zip(r0s, outs):
            a_ref[pl.ds(r0, blk_rows), :] = o
        return carry

    lax.fori_loop(0, DFT_K1 // group, body3, 0)
    bias = bias_ref[...]

    def body1(m, carry):
        sub = pl.ds(pl.multiple_of(m * SUBLANES, SUBLANES), SUBLANES)
        blk = [a_ref.at[pl.ds(DFT_N2 * j, DFT_N2)][sub, :] for j in range(DFT_ROWS)]
        rhs = jnp.concatenate(blk, axis=0).astype(BF16)
        out = jnp.dot(gk_ref[...], rhs, preferred_element_type=F32)
        for n1 in range(half):
            blk_n1 = pl.ds(DFT_N2 * n1, DFT_N2)
            conv = out[SUBLANES * n1:SUBLANES * (n1 + 1)]
            o_ref.at[blk_n1][sub, :] = m_ref.at[blk_n1][sub, :] * (
                conv + s_ref.at[blk_n1][sub, :] * bias)
        return carry

    lax.fori_loop(0, DFT_N2 // SUBLANES, body1, 0, unroll=2)


def hyena_conv(sig, sig_col, mul, mul_col, kf, bias, c_blk=256, group=11):
    b, length, _ = sig.shape
    width = kf.shape[-1]
    ncb = width // c_blk
    assert DFT_K1 % group == 0
    fk, gk, f3, g3 = _dft_tables()
    once = pl.Buffered(1)
    const2 = lambda a: pl.BlockSpec(a.shape, lambda j, i: (0, 0), pipeline_mode=once)
    const3 = lambda a: pl.BlockSpec(a.shape, lambda j, i: (0, 0, 0), pipeline_mode=once)
    return pl.pallas_call(
        functools.partial(_hyena_conv_kernel, group=group),
        grid=(ncb, b),
        in_specs=[pl.BlockSpec((None, length, c_blk), lambda j, i: (i, 0, sig_col * ncb + j)),
                  pl.BlockSpec((None, length, c_blk), lambda j, i: (i, 0, mul_col * ncb + j)),
                  pl.BlockSpec((DFT_K1, 2, DFT_N2, c_blk), lambda j, i: (0, 0, 0, j),
                               pipeline_mode=once),
                  pl.BlockSpec((1, c_blk), lambda j, i: (0, j), pipeline_mode=once),
                  const2(fk), const3(f3), const3(g3), const2(gk)],
        out_specs=pl.BlockSpec((None, length, c_blk), lambda j, i: (i, 0, j)),
        out_shape=jax.ShapeDtypeStruct((b, length, width), F32),
        scratch_shapes=[pltpu.VMEM((DFT_ROWS * DFT_N2, c_blk), F32)],
        compiler_params=_cparams(("arbitrary", "arbitrary"), VMEM_LIMIT_BYTES),
        name="hyena_conv",
    )(sig, mul, kf, bias.reshape(1, width), fk, f3, g3, gk)


def _s5_powers_kernel(lr_ref, li_ref, dt_ref, vr_ref, vi_ref, or_ref, oi_ref, *, n_pow, zoh):
    lr, li, dt = lr_ref[...], li_ref[...], jnp.exp(dt_ref[...])
    mag = jnp.exp(lr * dt)
    ar = mag * jnp.cos(li * dt)
    ai = mag * jnp.sin(li * dt)
    vr, vi = vr_ref[...], vi_ref[...]
    if zoh:
        den = 1.0 / (lr * lr + li * li)
        qr = ((ar - 1.0) * lr + ai * li) * den
        qi = (ai * lr - (ar - 1.0) * li) * den
        vr, vi = qr * vr - qi * vi, qr * vi + qi * vr
    for j in range(n_pow):
        or_ref[j] = vr
        oi_ref[j] = vi
        vr, vi = ar * vr - ai * vi, ar * vi + ai * vr


def s5_powers(lam_re, lam_im, log_step, v_re, v_im, state_axis, n_pow, zoh):
    nd, g, a, b = v_re.shape
    expand = (lambda x: x[..., :, None]) if state_axis == 2 else (lambda x: x[..., None, :])
    rep = lambda x: jnp.broadcast_to(expand(x), v_re.shape).reshape(nd * g, a * b)
    dt = jnp.broadcast_to(log_step[:, :, None, None], v_re.shape).reshape(nd * g, a * b)
    flat = lambda x: x.reshape(nd * g, a * b)
    shp = jax.ShapeDtypeStruct((n_pow, nd * g, a * b), F32)
    o_r, o_i = pl.pallas_call(
        functools.partial(_s5_powers_kernel, n_pow=n_pow, zoh=zoh), out_shape=[shp, shp],
        name="s5_powers",
    )(rep(lam_re), rep(lam_im), dt, flat(v_re), flat(v_im))
    un = lambda x: x.reshape(n_pow, nd, g, a, b)
    return un(o_r), un(o_i)


def _s5_taps_kernel(cr_ref, ci_ref, br_ref, bi_ref, o_ref):
    n_pow, nk = br_ref.shape[0], br_ref.shape[1]
    for j in range(n_pow):
        for k in range(nk):
            o_ref[j, k] = (jnp.dot(cr_ref[k], br_ref[j, k], preferred_element_type=F32)
                           - jnp.dot(ci_ref[k], bi_ref[j, k], preferred_element_type=F32))


def s5_taps(c_re, c_im, bbp_r, bbp_i, nk):
    nd, g, h, p = c_re.shape
    n_pow = bbp_r.shape[0]
    gpk = g // nk
    eye = jnp.eye(gpk, dtype=F32)
    bdiag = lambda c: jnp.einsum('dkahp,ab->dkahbp', c.reshape(nd, nk, gpk, h, p), eye).reshape(
        nd, nk, gpk * h, gpk * p).astype(BF16)
    flat = lambda x: jnp.transpose(x, (1, 0, 2, 3, 4)).reshape(nd, n_pow, nk, gpk * p, h).astype(BF16)
    per_d = lambda a: pl.BlockSpec((None,) + a.shape[1:], lambda d: (d,) + (0,) * (a.ndim - 1))
    args = (bdiag(c_re), bdiag(c_im), flat(bbp_r), flat(bbp_i))
    shp = jax.ShapeDtypeStruct((nd, n_pow, nk, gpk * h, h), F32)
    out = pl.pallas_call(
        _s5_taps_kernel,
        grid=(nd,),
        in_specs=[per_d(a) for a in args],
        out_specs=per_d(shp),
        out_shape=shp,
        compiler_params=_cparams(("arbitrary",)),
        name="s5_taps",
    )(*args)
    return out.reshape(nd, n_pow, g, h, h)


def _s5_scan_kernel(ucf_ref, uf_ref, ucb_ref, ub_ref, qf_ref, qb_ref, pf_ref, pb_ref, mf_ref, mb_ref,
                    lam_ref, yf_ref, yb_ref, xf_ref, xb_ref, st_ref, *, nc, kpp, tlen):
    width = uf_ref.shape[1]
    batch = st_ref.shape[3]
    nk = xf_ref.shape[1]
    ck = width // nk
    rows = xf_ref.shape[2]
    cpg = rows // batch
    i = pl.program_id(0)
    is_ctx = i < nc

    @pl.when(i == 0)
    def _():
        st_ref[...] = jnp.zeros_like(st_ref)

    pick = lambda c_ref, l_ref: jnp.where(is_ctx, c_ref[...], l_ref[...]).astype(F32).reshape(
        cpg, tlen, batch, width)
    uf = pick(ucf_ref, uf_ref)
    ub = pick(ucb_ref, ub_ref)

    def scan(x_ref, d, reverse):
        for k0 in range(0, nk, kpp):
            ks = slice(k0, k0 + kpp)
            lr, li = lam_ref[d, 0, ks], lam_ref[d, 1, ks]

            def body(c, carry, ks=ks, lr=lr, li=li):
                sr, si = carry
                cc = (cpg - 1 - c) if reverse else c
                r = pl.ds(pl.multiple_of(cc * batch, batch), batch)
                qr, qi = x_ref[0, ks, r, :], x_ref[1, ks, r, :]
                x_ref[0, ks, r, :] = sr
                x_ref[1, ks, r, :] = si
                return lr * sr - li * si + qr, lr * si + li * sr + qi

            st_ref[d, 0, ks], st_ref[d, 1, ks] = lax.fori_loop(
                0, cpg, body, (st_ref[d, 0, ks], st_ref[d, 1, ks]))

    for u, q_ref, p_ref, m_ref, x_ref, y_ref, d in ((uf, qf_ref, pf_ref, mf_ref, xf_ref, yf_ref, 0),
                                                    (ub, qb_ref, pb_ref, mb_ref, xb_ref, yb_ref, 1)):
        uks = []
        for k in range(nk):
            uk = jnp.concatenate([u[:, t, :, k * ck:(k + 1) * ck].reshape(rows, ck)
                                  for t in range(tlen)], axis=1).astype(BF16)
            uks.append(uk)
            for ri in range(2):
                x_ref[ri, k] = jnp.dot(uk, q_ref[ri, k], preferred_element_type=F32)
        scan(x_ref, d, d == 1)
        for k in range(nk):
            yk = (jnp.dot(uks[k], m_ref[k], preferred_element_type=F32)
                  + jnp.dot(x_ref[0, k].astype(BF16), p_ref[0, k], preferred_element_type=F32)
                  + jnp.dot(x_ref[1, k].astype(BF16), p_ref[1, k], preferred_element_type=F32))
            y_ref[:, k * ck:(k + 1) * ck] = jnp.stack(
                [yk[:, t * ck:(t + 1) * ck].reshape(cpg, batch, ck) for t in range(tlen)],
                axis=1).reshape(cpg * tlen * batch, ck)


def s5_chunk_weights(lam_re, lam_im, log_step, b_re, b_im, c_re, c_im, nk, tlen):
    nd, g, p, h = b_re.shape
    gpk, sk = g // nk, g * p // nk
    bb_r, bb_i = s5_powers(lam_re, lam_im, log_step, b_re, b_im, 2, tlen, True)
    cl_r, cl_i = s5_powers(lam_re, lam_im, log_step, c_re, c_im, 3, tlen + 1, False)
    ones = jnp.ones((nd, g, p, 2), F32)
    pw_r, pw_i = s5_powers(lam_re, lam_im, log_step, ones, jnp.zeros_like(ones), 2, tlen + 1, False)
    taps = s5_taps(c_re, c_im, bb_r, bb_i, nk)
    dirs = np.arange(nd)[:, None]
    step = np.arange(tlen)[None, :]
    jq = np.where(dirs == 0, tlen - 1 - step, step)
    jp = np.where(dirs == 0, step + 1, tlen - step)
    lag = step[0][None, None, :] - step[0][None, :, None]
    lag = np.where(dirs[:, :, None] == 0, lag, -lag)
    eye = jnp.eye(gpk, dtype=F32)
    qsel = jnp.stack([bb_r, bb_i])[:, jq, dirs].reshape(2, nd, tlen, nk, gpk, p, h)
    q = jnp.einsum('rdskaph,ab->drksahbp', qsel, eye).reshape(nd, 2, nk, tlen * gpk * h, sk)
    psel = jnp.stack([cl_r, -cl_i])[:, jp, dirs].reshape(2, nd, tlen, nk, gpk, h, p)
    pm = jnp.einsum('rdtkahp,ab->drkaptbh', psel, eye).reshape(nd, 2, nk, sk, tlen * gpk * h)
    tsel = jnp.where((lag >= 0)[..., None, None, None], taps[dirs[:, :, None], np.maximum(lag, 0)], 0.0)
    tsel = tsel.reshape(nd, tlen, tlen, nk, gpk, h, h)
    m = jnp.einsum('dstkahq,ab->dksaqtbh', tsel, eye).reshape(nd, nk, tlen * gpk * h, tlen * gpk * h)
    decay = jnp.stack([pw_r[tlen, ..., 0], pw_i[tlen, ..., 0]], axis=1).reshape(nd, 2, nk, 1, sk)
    return q.astype(BF16), pm.astype(BF16), m.astype(BF16), decay


def s5_scan(uc, u, weights, batch, steps=64, kpp=2):
    q, pm, m, decay = weights
    width = u.shape[1]
    nk, sk = q.shape[2], q.shape[4]
    tlen = m.shape[2] * nk // width
    cpg = steps // tlen
    rpc = steps * batch
    nc, nl = uc.shape[0] // rpc, u.shape[0] // rpc
    assert steps % tlen == 0 and uc.shape[0] % rpc == 0 and u.shape[0] % rpc == 0
    lam = jnp.broadcast_to(decay, (2, 2, nk, batch, sk))
    blk = lambda f: pl.BlockSpec((rpc, width), lambda i: (f(i), 0))
    lat_f = lambda i: jnp.maximum(i - nc, 0)
    lat_b = lambda i: jnp.clip(nl - 1 - i + nc, 0, nl - 1)
    once = pl.Buffered(1)
    par = lambda a, d: pl.BlockSpec((None,) + a.shape[1:], lambda i: (d,) + (0,) * (a.ndim - 1),
                                    pipeline_mode=once)
    shp = jax.ShapeDtypeStruct(u.shape, F32)
    return pl.pallas_call(
        functools.partial(_s5_scan_kernel, nc=nc, kpp=kpp, tlen=tlen),
        grid=(nc + nl,),
        in_specs=[blk(lambda i: jnp.minimum(i, nc - 1)), blk(lat_f),
                  blk(lambda i: jnp.maximum(nc - 1 - i, 0)), blk(lat_b),
                  par(q, 0), par(q, 1), par(pm, 0), par(pm, 1), par(m, 0), par(m, 1),
                  pl.BlockSpec(lam.shape, lambda i: (0, 0, 0, 0, 0), pipeline_mode=once)],
        out_specs=[blk(lat_f), blk(lat_b)],
        out_shape=[shp, shp],
        scratch_shapes=[pltpu.VMEM((2, nk, cpg * batch, sk), F32),
                        pltpu.VMEM((2, nk, cpg * batch, sk), F32),
                        pltpu.VMEM((2, 2, nk, batch, sk), F32)],
        compiler_params=_cparams(("arbitrary",), VMEM_LIMIT_BYTES),
        name="s5_scan",
    )(uc, u, uc, u, q, q, pm, pm, m, m, lam)


def _mixer_tail_kernel(x_ref, hy_ref, yf_ref, yb_ref, u_ref, d_ref, gw_ref, gb_ref, woh_ref, wos_ref,
                       g1_ref, n2_ref, sh2_ref, sc2_ref, rwt_ref, perm_ref, x1_ref, h2_ref, lg_ref):
    bsz, tt, d = x_ref.shape
    rows = bsz * tt
    y = yf_ref[...] + yb_ref[...] + d_ref[...] * u_ref[...].astype(F32)
    y = 0.5 * y * (1.0 + jnp.tanh(math.sqrt(2.0 / math.pi) * (y + 0.044715 * (y * y * y))))
    gate = jnp.dot(y.astype(BF16), gw_ref[...], preferred_element_type=F32) + gb_ref[...]
    s5 = (y * (1.0 / (1.0 + jnp.exp(-gate)))).astype(BF16)
    s5 = jnp.dot(perm_ref[...], s5, preferred_element_type=F32).astype(BF16)
    hy = hy_ref[...].reshape(rows, hy_ref.shape[2]).astype(BF16)
    mix = (jnp.dot(hy, woh_ref[...], preferred_element_type=F32)
           + jnp.dot(s5, wos_ref[...], preferred_element_type=F32))
    x1 = x_ref[...] + g1_ref[...] * mix.reshape(bsz, tt, d)
    x1_ref[...] = x1
    h2 = _rmsnorm(x1, n2_ref[...]) * (1.0 + sc2_ref[...]) + sh2_ref[...]
    h2_ref[...] = h2.astype(BF16)
    lg = lax.dot_general(rwt_ref[...], h2.reshape(rows, d), (((1,), (1,)), ((), ())),
                         precision=HIGHEST, preferred_element_type=F32)
    for i in range(bsz):
        lg_ref[i] = lg[:, i * tt:(i + 1) * tt]


def mixer_tail(x, hy, yf, yb, u_tm, s5_d, glu_w, glu_b, w_out, g1, norm2_g, sh2, sc2, router_w,
               tt=128):
    b, s, d = x.shape
    hw = hy.shape[2]
    sw = u_tm.shape[1]
    ne = router_w.shape[1]
    perm = _perm_time_major(b, tt).T
    once = pl.Buffered(1)
    tok = lambda n: pl.BlockSpec((b, tt, n), lambda j: (0, j, 0))
    tmj = pl.BlockSpec((tt * b, sw), lambda j: (j, 0))
    const = lambda a: pl.BlockSpec(a.shape, lambda j: (0,) * a.ndim, pipeline_mode=once)
    consts = [s5_d.reshape(1, sw), glu_w.astype(BF16), glu_b.reshape(1, sw), w_out[:hw].astype(BF16),
              w_out[hw:].astype(BF16), g1, norm2_g.reshape(1, d), sh2, sc2, router_w.T, perm]
    return pl.pallas_call(
        _mixer_tail_kernel,
        grid=(s // tt,),
        in_specs=[tok(d), tok(hw), tmj, tmj, tmj] + [const(a) for a in consts],
        out_specs=[tok(d), tok(d), pl.BlockSpec((b, ne, tt), lambda j: (0, 0, j))],
        out_shape=[jax.ShapeDtypeStruct((b, s, d), F32), jax.ShapeDtypeStruct((b, s, d), BF16),
                   jax.ShapeDtypeStruct((b, ne, s), F32)],
        compiler_params=_cparams(("arbitrary",), VMEM_LIMIT_BYTES),
        name="mixer_tail",
    )(x, hy, yf, yb, u_tm, *consts)


def _lane_cumsum_exclusive(x):
    rows, s = x.shape
    ii = lax.broadcasted_iota(I32, (LANES, LANES), 0)
    jj = lax.broadcasted_iota(I32, (LANES, LANES), 1)
    tri = jnp.where(ii < jj, 1.0, 0.0).astype(BF16)
    carry = jnp.zeros((rows, 1), F32)
    out, base = [], []
    for blk in range(s // LANES):
        xb = x[:, blk * LANES:(blk + 1) * LANES]
        out.append(jnp.dot(xb.astype(BF16), tri, preferred_element_type=F32) + carry)
        base.append(carry)
        carry = carry + jnp.sum(xb, axis=1, keepdims=True)
    return jnp.concatenate(out, axis=1), jnp.concatenate(base, axis=1)


def _route_kernel(lg_ref, pos_em_ref, pos_tm_ref, gate_tm_ref, base_ref, aff_ref, *, cap):
    lg = lg_ref[...]
    ne, s = lg.shape
    ex = jnp.exp(lg - jnp.max(lg, axis=0, keepdims=True))
    aff_ref[...] = ex / jnp.sum(ex, axis=0, keepdims=True)
    aff = aff_ref[...]
    count_ge = lambda v, t: jnp.sum(jnp.where(v >= t, 1.0, 0.0), axis=1, keepdims=True)

    def coarse(i, tb):
        cand = tb | jnp.left_shift(jnp.int32(1), 30 - i)
        return jnp.where(count_ge(aff, pltpu.bitcast(cand, F32)) >= cap, cand, tb)

    tb = lax.fori_loop(0, 31, coarse, jnp.zeros((ne, 1), I32))
    t_hi = pltpu.bitcast(tb, F32)
    ulp = pltpu.bitcast(tb + 1, F32) - t_hi
    resid = aff - t_hi

    def fine(j, carry):
        c, step = carry
        cand = c + step
        return jnp.where(count_ge(resid, cand) >= cap, cand, c), step * 0.5

    t_lo, _ = lax.fori_loop(0, 24, fine, (jnp.zeros((ne, 1), F32), ulp * 0.5))
    gt = resid > t_lo
    eq = resid == t_lo
    need = cap - jnp.sum(jnp.where(gt, 1.0, 0.0), axis=1, keepdims=True)
    eq_rank, _ = _lane_cumsum_exclusive(jnp.where(eq, 1.0, 0.0))
    sel = gt | (eq & (eq_rank < need))
    pos, base = _lane_cumsum_exclusive(jnp.where(sel, 1.0, 0.0))
    posf = jnp.where(sel, pos + 1.0, 0.0)
    gate = jnp.where(sel, aff, 0.0)
    pos_em_ref[...] = posf.astype(I32) - 1
    base_ref[...] = base.astype(I32)
    hi = jnp.floor(posf * (1.0 / 16.0))
    lo = posf - 16.0 * hi
    ii = lax.broadcasted_iota(I32, (LANES, LANES), 0)
    jj = lax.broadcasted_iota(I32, (LANES, LANES), 1)
    eye = jnp.where(ii == jj, 1.0, 0.0)
    nt = (((1,), (1,)), ((), ()))
    for blk in range(s // LANES):
        sl = slice(blk * LANES, (blk + 1) * LANES)
        t_hi = lax.dot_general(eye.astype(BF16), hi[:, sl].astype(BF16), nt, preferred_element_type=F32)
        t_lo = lax.dot_general(eye.astype(BF16), lo[:, sl].astype(BF16), nt, preferred_element_type=F32)
        pos_tm_ref[sl, :] = (16.0 * t_hi + t_lo).astype(I32) - 1
        gate_tm_ref[sl, :] = lax.dot_general(eye, gate[:, sl], nt, precision=HIGHEST,
                                             preferred_element_type=F32)


def route(logits, cap):
    b, ne, s = logits.shape
    nb = s // LANES
    return pl.pallas_call(
        functools.partial(_route_kernel, cap=cap),
        grid=(b,),
        in_specs=[pl.BlockSpec((None, ne, s), lambda i: (i, 0, 0))],
        out_specs=[pl.BlockSpec((None, ne, s), lambda i: (i, 0, 0)),
                   pl.BlockSpec((None, s, ne), lambda i: (i, 0, 0)),
                   pl.BlockSpec((None, s, ne), lambda i: (i, 0, 0)),
                   pl.BlockSpec((None, ne, nb), lambda i: (i, 0, 0))],
        out_shape=[jax.ShapeDtypeStruct((b, ne, s), I32), jax.ShapeDtypeStruct((b, s, ne), I32),
                   jax.ShapeDtypeStruct((b, s, ne), F32), jax.ShapeDtypeStruct((b, ne, nb), I32)],
        scratch_shapes=[pltpu.VMEM((ne, s), F32)],
        compiler_params=_cparams(("arbitrary",)),
        name="route",
    )(logits)


SLOT_ALIGN = 16


def _slot_windows(base, cap, chunk, win):
    lo = base[:, :, ::chunk // LANES]
    hi = jnp.concatenate([lo[:, :, 1:], jnp.full_like(lo[:, :, :1], cap)], axis=2)
    start = (lo // SLOT_ALIGN) * SLOT_ALIGN
    nwin = jnp.max((hi - start + win - 1) // win, axis=1)
    return jnp.transpose(start, (0, 2, 1)).reshape(-1), nwin.reshape(-1)


def _window(st_ref, idx, w, win, cap):
    first = st_ref[idx] + w * win
    return first, pl.multiple_of(jnp.minimum(first, cap - win), SLOT_ALIGN)


def _gather_kernel(st_ref, nw_ref, pos_ref, h_ref, o_ref, *, win):
    b, j, nch = pl.program_id(0), pl.program_id(1), pl.num_programs(1)
    ne, cap, _ = o_ref.shape
    tk = h_ref.shape[0]

    @pl.when(j == 0)
    def _():
        o_ref[...] = jnp.zeros_like(o_ref)

    pos = pos_ref[...]
    h = h_ref[...]
    row = lax.broadcasted_iota(I32, (win, tk), 0)

    def window(w, carry):
        starts, lhs = [], []
        for e in range(ne):
            first, start = _window(st_ref, (b * nch + j) * ne + e, w, win, cap)
            slot = row + start
            hit = (pos[e:e + 1, :] == slot) & (slot >= first)
            lhs.append(jnp.where(hit, 1.0, 0.0).astype(BF16))
            starts.append(start)
        res = jnp.dot(jnp.concatenate(lhs, axis=0), h, preferred_element_type=F32)
        for e, start in enumerate(starts):
            o_ref[e, pl.ds(start, win), :] += res[e * win:(e + 1) * win].astype(o_ref.dtype)
        return carry

    lax.fori_loop(0, nw_ref[b * nch + j], window, 0)


def moe_gather(pos_em, h2, base, cap, tk=256, win=64):
    b, ne, s = pos_em.shape
    d = h2.shape[2]
    starts, nwin = _slot_windows(base, cap, tk, win)
    return pl.pallas_call(
        functools.partial(_gather_kernel, win=win),
        grid_spec=pltpu.PrefetchScalarGridSpec(
            num_scalar_prefetch=2,
            grid=(b, s // tk),
            in_specs=[pl.BlockSpec((None, ne, tk), lambda i, j, st, nw: (i, 0, j)),
                      pl.BlockSpec((None, tk, d), lambda i, j, st, nw: (i, j, 0))],
            out_specs=pl.BlockSpec((ne, cap, d), lambda i, j, st, nw: (0, i, 0))),
        out_shape=jax.ShapeDtypeStruct((ne, b * cap, d), BF16),
        compiler_params=_cparams(("arbitrary", "arbitrary"), VMEM_LIMIT_BYTES),
        name="moe_gather",
    )(starts, nwin, pos_em, h2)


def _ffn_kernel(x_ref, wg_ref, wu_ref, wd_ref, o_ref, acc_ref, *, sub):
    f = pl.program_id(2)

    @pl.when(f == 0)
    def _():
        acc_ref[...] = jnp.zeros_like(acc_ref)

    wg = wg_ref[...].astype(BF16)
    wu = wu_ref[...].astype(BF16)
    wd = wd_ref[...].astype(BF16)
    for r in range(x_ref.shape[0] // sub):
        rows = pl.ds(r * sub, sub)
        x = x_ref[rows, :]
        g = jnp.dot(x, wg, preferred_element_type=F32)
        u = jnp.dot(x, wu, preferred_element_type=F32)
        h = (_silu(g) * u).astype(BF16)
        acc_ref[rows, :] += jnp.dot(h, wd, preferred_element_type=F32)

    @pl.when(f == pl.num_programs(2) - 1)
    def _():
        o_ref[...] = acc_ref[...].astype(o_ref.dtype)


def moe_ffn(xe, w_gate, w_up, w_down, tm=2048, tf=256, sub=512):
    ne, m, d = xe.shape
    ff = w_gate.shape[2]
    tm = min(tm, m)
    return pl.pallas_call(
        functools.partial(_ffn_kernel, sub=min(sub, tm)),
        grid=(ne, m // tm, ff // tf),
        in_specs=[pl.BlockSpec((None, tm, d), lambda e, i, f: (e, i, 0)),
                  pl.BlockSpec((None, d, tf), lambda e, i, f: (e, 0, f)),
                  pl.BlockSpec((None, d, tf), lambda e, i, f: (e, 0, f)),
                  pl.BlockSpec((None, tf, d), lambda e, i, f: (e, f, 0))],
        out_specs=pl.BlockSpec((None, tm, d), lambda e, i, f: (e, i, 0)),
        out_shape=jax.ShapeDtypeStruct((ne, m, d), BF16),
        scratch_shapes=[pltpu.VMEM((tm, d), F32)],
        compiler_params=_cparams(("arbitrary", "arbitrary", "arbitrary"), VMEM_LIMIT_BYTES),
        name="moe_ffn",
    )(xe, w_gate, w_up, w_down)


def _combine_kernel(st_ref, nw_ref, pos_ref, gate_ref, ye_ref, x1_ref, g2_ref, fg_ref, o_ref, acc_ref,
                    *, win):
    b, j, nt = pl.program_id(0), pl.program_id(1), pl.num_programs(1)
    tt, ne = pos_ref.shape
    cap = ye_ref.shape[1]
    acc_ref[...] = jnp.zeros_like(acc_ref)
    pos = pos_ref[...]
    gate = gate_ref[...]
    col = lax.broadcasted_iota(I32, (tt, win), 1)

    def window(w, carry):
        acc = None
        for p in range(ne // 2):
            lhs, rhs = [], []
            for e in (2 * p, 2 * p + 1):
                first, start = _window(st_ref, (b * nt + j) * ne + e, w, win, cap)
                slot = col + start
                hit = (pos[:, e:e + 1] == slot) & (slot >= first)
                lhs.append(jnp.where(hit, gate[:, e:e + 1], 0.0).astype(BF16))
                rhs.append(ye_ref[e, pl.ds(start, win), :])
            part = jnp.dot(jnp.concatenate(lhs, axis=1), jnp.concatenate(rhs, axis=0),
                           preferred_element_type=F32)
            acc = part if acc is None else acc + part
        acc_ref[...] += acc
        return carry

    lax.fori_loop(0, nw_ref[b * nt + j], window, 0)
    xo = x1_ref[...] + g2_ref[...] * acc_ref[...]
    o_ref[...] = _rmsnorm(xo, fg_ref[...])


def moe_combine(pos_tm, gate_tm, base, ye, x1, g2, final_g, cap, tt=512, win=128):
    b, s, ne = pos_tm.shape
    d = x1.shape[2]
    starts, nwin = _slot_windows(base, cap, tt, win)
    tok = lambda n: pl.BlockSpec((None, tt, n), lambda i, j, st, nw: (i, j, 0))
    return pl.pallas_call(
        functools.partial(_combine_kernel, win=win),
        grid_spec=pltpu.PrefetchScalarGridSpec(
            num_scalar_prefetch=2,
            grid=(b, s // tt),
            in_specs=[tok(ne), tok(ne),
                      pl.BlockSpec((ne, cap, d), lambda i, j, st, nw: (0, i, 0)),
                      tok(d),
                      pl.BlockSpec((None, 1, d), lambda i, j, st, nw: (i, 0, 0)),
                      pl.BlockSpec((1, d), lambda i, j, st, nw: (0, 0))],
            out_specs=tok(d),
            scratch_shapes=[pltpu.VMEM((tt, d), F32)]),
        out_shape=jax.ShapeDtypeStruct((b, s, d), F32),
        compiler_params=_cparams(("arbitrary", "arbitrary"), VMEM_LIMIT_BYTES),
        name="moe_combine",
    )(starts, nwin, pos_tm, gate_tm, ye, x1, g2, final_g.reshape(1, d))


def _layer(x, ctx, mods, norm1_g, norm2_g, w_in, w_out, conv_w, conv_b, filt, hy_bias, s5p,
           s5_c_re, s5_c_im, s5_d, s5_glu_w, s5_glu_b, router_w, ex_w_gate, ex_w_up, ex_w_down,
           final_g):
    b, s, d = x.shape
    n_order, hw = hy_bias.shape
    hy_cols = (n_order + 1) * hw
    sw = w_in.shape[1] - hy_cols
    rows = s // GRID_W
    ne = router_w.shape[1]
    cap = CAPACITY_FACTOR * s // ne

    per_b = lambda k: mods[:b, k * d:(k + 1) * d].reshape(b, 1, d)
    ctx_v = lambda k: jnp.broadcast_to(mods[b:b + 1, k * d:(k + 1) * d].reshape(1, 1, d), (b, 1, d))
    sh1, sc1, g1, sh2, sc2, g2 = [per_b(k) for k in range(N_MOD)]

    w_in_bf = w_in.astype(BF16)
    (u_ctx,) = inproj(ctx, norm1_g, ctx_v(0), ctx_v(1), w_in_bf[:, hy_cols:], 0)
    z_hy, u = inproj(x, norm1_g, sh1, sc1, w_in_bf, hy_cols, conv=(conv_w, conv_b, GRID_W))

    hraw = hyena_filter_mlp(s, *filt)
    kf = hyena_filter_spectra(hraw, n_order, hw)
    y1 = hyena_conv(z_hy, 0, z_hy, 1, kf[0], hy_bias[0])
    hy = hyena_conv(y1, 0, z_hy, 2, kf[1], hy_bias[1])

    s5w = s5_chunk_weights(*s5p, s5_c_re, s5_c_im, sw // LANES, S5_CHUNK)
    yf, yb = s5_scan(u_ctx, u, s5w, b)

    x1, h2, logits = mixer_tail(x, hy, yf, yb, u, s5_d, s5_glu_w, s5_glu_b, w_out, g1, norm2_g,
                                sh2, sc2, router_w)
    pos_em, pos_tm, gate_tm, base = route(logits, cap)
    xe = moe_gather(pos_em, h2, base, cap)
    ye = moe_ffn(xe, ex_w_gate, ex_w_up, ex_w_down)
    return moe_combine(pos_tm, gate_tm, base, ye, x1, g2, final_g, cap)


def kernel(x, c, ctx, c_ctx, mod_w, mod_b, norm1_g, norm2_g, w_in, w_out, conv_w, conv_b, hy_w1, hy_b1, hy_freq, hy_w2, hy_b2, hy_w3, hy_b3, hy_bias, s5_lam_re, s5_lam_im, s5_log_step, s5_b_re, s5_b_im, s5_c_re, s5_c_im, s5_d, s5_glu_w, s5_glu_b, router_w, ex_w_gate, ex_w_up, ex_w_down, final_g):
    depth = mod_w.shape[0]
    assert depth == 1, "context-token updates of non-final layers are not implemented"
    b, _, d = x.shape
    l = 0
    pad = (-(b + 1)) % SUBLANES
    cond = jnp.concatenate([c, c_ctx[None], jnp.zeros((pad, d), F32)], axis=0)
    mods = adaln_mods(cond, mod_w[l], mod_b[l])
    filt = (hy_w1[l], hy_b1[l], hy_freq[l], hy_w2[l], hy_b2[l], hy_w3[l], hy_b3[l])
    s5p = (s5_lam_re[l], s5_lam_im[l], s5_log_step[l], s5_b_re[l], s5_b_im[l])
    return _layer(x, ctx, mods, norm1_g[l], norm2_g[l], w_in[l], w_out[l], conv_w[l], conv_b[l],
                  filt, hy_bias[l], s5p, s5_c_re[l], s5_c_im[l], s5_d[l], s5_glu_w[l], s5_glu_b[l],
                  router_w[l], ex_w_gate[l], ex_w_up[l], ex_w_down[l], final_g)
```

```python
import functools
import math

import numpy as np
import jax
import jax.numpy as jnp
from jax import lax
from jax.experimental import pallas as pl
from jax.experimental.pallas import tpu as pltpu

F32 = jnp.float32
BF16 = jnp.bfloat16
I32 = jnp.int32
HIGHEST = lax.Precision.HIGHEST

SUBLANES = 8
LANES = 128
VMEM_LIMIT_BYTES = 58 * 1024 * 1024

GRID_W = 64
N_MOD = 6
NORM_EPS = 1e-6
POS_BANDS = 16
DECAY_FAST = 0.3
DECAY_SLOW = 1.5
DECAY_TARGET = 1e-2
CAPACITY_FACTOR = 2

DFT_N1 = 64
DFT_N2 = 128
DFT_K1 = DFT_N1 // 2 + 1
DFT_ROWS = 2 * DFT_K1

S5_CHUNK = 4


def _cparams(sem, vmem=None):
    return pltpu.CompilerParams(dimension_semantics=sem, vmem_limit_bytes=vmem)


def _silu(x):
    return x * (1.0 / (1.0 + jnp.exp(-x)))


def _rmsnorm(x, g):
    ms = jnp.mean(x * x, axis=-1, keepdims=True)
    return x * lax.rsqrt(ms + NORM_EPS) * g


def _adaln_kernel(c_ref, w_ref, b_ref, o_ref):
    s = _silu(c_ref[...])
    o_ref[...] = jnp.dot(s, w_ref[...], precision=HIGHEST, preferred_element_type=F32) + b_ref[...]


def adaln_mods(cond, mod_w, mod_b, tn=1536):
    rows, d = cond.shape
    n = mod_w.shape[1]
    return pl.pallas_call(
        _adaln_kernel,
        grid=(n // tn,),
        in_specs=[pl.BlockSpec((rows, d), lambda j: (0, 0)),
                  pl.BlockSpec((d, tn), lambda j: (0, j)),
                  pl.BlockSpec((1, tn), lambda j: (0, j))],
        out_specs=pl.BlockSpec((rows, tn), lambda j: (0, j)),
        out_shape=jax.ShapeDtypeStruct((rows, n), F32),
        compiler_params=_cparams(("arbitrary",)),
        name="adaln",
    )(cond, mod_w, mod_b.reshape(1, n))


def _short_conv(z, w_ref, b_ref, row_len):
    length = z.shape[0]
    pos = lax.broadcasted_iota(I32, (length, 1), 0) % row_len
    zm = jnp.where(pos == 0, 0.0, pltpu.roll(z, 1, 0))
    zp = jnp.where(pos == row_len - 1, 0.0, pltpu.roll(z, length - 1, 0))
    w = w_ref[...]
    return zm * w[0:1, :] + z * w[1:2, :] + zp * w[2:3, :] + b_ref[...]


@functools.lru_cache(maxsize=None)
def _perm_time_major(batch, tt):
    n = batch * tt
    p = np.zeros((n, n), np.float32)
    t, b = np.meshgrid(np.arange(tt), np.arange(batch), indexing='ij')
    p[(t * batch + b).ravel(), (b * tt + t).ravel()] = 1.0
    return jnp.asarray(p, dtype=BF16)


def _inproj_kernel(x_ref, g_ref, sh_ref, sc_ref, w_ref, perm_ref, *refs, n_tok, row_len):
    bsz, tt, d = x_ref.shape
    h = _rmsnorm(x_ref[...], g_ref[...])
    h = h * (1.0 + sc_ref[...]) + sh_ref[...]
    z = jnp.dot(h.reshape(bsz * tt, d).astype(BF16), w_ref[...], preferred_element_type=F32)
    if n_tok:
        cw_ref, cb_ref, tok_ref, tm_ref = refs
        tok_ref[...] = _short_conv(z[:, :n_tok], cw_ref, cb_ref, row_len).reshape(bsz, tt, n_tok)
    else:
        (tm_ref,) = refs
    u = z[:, n_tok:].astype(BF16)
    tm_ref[...] = jnp.dot(perm_ref[...], u, preferred_element_type=F32).astype(BF16)


def inproj(x, g, shift, scale, w_bf16, n_tok, conv=None, tt=64):
    b, s, d = x.shape
    n = w_bf16.shape[1]
    perm = _perm_time_major(b, tt)
    const = lambda a: pl.BlockSpec(a.shape, lambda j: (0,) * a.ndim)
    args = [x, g.reshape(1, d), shift, scale, w_bf16, perm]
    out_specs = [pl.BlockSpec((tt * b, n - n_tok), lambda j: (j, 0))]
    out_shape = [jax.ShapeDtypeStruct((s * b, n - n_tok), BF16)]
    row_len = 1
    if n_tok:
        cw, cb, row_len = conv
        assert tt % row_len == 0
        args += [cw, cb.reshape(1, -1)]
        out_specs.insert(0, pl.BlockSpec((b, tt, n_tok), lambda j: (0, j, 0)))
        out_shape.insert(0, jax.ShapeDtypeStruct((b, s, n_tok), F32))
    return pl.pallas_call(
        functools.partial(_inproj_kernel, n_tok=n_tok, row_len=row_len),
        grid=(s // tt,),
        in_specs=[pl.BlockSpec((b, tt, d), lambda j: (0, j, 0))] + [const(a) for a in args[1:]],
        out_specs=out_specs,
        out_shape=out_shape,
        compiler_params=_cparams(("arbitrary",)),
        name="inproj",
    )(*args)


def _filt_mlp_kernel(w1t_ref, w1c_ref, w1s_ref, b1_ref, fr_ref, w2_ref, b2_ref, w3_ref, b3_ref,
                     o_ref, *, length, tl):
    i0 = pl.program_id(0) * tl
    idx = (lax.broadcasted_iota(I32, (tl, 1), 0) + i0).astype(F32)
    t = idx / float(length - 1)
    omega = (2.0 * math.pi) * idx / float(length)
    fstep = ((POS_BANDS - 1) - 1e-4) / (POS_BANDS - 1)
    f = 1e-4 + lax.broadcasted_iota(I32, (1, POS_BANDS), 1).astype(F32) * fstep
    arg = omega * f
    pre = (t * w1t_ref[...]
           + jnp.dot(jnp.cos(arg), w1c_ref[...], precision=HIGHEST, preferred_element_type=F32)
           - jnp.dot(jnp.sin(arg), w1s_ref[...], precision=HIGHEST, preferred_element_type=F32)
           + b1_ref[...])
    fr = fr_ref[...]
    h = jnp.sin(fr[0:1, :] * pre)
    h = jnp.sin(fr[1:2, :] * (jnp.dot(h, w2_ref[...], precision=HIGHEST,
                                      preferred_element_type=F32) + b2_ref[...]))
    o_ref[...] = jnp.dot(h, w3_ref[...], precision=HIGHEST, preferred_element_type=F32) + b3_ref[...]


def hyena_filter_mlp(length, w1, b1, freq, w2, b2, w3, b3, tl=512):
    fw = w1.shape[1]
    n = w3.shape[1]
    full = lambda shape: pl.BlockSpec(shape, lambda i: (0, 0))
    return pl.pallas_call(
        functools.partial(_filt_mlp_kernel, length=length, tl=tl),
        grid=(length // tl,),
        in_specs=[full((1, fw)), full((POS_BANDS, fw)), full((POS_BANDS, fw)), full((1, fw)),
                  full((2, fw)), full((fw, fw)), full((1, fw)), full((fw, n)), full((1, n))],
        out_specs=pl.BlockSpec((tl, n), lambda i: (i, 0)),
        out_shape=jax.ShapeDtypeStruct((length, n), F32),
        compiler_params=_cparams(("arbitrary",)),
        name="hyena_filter_mlp",
    )(w1[0:1], w1[1:1 + POS_BANDS], w1[1 + POS_BANDS:], b1.reshape(1, fw), freq, w2,
      b2.reshape(1, fw), w3, b3.reshape(1, n))


@functools.lru_cache(maxsize=None)
def _dft_tables():
    n1n, n2n, k1n = DFT_N1, DFT_N2, DFT_K1
    n = n1n * n2n
    half = n1n // 2
    k1 = np.arange(k1n)[:, None]
    n1 = np.arange(half)[None, :]
    th = 2.0 * np.pi * k1 * n1 / n1n
    f1 = np.zeros((DFT_ROWS, half))
    f1[0::2] = np.cos(th)
    f1[1::2] = -np.sin(th)
    wgt = np.where((k1 == 0) | (k1 == half), 1.0, 2.0)
    g1 = np.zeros((half, DFT_ROWS))
    g1[:, 0::2] = (wgt * np.cos(th)).T / n
    g1[:, 1::2] = (-wgt * np.sin(th)).T / n
    eye = np.eye(SUBLANES)
    fk = np.kron(f1, eye)
    gk = np.kron(g1, eye)
    k2 = np.arange(n2n)[:, None]
    n2 = np.arange(n2n)[None, :]
    f3 = np.zeros((k1n, 2 * n2n, 2 * n2n))
    for kk in range(k1n):
        ph = 2.0 * np.pi * n2 * (n1n * k2 + kk) / n
        tr, ti = np.cos(ph), -np.sin(ph)
        f3[kk, :n2n, :n2n] = tr
        f3[kk, :n2n, n2n:] = -ti
        f3[kk, n2n:, :n2n] = ti
        f3[kk, n2n:, n2n:] = tr
    g3 = np.transpose(f3, (0, 2, 1))
    to = lambda a: jnp.asarray(a, dtype=F32).astype(BF16)
    return to(fk), to(gk), to(f3), to(g3)


def _dft_stage1(src_ref, a_ref, fk_ref):
    half = DFT_N1 // 2

    def body(m, carry):
        sub = pl.ds(pl.multiple_of(m * SUBLANES, SUBLANES), SUBLANES)
        rows = [src_ref.at[pl.ds(DFT_N2 * n1, DFT_N2)][sub, :] for n1 in range(half)]
        rhs = jnp.concatenate(rows, axis=0).astype(BF16)
        out = jnp.dot(fk_ref[...], rhs, preferred_element_type=F32)
        for j in range(DFT_ROWS):
            a_ref.at[pl.ds(DFT_N2 * j, DFT_N2)][sub, :] = out[SUBLANES * j:SUBLANES * (j + 1)]
        return carry

    lax.fori_loop(0, DFT_N2 // SUBLANES, body, 0, unroll=2)


def _dft_stage3(a_ref, f3_ref, k1):
    r0 = pl.multiple_of(k1 * (2 * DFT_N2), 2 * DFT_N2)
    a = a_ref[pl.ds(r0, 2 * DFT_N2), :].astype(BF16)
    x = jnp.dot(f3_ref[k1], a, preferred_element_type=F32)
    return x[:DFT_N2], x[DFT_N2:]


def _filt_spec_kernel(hf_ref, hb_ref, fk_ref, f3_ref, o_ref, src_ref, af_ref, ab_ref, *, length):
    c = hf_ref.shape[1]
    cb = pl.program_id(1)
    hw = pl.num_programs(1) * c
    row = lax.broadcasted_iota(I32, (length, 1), 0)
    t = row.astype(F32) / float(length - 1)
    ch = (lax.broadcasted_iota(I32, (1, c), 1) + cb * c).astype(F32)
    d0 = math.log(DECAY_TARGET) / DECAY_FAST
    d1 = math.log(DECAY_TARGET) / DECAY_SLOW
    deltas = jnp.abs(d0 + ch * ((d1 - d0) / float(hw - 1)))
    decay = jnp.exp(-t * deltas)
    fwd = hf_ref[...] * decay
    bwd = jnp.where(row == 0, 0.0, hb_ref[...] * decay)
    inv = 1.0 / (jnp.sum(jnp.abs(fwd), axis=0, keepdims=True)
                 + jnp.sum(jnp.abs(bwd), axis=0, keepdims=True))
    src_ref[...] = fwd
    _dft_stage1(src_ref, af_ref, fk_ref)
    src_ref[...] = bwd
    _dft_stage1(src_ref, ab_ref, fk_ref)

    def body(k1, carry):
        fr, fi = _dft_stage3(af_ref, f3_ref, k1)
        br, bi = _dft_stage3(ab_ref, f3_ref, k1)
        o_ref[k1, 0] = ((fr + br) * inv).astype(o_ref.dtype)
        o_ref[k1, 1] = ((fi - bi) * inv).astype(o_ref.dtype)
        return carry

    lax.fori_loop(0, DFT_K1, body, 0, unroll=3)


def hyena_filter_spectra(hraw, n_order, width, c_blk=256):
    length = hraw.shape[0]
    assert 2 * length == DFT_N1 * DFT_N2
    fk, _, f3, _ = _dft_tables()
    ncb = width // c_blk
    return pl.pallas_call(
        functools.partial(_filt_spec_kernel, length=length),
        grid=(n_order, ncb),
        in_specs=[pl.BlockSpec((length, c_blk), lambda o, j: (0, o * 2 * ncb + j)),
                  pl.BlockSpec((length, c_blk), lambda o, j: (0, o * 2 * ncb + ncb + j)),
                  pl.BlockSpec(fk.shape, lambda o, j: (0, 0)),
                  pl.BlockSpec(f3.shape, lambda o, j: (0, 0, 0))],
        out_specs=pl.BlockSpec((None, DFT_K1, 2, DFT_N2, c_blk), lambda o, j: (o, 0, 0, 0, j)),
        out_shape=jax.ShapeDtypeStruct((n_order, DFT_K1, 2, DFT_N2, width), BF16),
        scratch_shapes=[pltpu.VMEM((length, c_blk), F32),
                        pltpu.VMEM((DFT_ROWS * DFT_N2, c_blk), F32),
                        pltpu.VMEM((DFT_ROWS * DFT_N2, c_blk), F32)],
        compiler_params=_cparams(("arbitrary", "arbitrary"), VMEM_LIMIT_BYTES),
        name="hyena_filter_spectrum",
    )(hraw, hraw, fk, f3)


def _hyena_conv_kernel(s_ref, m_ref, kf_ref, bias_ref, fk_ref, f3_ref, g3_ref, gk_ref, o_ref, a_ref,
                       *, group):
    half = DFT_N1 // 2
    blk_rows = 2 * DFT_N2
    _dft_stage1(s_ref, a_ref, fk_ref)

    def body3(i, carry):
        k1s = [i * group + q for q in range(group)]
        r0s = [pl.multiple_of(k1 * blk_rows, blk_rows) for k1 in k1s]
        blocks = [a_ref[pl.ds(r0, blk_rows), :].astype(BF16) for r0 in r0s]
        outs = []
        for k1, a in zip(k1s, blocks):
            x = jnp.dot(f3_ref[k1], a, preferred_element_type=F32)
            xr, xi = x[:DFT_N2], x[DFT_N2:]
            kr = kf_ref[k1, 0].astype(F32)
            ki = kf_ref[k1, 1].astype(F32)
            y = jnp.concatenate([xr * kr - xi * ki, xr * ki + xi * kr], axis=0).astype(BF16)
            outs.append(jnp.dot(g3_ref[k1], y, preferred_element_type=F32))
        for r0, o in zip(r0s, outs):
            a_ref[pl.ds(r0, blk_rows), :] = o
        return carry

    lax.fori_loop(0, DFT_K1 // group, body3, 0)
    bias = bias_ref[...]

    def body1(m, carry):
        sub = pl.ds(pl.multiple_of(m * SUBLANES, SUBLANES), SUBLANES)
        blk = [a_ref.at[pl.ds(DFT_N2 * j, DFT_N2)][sub, :] for j in range(DFT_ROWS)]
        rhs = jnp.concatenate(blk, axis=0).astype(BF16)
        out = jnp.dot(gk_ref[...], rhs, preferred_element_type=F32)
        for n1 in range(half):
            blk_n1 = pl.ds(DFT_N2 * n1, DFT_N2)
            conv = out[SUBLANES * n1:SUBLANES * (n1 + 1)]
            o_ref.at[blk_n1][sub, :] = m_ref.at[blk_n1][sub, :] * (
                conv + s_ref.at[blk_n1][sub, :] * bias)
        return carry

    lax.fori_loop(0, DFT_N2 // SUBLANES, body1, 0, unroll=2)


def hyena_conv(sig, sig_col, mul, mul_col, kf, bias, c_blk=256, group=11):
    b, length, _ = sig.shape
    width = kf.shape[-1]
    ncb = width // c_blk
    assert DFT_K1 % group == 0
    fk, gk, f3, g3 = _dft_tables()
    once = pl.Buffered(1)
    const2 = lambda a: pl.BlockSpec(a.shape, lambda j, i: (0, 0), pipeline_mode=once)
    const3 = lambda a: pl.BlockSpec(a.shape, lambda j, i: (0, 0, 0), pipeline_mode=once)
    return pl.pallas_call(
        functools.partial(_hyena_conv_kernel, group=group),
        grid=(ncb, b),
        in_specs=[pl.BlockSpec((None, length, c_blk), lambda j, i: (i, 0, sig_col * ncb + j)),
                  pl.BlockSpec((None, length, c_blk), lambda j, i: (i, 0, mul_col * ncb + j)),
                  pl.BlockSpec((DFT_K1, 2, DFT_N2, c_blk), lambda j, i: (0, 0, 0, j),
                               pipeline_mode=once),
                  pl.BlockSpec((1, c_blk), lambda j, i: (0, j), pipeline_mode=once),
                  const2(fk), const3(f3), const3(g3), const2(gk)],
        out_specs=pl.BlockSpec((None, length, c_blk), lambda j, i: (i, 0, j)),
        out_shape=jax.ShapeDtypeStruct((b, length, width), F32),
        scratch_shapes=[pltpu.VMEM((DFT_ROWS * DFT_N2, c_blk), F32)],
        compiler_params=_cparams(("arbitrary", "arbitrary"), VMEM_LIMIT_BYTES),
        name="hyena_conv",
    )(sig, mul, kf, bias.reshape(1, width), fk, f3, g3, gk)


def _s5_powers_kernel(lr_ref, li_ref, dt_ref, vr_ref, vi_ref, or_ref, oi_ref, *, n_pow, zoh):
    lr, li, dt = lr_ref[...], li_ref[...], jnp.exp(dt_ref[...])
    mag = jnp.exp(lr * dt)
    ar = mag * jnp.cos(li * dt)
    ai = mag * jnp.sin(li * dt)
    vr, vi = vr_ref[...], vi_ref[...]
    if zoh:
        den = 1.0 / (lr * lr + li * li)
        qr = ((ar - 1.0) * lr + ai * li) * den
        qi = (ai * lr - (ar - 1.0) * li) * den
        vr, vi = qr * vr - qi * vi, qr * vi + qi * vr
    for j in range(n_pow):
        or_ref[j] = vr
        oi_ref[j] = vi
        vr, vi = ar * vr - ai * vi, ar * vi + ai * vr


def s5_powers(lam_re, lam_im, log_step, v_re, v_im, state_axis, n_pow, zoh):
    nd, g, a, b = v_re.shape
    expand = (lambda x: x[..., :, None]) if state_axis == 2 else (lambda x: x[..., None, :])
    rep = lambda x: jnp.broadcast_to(expand(x), v_re.shape).reshape(nd * g, a * b)
    dt = jnp.broadcast_to(log_step[:, :, None, None], v_re.shape).reshape(nd * g, a * b)
    flat = lambda x: x.reshape(nd * g, a * b)
    shp = jax.ShapeDtypeStruct((n_pow, nd * g, a * b), F32)
    o_r, o_i = pl.pallas_call(
        functools.partial(_s5_powers_kernel, n_pow=n_pow, zoh=zoh), out_shape=[shp, shp],
        name="s5_powers",
    )(rep(lam_re), rep(lam_im), dt, flat(v_re), flat(v_im))
    un = lambda x: x.reshape(n_pow, nd, g, a, b)
    return un(o_r), un(o_i)


def _s5_taps_kernel(cr_ref, ci_ref, br_ref, bi_ref, o_ref):
    n_pow, nk = br_ref.shape[0], br_ref.shape[1]
    for j in range(n_pow):
        for k in range(nk):
            o_ref[j, k] = (jnp.dot(cr_ref[k], br_ref[j, k], preferred_element_type=F32)
                           - jnp.dot(ci_ref[k], bi_ref[j, k], preferred_element_type=F32))


def s5_taps(c_re, c_im, bbp_r, bbp_i, nk):
    nd, g, h, p = c_re.shape
    n_pow = bbp_r.shape[0]
    gpk = g // nk
    eye = jnp.eye(gpk, dtype=F32)
    bdiag = lambda c: jnp.einsum('dkahp,ab->dkahbp', c.reshape(nd, nk, gpk, h, p), eye).reshape(
        nd, nk, gpk * h, gpk * p).astype(BF16)
    flat = lambda x: jnp.transpose(x, (1, 0, 2, 3, 4)).reshape(nd, n_pow, nk, gpk * p, h).astype(BF16)
    per_d = lambda a: pl.BlockSpec((None,) + a.shape[1:], lambda d: (d,) + (0,) * (a.ndim - 1))
    args = (bdiag(c_re), bdiag(c_im), flat(bbp_r), flat(bbp_i))
    shp = jax.ShapeDtypeStruct((nd, n_pow, nk, gpk * h, h), F32)
    out = pl.pallas_call(
        _s5_taps_kernel,
        grid=(nd,),
        in_specs=[per_d(a) for a in args],
        out_specs=per_d(shp),
        out_shape=shp,
        compiler_params=_cparams(("arbitrary",)),
        name="s5_taps",
    )(*args)
    return out.reshape(nd, n_pow, g, h, h)


def _s5_scan_kernel(ucf_ref, uf_ref, ucb_ref, ub_ref, qf_ref, qb_ref, pf_ref, pb_ref, mf_ref, mb_ref,
                    lam_ref, yf_ref, yb_ref, xf_ref, xb_ref, st_ref, *, nc, kpp, tlen):
    width = uf_ref.shape[1]
    batch = st_ref.shape[3]
    nk = xf_ref.shape[1]
    ck = width // nk
    rows = xf_ref.shape[2]
    cpg = rows // batch
    i = pl.program_id(0)
    is_ctx = i < nc

    @pl.when(i == 0)
    def _():
        st_ref[...] = jnp.zeros_like(st_ref)

    pick = lambda c_ref, l_ref: jnp.where(is_ctx, c_ref[...], l_ref[...]).astype(F32).reshape(
        cpg, tlen, batch, width)
    uf = pick(ucf_ref, uf_ref)
    ub = pick(ucb_ref, ub_ref)

    def scan(x_ref, d, reverse):
        for k0 in range(0, nk, kpp):
            ks = slice(k0, k0 + kpp)
            lr, li = lam_ref[d, 0, ks], lam_ref[d, 1, ks]

            def body(c, carry, ks=ks, lr=lr, li=li):
                sr, si = carry
                cc = (cpg - 1 - c) if reverse else c
                r = pl.ds(pl.multiple_of(cc * batch, batch), batch)
                qr, qi = x_ref[0, ks, r, :], x_ref[1, ks, r, :]
                x_ref[0, ks, r, :] = sr
                x_ref[1, ks, r, :] = si
                return lr * sr - li * si + qr, lr * si + li * sr + qi

            st_ref[d, 0, ks], st_ref[d, 1, ks] = lax.fori_loop(
                0, cpg, body, (st_ref[d, 0, ks], st_ref[d, 1, ks]))

    for u, q_ref, p_ref, m_ref, x_ref, y_ref, d in ((uf, qf_ref, pf_ref, mf_ref, xf_ref, yf_ref, 0),
                                                    (ub, qb_ref, pb_ref, mb_ref, xb_ref, yb_ref, 1)):
        uks = []
        for k in range(nk):
            uk = jnp.concatenate([u[:, t, :, k * ck:(k + 1) * ck].reshape(rows, ck)
                                  for t in range(tlen)], axis=1).astype(BF16)
            uks.append(uk)
            for ri in range(2):
                x_ref[ri, k] = jnp.dot(uk, q_ref[ri, k], preferred_element_type=F32)
        scan(x_ref, d, d == 1)
        for k in range(nk):
            yk = (jnp.dot(uks[k], m_ref[k], preferred_element_type=F32)
                  + jnp.dot(x_ref[0, k].astype(BF16), p_ref[0, k], preferred_element_type=F32)
                  + jnp.dot(x_ref[1, k].astype(BF16), p_ref[1, k], preferred_element_type=F32))
            y_ref[:, k * ck:(k + 1) * ck] = jnp.stack(
                [yk[:, t * ck:(t + 1) * ck].reshape(cpg, batch, ck) for t in range(tlen)],
                axis=1).reshape(cpg * tlen * batch, ck)


def s5_chunk_weights(lam_re, lam_im, log_step, b_re, b_im, c_re, c_im, nk, tlen):
    nd, g, p, h = b_re.shape
    gpk, sk = g // nk, g * p // nk
    bb_r, bb_i = s5_powers(lam_re, lam_im, log_step, b_re, b_im, 2, tlen, True)
    cl_r, cl_i = s5_powers(lam_re, lam_im, log_step, c_re, c_im, 3, tlen + 1, False)
    ones = jnp.ones((nd, g, p, 2), F32)
    pw_r, pw_i = s5_powers(lam_re, lam_im, log_step, ones, jnp.zeros_like(ones), 2, tlen + 1, False)
    taps = s5_taps(c_re, c_im, bb_r, bb_i, nk)
    dirs = np.arange(nd)[:, None]
    step = np.arange(tlen)[None, :]
    jq = np.where(dirs == 0, tlen - 1 - step, step)
    jp = np.where(dirs == 0, step + 1, tlen - step)
    lag = step[0][None, None, :] - step[0][None, :, None]
    lag = np.where(dirs[:, :, None] == 0, lag, -lag)
    ch = tlen * gpk * h
    rep_p = np.tile(np.eye(p, dtype=np.float32), (1, gpk))
    rep_h = np.einsum('tu,hi,b->thubi', np.eye(tlen), np.eye(h), np.ones(gpk)).reshape(
        tlen * h, ch).astype(np.float32)
    grp_ch = np.tile(np.repeat(np.arange(gpk), h), tlen)
    grp_st = np.repeat(np.arange(gpk), p)
    qsel = jnp.stack([bb_r, bb_i])[:, jq, dirs].reshape(2, nd, tlen, nk, gpk, p, h)
    qrows = jnp.transpose(qsel, (1, 0, 3, 2, 4, 6, 5)).reshape(nd, 2, nk, ch, p)
    q = jnp.where(grp_ch[:, None] == grp_st[None, :], qrows @ rep_p, 0.0)
    psel = jnp.stack([cl_r, -cl_i])[:, jp, dirs].reshape(2, nd, tlen, nk, gpk, h, p)
    prows = jnp.transpose(psel, (1, 0, 3, 4, 6, 2, 5)).reshape(nd, 2, nk, sk, tlen * h)
    pm = jnp.where(grp_st[:, None] == grp_ch[None, :], prows @ rep_h, 0.0)
    tsel = jnp.where((lag >= 0)[..., None, None, None], taps[dirs[:, :, None], np.maximum(lag, 0)], 0.0)
    tsel = tsel.reshape(nd, tlen, tlen, nk, gpk, h, h)
    trows = jnp.transpose(tsel, (0, 3, 1, 4, 6, 2, 5)).reshape(nd, nk, ch, tlen * h)
    m = jnp.where(grp_ch[:, None] == grp_ch[None, :], trows @ rep_h, 0.0)
    decay = jnp.stack([pw_r[tlen, ..., 0], pw_i[tlen, ..., 0]], axis=1).reshape(nd, 2, nk, 1, sk)
    return q.astype(BF16), pm.astype(BF16), m.astype(BF16), decay


def s5_scan(uc, u, weights, batch, steps=64, kpp=2):
    q, pm, m, decay = weights
    width = u.shape[1]
    nk, sk = q.shape[2], q.shape[4]
    tlen = m.shape[2] * nk // width
    cpg = steps // tlen
    rpc = steps * batch
    nc, nl = uc.shape[0] // rpc, u.shape[0] // rpc
    assert steps % tlen == 0 and uc.shape[0] % rpc == 0 and u.shape[0] % rpc == 0
    lam = jnp.broadcast_to(decay, (2, 2, nk, batch, sk))
    blk = lambda f: pl.BlockSpec((rpc, width), lambda i: (f(i), 0))
    lat_f = lambda i: jnp.maximum(i - nc, 0)
    lat_b = lambda i: jnp.clip(nl - 1 - i + nc, 0, nl - 1)
    once = pl.Buffered(1)
    par = lambda a, d: pl.BlockSpec((None,) + a.shape[1:], lambda i: (d,) + (0,) * (a.ndim - 1),
                                    pipeline_mode=once)
    shp = jax.ShapeDtypeStruct(u.shape, F32)
    return pl.pallas_call(
        functools.partial(_s5_scan_kernel, nc=nc, kpp=kpp, tlen=tlen),
        grid=(nc + nl,),
        in_specs=[blk(lambda i: jnp.minimum(i, nc - 1)), blk(lat_f),
                  blk(lambda i: jnp.maximum(nc - 1 - i, 0)), blk(lat_b),
                  par(q, 0), par(q, 1), par(pm, 0), par(pm, 1), par(m, 0), par(m, 1),
                  pl.BlockSpec(lam.shape, lambda i: (0, 0, 0, 0, 0), pipeline_mode=once)],
        out_specs=[blk(lat_f), blk(lat_b)],
        out_shape=[shp, shp],
        scratch_shapes=[pltpu.VMEM((2, nk, cpg * batch, sk), F32),
                        pltpu.VMEM((2, nk, cpg * batch, sk), F32),
                        pltpu.VMEM((2, 2, nk, batch, sk), F32)],
        compiler_params=_cparams(("arbitrary",), VMEM_LIMIT_BYTES),
        name="s5_scan",
    )(uc, u, uc, u, q, q, pm, pm, m, m, lam)


def _mixer_tail_kernel(x_ref, hy_ref, yf_ref, yb_ref, u_ref, d_ref, gw_ref, gb_ref, woh_ref, wos_ref,
                       g1_ref, n2_ref, sh2_ref, sc2_ref, rwt_ref, perm_ref, x1_ref, h2_ref, lg_ref):
    bsz, tt, d = x_ref.shape
    rows = bsz * tt
    y = yf_ref[...] + yb_ref[...] + d_ref[...] * u_ref[...].astype(F32)
    y = 0.5 * y * (1.0 + jnp.tanh(math.sqrt(2.0 / math.pi) * (y + 0.044715 * (y * y * y))))
    gate = jnp.dot(y.astype(BF16), gw_ref[...], preferred_element_type=F32) + gb_ref[...]
    s5 = (y * (1.0 / (1.0 + jnp.exp(-gate)))).astype(BF16)
    s5 = jnp.dot(perm_ref[...], s5, preferred_element_type=F32).astype(BF16)
    hy = hy_ref[...].reshape(rows, hy_ref.shape[2]).astype(BF16)
    mix = (jnp.dot(hy, woh_ref[...], preferred_element_type=F32)
           + jnp.dot(s5, wos_ref[...], preferred_element_type=F32))
    x1 = x_ref[...] + g1_ref[...] * mix.reshape(bsz, tt, d)
    x1_ref[...] = x1
    h2 = _rmsnorm(x1, n2_ref[...]) * (1.0 + sc2_ref[...]) + sh2_ref[...]
    h2_ref[...] = h2.astype(BF16)
    lg = lax.dot_general(rwt_ref[...], h2.reshape(rows, d), (((1,), (1,)), ((), ())),
                         precision=HIGHEST, preferred_element_type=F32)
    for i in range(bsz):
        lg_ref[i] = lg[:, i * tt:(i + 1) * tt]


def mixer_tail(x, hy, yf, yb, u_tm, s5_d, glu_w, glu_b, w_out, g1, norm2_g, sh2, sc2, router_w,
               tt=128):
    b, s, d = x.shape
    hw = hy.shape[2]
    sw = u_tm.shape[1]
    ne = router_w.shape[1]
    perm = _perm_time_major(b, tt).T
    once = pl.Buffered(1)
    tok = lambda n: pl.BlockSpec((b, tt, n), lambda j: (0, j, 0))
    tmj = pl.BlockSpec((tt * b, sw), lambda j: (j, 0))
    const = lambda a: pl.BlockSpec(a.shape, lambda j: (0,) * a.ndim, pipeline_mode=once)
    consts = [s5_d.reshape(1, sw), glu_w.astype(BF16), glu_b.reshape(1, sw), w_out[:hw].astype(BF16),
              w_out[hw:].astype(BF16), g1, norm2_g.reshape(1, d), sh2, sc2, router_w.T, perm]
    return pl.pallas_call(
        _mixer_tail_kernel,
        grid=(s // tt,),
        in_specs=[tok(d), tok(hw), tmj, tmj, tmj] + [const(a) for a in consts],
        out_specs=[tok(d), tok(d), pl.BlockSpec((b, ne, tt), lambda j: (0, 0, j))],
        out_shape=[jax.ShapeDtypeStruct((b, s, d), F32), jax.ShapeDtypeStruct((b, s, d), BF16),
                   jax.ShapeDtypeStruct((b, ne, s), F32)],
        compiler_params=_cparams(("arbitrary",), VMEM_LIMIT_BYTES),
        name="mixer_tail",
    )(x, hy, yf, yb, u_tm, *consts)


def _lane_cumsum_exclusive(x):
    rows, s = x.shape
    ii = lax.broadcasted_iota(I32, (LANES, LANES), 0)
    jj = lax.broadcasted_iota(I32, (LANES, LANES), 1)
    tri = jnp.where(ii < jj, 1.0, 0.0).astype(BF16)
    carry = jnp.zeros((rows, 1), F32)
    out, base = [], []
    for blk in range(s // LANES):
        xb = x[:, blk * LANES:(blk + 1) * LANES]
        out.append(jnp.dot(xb.astype(BF16), tri, preferred_element_type=F32) + carry)
        base.append(carry)
        carry = carry + jnp.sum(xb, axis=1, keepdims=True)
    return jnp.concatenate(out, axis=1), jnp.concatenate(base, axis=1)


def _route_kernel(lg_ref, pos_em_ref, pos_tm_ref, gate_tm_ref, base_ref, aff_ref, *, cap):
    lg = lg_ref[...]
    ne, s = lg.shape
    ex = jnp.exp(lg - jnp.max(lg, axis=0, keepdims=True))
    aff_ref[...] = ex / jnp.sum(ex, axis=0, keepdims=True)
    aff = aff_ref[...]
    count_ge = lambda v, t: jnp.sum(jnp.where(v >= t, 1.0, 0.0), axis=1, keepdims=True)

    def coarse(i, tb):
        cand = tb | jnp.left_shift(jnp.int32(1), 30 - i)
        return jnp.where(count_ge(aff, pltpu.bitcast(cand, F32)) >= cap, cand, tb)

    tb = lax.fori_loop(0, 31, coarse, jnp.zeros((ne, 1), I32))
    t_hi = pltpu.bitcast(tb, F32)
    ulp = pltpu.bitcast(tb + 1, F32) - t_hi
    resid = aff - t_hi

    def fine(j, carry):
        c, step = carry
        cand = c + step
        return jnp.where(count_ge(resid, cand) >= cap, cand, c), step * 0.5

    t_lo, _ = lax.fori_loop(0, 24, fine, (jnp.zeros((ne, 1), F32), ulp * 0.5))
    gt = resid > t_lo
    eq = resid == t_lo
    need = cap - jnp.sum(jnp.where(gt, 1.0, 0.0), axis=1, keepdims=True)
    eq_rank, _ = _lane_cumsum_exclusive(jnp.where(eq, 1.0, 0.0))
    sel = gt | (eq & (eq_rank < need))
    pos, base = _lane_cumsum_exclusive(jnp.where(sel, 1.0, 0.0))
    posf = jnp.where(sel, pos + 1.0, 0.0)
    gate = jnp.where(sel, aff, 0.0)
    pos_em_ref[...] = posf.astype(I32) - 1
    base_ref[...] = base.astype(I32)
    hi = jnp.floor(posf * (1.0 / 16.0))
    lo = posf - 16.0 * hi
    ii = lax.broadcasted_iota(I32, (LANES, LANES), 0)
    jj = lax.broadcasted_iota(I32, (LANES, LANES), 1)
    eye = jnp.where(ii == jj, 1.0, 0.0)
    nt = (((1,), (1,)), ((), ()))
    for blk in range(s // LANES):
        sl = slice(blk * LANES, (blk + 1) * LANES)
        t_hi = lax.dot_general(eye.astype(BF16), hi[:, sl].astype(BF16), nt, preferred_element_type=F32)
        t_lo = lax.dot_general(eye.astype(BF16), lo[:, sl].astype(BF16), nt, preferred_element_type=F32)
        pos_tm_ref[sl, :] = (16.0 * t_hi + t_lo).astype(I32) - 1
        gate_tm_ref[sl, :] = lax.dot_general(eye, gate[:, sl], nt, precision=HIGHEST,
                                             preferred_element_type=F32)


def route(logits, cap):
    b, ne, s = logits.shape
    nb = s // LANES
    return pl.pallas_call(
        functools.partial(_route_kernel, cap=cap),
        grid=(b,),
        in_specs=[pl.BlockSpec((None, ne, s), lambda i: (i, 0, 0))],
        out_specs=[pl.BlockSpec((None, ne, s), lambda i: (i, 0, 0)),
                   pl.BlockSpec((None, s, ne), lambda i: (i, 0, 0)),
                   pl.BlockSpec((None, s, ne), lambda i: (i, 0, 0)),
                   pl.BlockSpec((None, ne, nb), lambda i: (i, 0, 0))],
        out_shape=[jax.ShapeDtypeStruct((b, ne, s), I32), jax.ShapeDtypeStruct((b, s, ne), I32),
                   jax.ShapeDtypeStruct((b, s, ne), F32), jax.ShapeDtypeStruct((b, ne, nb), I32)],
        scratch_shapes=[pltpu.VMEM((ne, s), F32)],
        compiler_params=_cparams(("arbitrary",)),
        name="route",
    )(logits)


SLOT_ALIGN = 16


def _slot_windows(base, cap, chunk, win):
    lo = base[:, :, ::chunk // LANES]
    hi = jnp.concatenate([lo[:, :, 1:], jnp.full_like(lo[:, :, :1], cap)], axis=2)
    start = (lo // SLOT_ALIGN) * SLOT_ALIGN
    nwin = jnp.max((hi - start + win - 1) // win, axis=1)
    return jnp.transpose(start, (0, 2, 1)).reshape(-1), nwin.reshape(-1)


def _window(st_ref, idx, w, win, cap):
    first = st_ref[idx] + w * win
    return first, pl.multiple_of(jnp.minimum(first, cap - win), SLOT_ALIGN)


def _gather_kernel(st_ref, nw_ref, pos_ref, h_ref, o_ref, *, win):
    b, j, nch = pl.program_id(0), pl.program_id(1), pl.num_programs(1)
    ne, cap, _ = o_ref.shape
    tk = h_ref.shape[0]

    @pl.when(j == 0)
    def _():
        o_ref[...] = jnp.zeros_like(o_ref)

    pos = pos_ref[...]
    h = h_ref[...]
    row = lax.broadcasted_iota(I32, (win, tk), 0)

    def window(w, carry):
        starts, lhs = [], []
        for e in range(ne):
            first, start = _window(st_ref, (b * nch + j) * ne + e, w, win, cap)
            slot = row + start
            hit = (pos[e:e + 1, :] == slot) & (slot >= first)
            lhs.append(jnp.where(hit, 1.0, 0.0).astype(BF16))
            starts.append(start)
        res = jnp.dot(jnp.concatenate(lhs, axis=0), h, preferred_element_type=F32)
        for e, start in enumerate(starts):
            o_ref[e, pl.ds(start, win), :] += res[e * win:(e + 1) * win].astype(o_ref.dtype)
        return carry

    lax.fori_loop(0, nw_ref[b * nch + j], window, 0)


def moe_gather(pos_em, h2, base, cap, tk=256, win=64):
    b, ne, s = pos_em.shape
    d = h2.shape[2]
    starts, nwin = _slot_windows(base, cap, tk, win)
    return pl.pallas_call(
        functools.partial(_gather_kernel, win=win),
        grid_spec=pltpu.PrefetchScalarGridSpec(
            num_scalar_prefetch=2,
            grid=(b, s // tk),
            in_specs=[pl.BlockSpec((None, ne, tk), lambda i, j, st, nw: (i, 0, j)),
                      pl.BlockSpec((None, tk, d), lambda i, j, st, nw: (i, j, 0))],
            out_specs=pl.BlockSpec((ne, cap, d), lambda i, j, st, nw: (0, i, 0))),
        out_shape=jax.ShapeDtypeStruct((ne, b * cap, d), BF16),
        compiler_params=_cparams(("arbitrary", "arbitrary"), VMEM_LIMIT_BYTES),
        name="moe_gather",
    )(starts, nwin, pos_em, h2)


def _ffn_kernel(x_ref, wg_ref, wu_ref, wd_ref, o_ref, acc_ref, *, sub):
    f = pl.program_id(2)

    @pl.when(f == 0)
    def _():
        acc_ref[...] = jnp.zeros_like(acc_ref)

    wg = wg_ref[...].astype(BF16)
    wu = wu_ref[...].astype(BF16)
    wd = wd_ref[...].astype(BF16)
    for r in range(x_ref.shape[0] // sub):
        rows = pl.ds(r * sub, sub)
        x = x_ref[rows, :]
        g = jnp.dot(x, wg, preferred_element_type=F32)
        u = jnp.dot(x, wu, preferred_element_type=F32)
        h = (_silu(g) * u).astype(BF16)
        acc_ref[rows, :] += jnp.dot(h, wd, preferred_element_type=F32)

    @pl.when(f == pl.num_programs(2) - 1)
    def _():
        o_ref[...] = acc_ref[...].astype(o_ref.dtype)


def moe_ffn(xe, w_gate, w_up, w_down, tm=2048, tf=256, sub=512):
    ne, m, d = xe.shape
    ff = w_gate.shape[2]
    tm = min(tm, m)
    return pl.pallas_call(
        functools.partial(_ffn_kernel, sub=min(sub, tm)),
        grid=(ne, m // tm, ff // tf),
        in_specs=[pl.BlockSpec((None, tm, d), lambda e, i, f: (e, i, 0)),
                  pl.BlockSpec((None, d, tf), lambda e, i, f: (e, 0, f)),
                  pl.BlockSpec((None, d, tf), lambda e, i, f: (e, 0, f)),
                  pl.BlockSpec((None, tf, d), lambda e, i, f: (e, f, 0))],
        out_specs=pl.BlockSpec((None, tm, d), lambda e, i, f: (e, i, 0)),
        out_shape=jax.ShapeDtypeStruct((ne, m, d), BF16),
        scratch_shapes=[pltpu.VMEM((tm, d), F32)],
        compiler_params=_cparams(("arbitrary", "arbitrary", "arbitrary"), VMEM_LIMIT_BYTES),
        name="moe_ffn",
    )(xe, w_gate, w_up, w_down)


def _combine_kernel(st_ref, nw_ref, pos_ref, gate_ref, ye_ref, x1_ref, g2_ref, fg_ref, o_ref, acc_ref,
                    *, win):
    b, j, nt = pl.program_id(0), pl.program_id(1), pl.num_programs(1)
    tt, ne = pos_ref.shape
    cap = ye_ref.shape[1]
    acc_ref[...] = jnp.zeros_like(acc_ref)
    pos = pos_ref[...]
    gate = gate_ref[...]
    col = lax.broadcasted_iota(I32, (tt, win), 1)

    def window(w, carry):
        acc = None
        for p in range(ne // 2):
            lhs, rhs = [], []
            for e in (2 * p, 2 * p + 1):
                first, start = _window(st_ref, (b * nt + j) * ne + e, w, win, cap)
                slot = col + start
                hit = (pos[:, e:e + 1] == slot) & (slot >= first)
                lhs.append(jnp.where(hit, gate[:, e:e + 1], 0.0).astype(BF16))
                rhs.append(ye_ref[e, pl.ds(start, win), :])
            part = jnp.dot(jnp.concatenate(lhs, axis=1), jnp.concatenate(rhs, axis=0),
                           preferred_element_type=F32)
            acc = part if acc is None else acc + part
        acc_ref[...] += acc
        return carry

    lax.fori_loop(0, nw_ref[b * nt + j], window, 0)
    xo = x1_ref[...] + g2_ref[...] * acc_ref[...]
    o_ref[...] = _rmsnorm(xo, fg_ref[...])


def moe_combine(pos_tm, gate_tm, base, ye, x1, g2, final_g, cap, tt=512, win=128):
    b, s, ne = pos_tm.shape
    d = x1.shape[2]
    starts, nwin = _slot_windows(base, cap, tt, win)
    tok = lambda n: pl.BlockSpec((None, tt, n), lambda i, j, st, nw: (i, j, 0))
    return pl.pallas_call(
        functools.partial(_combine_kernel, win=win),
        grid_spec=pltpu.PrefetchScalarGridSpec(
            num_scalar_prefetch=2,
            grid=(b, s // tt),
            in_specs=[tok(ne), tok(ne),
                      pl.BlockSpec((ne, cap, d), lambda i, j, st, nw: (0, i, 0)),
                      tok(d),
                      pl.BlockSpec((None, 1, d), lambda i, j, st, nw: (i, 0, 0)),
                      pl.BlockSpec((1, d), lambda i, j, st, nw: (0, 0))],
            out_specs=tok(d),
            scratch_shapes=[pltpu.VMEM((tt, d), F32)]),
        out_shape=jax.ShapeDtypeStruct((b, s, d), F32),
        compiler_params=_cparams(("arbitrary", "arbitrary"), VMEM_LIMIT_BYTES),
        name="moe_combine",
    )(starts, nwin, pos_tm, gate_tm, ye, x1, g2, final_g.reshape(1, d))


def _layer(x, ctx, mods, norm1_g, norm2_g, w_in, w_out, conv_w, conv_b, filt, hy_bias, s5p,
           s5_c_re, s5_c_im, s5_d, s5_glu_w, s5_glu_b, router_w, ex_w_gate, ex_w_up, ex_w_down,
           final_g):
    b, s, d = x.shape
    n_order, hw = hy_bias.shape
    hy_cols = (n_order + 1) * hw
    sw = w_in.shape[1] - hy_cols
    rows = s // GRID_W
    ne = router_w.shape[1]
    cap = CAPACITY_FACTOR * s // ne

    per_b = lambda k: mods[:b, k * d:(k + 1) * d].reshape(b, 1, d)
    ctx_v = lambda k: jnp.broadcast_to(mods[b:b + 1, k * d:(k + 1) * d].reshape(1, 1, d), (b, 1, d))
    sh1, sc1, g1, sh2, sc2, g2 = [per_b(k) for k in range(N_MOD)]

    w_in_bf = w_in.astype(BF16)
    (u_ctx,) = inproj(ctx, norm1_g, ctx_v(0), ctx_v(1), w_in_bf[:, hy_cols:], 0)
    z_hy, u = inproj(x, norm1_g, sh1, sc1, w_in_bf, hy_cols, conv=(conv_w, conv_b, GRID_W))

    hraw = hyena_filter_mlp(s, *filt)
    kf = hyena_filter_spectra(hraw, n_order, hw)
    y1 = hyena_conv(z_hy, 0, z_hy, 1, kf[0], hy_bias[0])
    hy = hyena_conv(y1, 0, z_hy, 2, kf[1], hy_bias[1])

    s5w = s5_chunk_weights(*s5p, s5_c_re, s5_c_im, sw // LANES, S5_CHUNK)
    yf, yb = s5_scan(u_ctx, u, s5w, b)

    x1, h2, logits = mixer_tail(x, hy, yf, yb, u, s5_d, s5_glu_w, s5_glu_b, w_out, g1, norm2_g,
                                sh2, sc2, router_w)
    pos_em, pos_tm, gate_tm, base = route(logits, cap)
    xe = moe_gather(pos_em, h2, base, cap)
    ye = moe_ffn(xe, ex_w_gate, ex_w_up, ex_w_down)
    return moe_combine(pos_tm, gate_tm, base, ye, x1, g2, final_g, cap)


def kernel(x, c, ctx, c_ctx, mod_w, mod_b, norm1_g, norm2_g, w_in, w_out, conv_w, conv_b, hy_w1, hy_b1, hy_freq, hy_w2, hy_b2, hy_w3, hy_b3, hy_bias, s5_lam_re, s5_lam_im, s5_log_step, s5_b_re, s5_b_im, s5_c_re, s5_c_im, s5_d, s5_glu_w, s5_glu_b, router_w, ex_w_gate, ex_w_up, ex_w_down, final_g):
    depth = mod_w.shape[0]
    assert depth == 1, "context-token updates of non-final layers are not implemented"
    b, _, d = x.shape
    l = 0
    pad = (-(b + 1)) % SUBLANES
    cond = jnp.concatenate([c, c_ctx[None], jnp.zeros((pad, d), F32)], axis=0)
    mods = adaln_mods(cond, mod_w[l], mod_b[l])
    filt = (hy_w1[l], hy_b1[l], hy_freq[l], hy_w2[l], hy_b2[l], hy_w3[l], hy_b3[l])
    s5p = (s5_lam_re[l], s5_lam_im[l], s5_log_step[l], s5_b_re[l], s5_b_im[l])
    return _layer(x, ctx, mods, norm1_g[l], norm2_g[l], w_in[l], w_out[l], conv_w[l], conv_b[l],
                  filt, hy_bias[l], s5p, s5_c_re[l], s5_c_im[l], s5_d[l], s5_glu_w[l], s5_glu_b[l],
                  router_w[l], ex_w_gate[l], ex_w_up[l], ex_w_down[l], final_g)
```

```python
import functools
import math

import numpy as np
import jax
import jax.numpy as jnp
from jax import lax
from jax.experimental import pallas as pl
from jax.experimental.pallas import tpu as pltpu

F32 = jnp.float32
BF16 = jnp.bfloat16
I32 = jnp.int32
HIGHEST = lax.Precision.HIGHEST

SUBLANES = 8
LANES = 128
VMEM_LIMIT_BYTES = 58 * 1024 * 1024

GRID_W = 64
N_MOD = 6
NORM_EPS = 1e-6
POS_BANDS = 16
DECAY_FAST = 0.3
DECAY_SLOW = 1.5
DECAY_TARGET = 1e-2
CAPACITY_FACTOR = 2

DFT_N1 = 64
DFT_N2 = 128
DFT_K1 = DFT_N1 // 2 + 1
DFT_ROWS = 2 * DFT_K1

S5_CHUNK = 4


def _cparams(sem, vmem=None):
    return pltpu.CompilerParams(dimension_semantics=sem, vmem_limit_bytes=vmem)


def _silu(x):
    return x * (1.0 / (1.0 + jnp.exp(-x)))


def _rmsnorm(x, g):
    ms = jnp.mean(x * x, axis=-1, keepdims=True)
    return x * lax.rsqrt(ms + NORM_EPS) * g


def _adaln_kernel(c_ref, w_ref, b_ref, o_ref):
    s = _silu(c_ref[...])
    o_ref[...] = jnp.dot(s, w_ref[...], precision=HIGHEST, preferred_element_type=F32) + b_ref[...]


def adaln_mods(cond, mod_w, mod_b, tn=1536):
    rows, d = cond.shape
    n = mod_w.shape[1]
    return pl.pallas_call(
        _adaln_kernel,
        grid=(n // tn,),
        in_specs=[pl.BlockSpec((rows, d), lambda j: (0, 0)),
                  pl.BlockSpec((d, tn), lambda j: (0, j)),
                  pl.BlockSpec((1, tn), lambda j: (0, j))],
        out_specs=pl.BlockSpec((rows, tn), lambda j: (0, j)),
        out_shape=jax.ShapeDtypeStruct((rows, n), F32),
        compiler_params=_cparams(("arbitrary",)),
        name="adaln",
    )(cond, mod_w, mod_b.reshape(1, n))


def _short_conv(z, w_ref, b_ref, row_len):
    length = z.shape[0]
    pos = lax.broadcasted_iota(I32, (length, 1), 0) % row_len
    zm = jnp.where(pos == 0, 0.0, pltpu.roll(z, 1, 0))
    zp = jnp.where(pos == row_len - 1, 0.0, pltpu.roll(z, length - 1, 0))
    w = w_ref[...]
    return zm * w[0:1, :] + z * w[1:2, :] + zp * w[2:3, :] + b_ref[...]


@functools.lru_cache(maxsize=None)
def _perm_time_major(batch, tt):
    n = batch * tt
    p = np.zeros((n, n), np.float32)
    t, b = np.meshgrid(np.arange(tt), np.arange(batch), indexing='ij')
    p[(t * batch + b).ravel(), (b * tt + t).ravel()] = 1.0
    return jnp.asarray(p, dtype=BF16)


def _inproj_kernel(x_ref, g_ref, sh_ref, sc_ref, w_ref, perm_ref, *refs, n_tok, row_len):
    bsz, tt, d = x_ref.shape
    h = _rmsnorm(x_ref[...], g_ref[...])
    h = h * (1.0 + sc_ref[...]) + sh_ref[...]
    z = jnp.dot(h.reshape(bsz * tt, d).astype(BF16), w_ref[...], preferred_element_type=F32)
    if n_tok:
        cw_ref, cb_ref, tok_ref, tm_ref = refs
        tok_ref[...] = _short_conv(z[:, :n_tok], cw_ref, cb_ref, row_len).reshape(bsz, tt, n_tok)
    else:
        (tm_ref,) = refs
    u = z[:, n_tok:].astype(BF16)
    tm_ref[...] = jnp.dot(perm_ref[...], u, preferred_element_type=F32).astype(BF16)


def inproj(x, g, shift, scale, w_bf16, n_tok, conv=None, w_cols=None, tt=64):
    b, s, d = x.shape
    col, n = (0, w_bf16.shape[1]) if w_cols is None else w_cols
    perm = _perm_time_major(b, tt)
    const = lambda a: (pl.BlockSpec((d, n), lambda j: (0, col)) if a is w_bf16
                       else pl.BlockSpec(a.shape, lambda j: (0,) * a.ndim))
    args = [x, g.reshape(1, d), shift, scale, w_bf16, perm]
    out_specs = [pl.BlockSpec((tt * b, n - n_tok), lambda j: (j, 0))]
    out_shape = [jax.ShapeDtypeStruct((s * b, n - n_tok), BF16)]
    row_len = 1
    if n_tok:
        cw, cb, row_len = conv
        assert tt % row_len == 0
        args += [cw, cb.reshape(1, -1)]
        out_specs.insert(0, pl.BlockSpec((b, tt, n_tok), lambda j: (0, j, 0)))
        out_shape.insert(0, jax.ShapeDtypeStruct((b, s, n_tok), F32))
    return pl.pallas_call(
        functools.partial(_inproj_kernel, n_tok=n_tok, row_len=row_len),
        grid=(s // tt,),
        in_specs=[pl.BlockSpec((b, tt, d), lambda j: (0, j, 0))] + [const(a) for a in args[1:]],
        out_specs=out_specs,
        out_shape=out_shape,
        compiler_params=_cparams(("arbitrary",)),
        name="inproj",
    )(*args)


def _filt_mlp_kernel(w1t_ref, w1c_ref, w1s_ref, b1_ref, fr_ref, w2_ref, b2_ref, w3_ref, b3_ref,
                     o_ref, *, length, tl):
    i0 = pl.program_id(0) * tl
    idx = (lax.broadcasted_iota(I32, (tl, 1), 0) + i0).astype(F32)
    t = idx / float(length - 1)
    omega = (2.0 * math.pi) * idx / float(length)
    fstep = ((POS_BANDS - 1) - 1e-4) / (POS_BANDS - 1)
    f = 1e-4 + lax.broadcasted_iota(I32, (1, POS_BANDS), 1).astype(F32) * fstep
    arg = omega * f
    pre = (t * w1t_ref[...]
           + jnp.dot(jnp.cos(arg), w1c_ref[...], precision=HIGHEST, preferred_element_type=F32)
           - jnp.dot(jnp.sin(arg), w1s_ref[...], precision=HIGHEST, preferred_element_type=F32)
           + b1_ref[...])
    fr = fr_ref[...]
    h = jnp.sin(fr[0:1, :] * pre)
    h = jnp.sin(fr[1:2, :] * (jnp.dot(h, w2_ref[...], precision=HIGHEST,
                                      preferred_element_type=F32) + b2_ref[...]))
    o_ref[...] = jnp.dot(h, w3_ref[...], precision=HIGHEST, preferred_element_type=F32) + b3_ref[...]


def hyena_filter_mlp(length, w1, b1, freq, w2, b2, w3, b3, tl=512):
    fw = w1.shape[1]
    n = w3.shape[1]
    full = lambda shape: pl.BlockSpec(shape, lambda i: (0, 0))
    return pl.pallas_call(
        functools.partial(_filt_mlp_kernel, length=length, tl=tl),
        grid=(length // tl,),
        in_specs=[full((1, fw)), full((POS_BANDS, fw)), full((POS_BANDS, fw)), full((1, fw)),
                  full((2, fw)), full((fw, fw)), full((1, fw)), full((fw, n)), full((1, n))],
        out_specs=pl.BlockSpec((tl, n), lambda i: (i, 0)),
        out_shape=jax.ShapeDtypeStruct((length, n), F32),
        compiler_params=_cparams(("arbitrary",)),
        name="hyena_filter_mlp",
    )(w1[0:1], w1[1:1 + POS_BANDS], w1[1 + POS_BANDS:], b1.reshape(1, fw), freq, w2,
      b2.reshape(1, fw), w3, b3.reshape(1, n))


@functools.lru_cache(maxsize=None)
def _dft_tables():
    n1n, n2n, k1n = DFT_N1, DFT_N2, DFT_K1
    n = n1n * n2n
    half = n1n // 2
    k1 = np.arange(k1n)[:, None]
    n1 = np.arange(half)[None, :]
    th = 2.0 * np.pi * k1 * n1 / n1n
    f1 = np.zeros((DFT_ROWS, half))
    f1[0::2] = np.cos(th)
    f1[1::2] = -np.sin(th)
    wgt = np.where((k1 == 0) | (k1 == half), 1.0, 2.0)
    g1 = np.zeros((half, DFT_ROWS))
    g1[:, 0::2] = (wgt * np.cos(th)).T / n
    g1[:, 1::2] = (-wgt * np.sin(th)).T / n
    eye = np.eye(SUBLANES)
    fk = np.kron(f1, eye)
    gk = np.kron(g1, eye)
    k2 = np.arange(n2n)[:, None]
    n2 = np.arange(n2n)[None, :]
    f3 = np.zeros((k1n, 2 * n2n, 2 * n2n))
    for kk in range(k1n):
        ph = 2.0 * np.pi * n2 * (n1n * k2 + kk) / n
        tr, ti = np.cos(ph), -np.sin(ph)
        f3[kk, :n2n, :n2n] = tr
        f3[kk, :n2n, n2n:] = -ti
        f3[kk, n2n:, :n2n] = ti
        f3[kk, n2n:, n2n:] = tr
    g3 = np.transpose(f3, (0, 2, 1))
    to = lambda a: jnp.asarray(a, dtype=F32).astype(BF16)
    return to(fk), to(gk), to(f3), to(g3)


def _dft_stage1(src_ref, a_ref, fk_ref):
    half = DFT_N1 // 2

    def body(m, carry):
        sub = pl.ds(pl.multiple_of(m * SUBLANES, SUBLANES), SUBLANES)
        rows = [src_ref.at[pl.ds(DFT_N2 * n1, DFT_N2)][sub, :] for n1 in range(half)]
        rhs = jnp.concatenate(rows, axis=0).astype(BF16)
        out = jnp.dot(fk_ref[...], rhs, preferred_element_type=F32)
        for j in range(DFT_ROWS):
            a_ref.at[pl.ds(DFT_N2 * j, DFT_N2)][sub, :] = out[SUBLANES * j:SUBLANES * (j + 1)]
        return carry

    lax.fori_loop(0, DFT_N2 // SUBLANES, body, 0, unroll=2)


def _dft_stage3(a_ref, f3_ref, k1):
    r0 = pl.multiple_of(k1 * (2 * DFT_N2), 2 * DFT_N2)
    a = a_ref[pl.ds(r0, 2 * DFT_N2), :].astype(BF16)
    x = jnp.dot(f3_ref[k1], a, preferred_element_type=F32)
    return x[:DFT_N2], x[DFT_N2:]


def _filt_spec_kernel(hf_ref, hb_ref, fk_ref, f3_ref, o_ref, src_ref, af_ref, ab_ref, *, length):
    c = hf_ref.shape[1]
    cb = pl.program_id(1)
    hw = pl.num_programs(1) * c
    row = lax.broadcasted_iota(I32, (length, 1), 0)
    t = row.astype(F32) / float(length - 1)
    ch = (lax.broadcasted_iota(I32, (1, c), 1) + cb * c).astype(F32)
    d0 = math.log(DECAY_TARGET) / DECAY_FAST
    d1 = math.log(DECAY_TARGET) / DECAY_SLOW
    deltas = jnp.abs(d0 + ch * ((d1 - d0) / float(hw - 1)))
    decay = jnp.exp(-t * deltas)
    fwd = hf_ref[...] * decay
    bwd = jnp.where(row == 0, 0.0, hb_ref[...] * decay)
    inv = 1.0 / (jnp.sum(jnp.abs(fwd), axis=0, keepdims=True)
                 + jnp.sum(jnp.abs(bwd), axis=0, keepdims=True))
    src_ref[...] = fwd
    _dft_stage1(src_ref, af_ref, fk_ref)
    src_ref[...] = bwd
    _dft_stage1(src_ref, ab_ref, fk_ref)

    def body(k1, carry):
        fr, fi = _dft_stage3(af_ref, f3_ref, k1)
        br, bi = _dft_stage3(ab_ref, f3_ref, k1)
        o_ref[k1, 0] = ((fr + br) * inv).astype(o_ref.dtype)
        o_ref[k1, 1] = ((fi - bi) * inv).astype(o_ref.dtype)
        return carry

    lax.fori_loop(0, DFT_K1, body, 0, unroll=3)


def hyena_filter_spectra(hraw, n_order, width, c_blk=256):
    length = hraw.shape[0]
    assert 2 * length == DFT_N1 * DFT_N2
    fk, _, f3, _ = _dft_tables()
    ncb = width // c_blk
    return pl.pallas_call(
        functools.partial(_filt_spec_kernel, length=length),
        grid=(n_order, ncb),
        in_specs=[pl.BlockSpec((length, c_blk), lambda o, j: (0, o * 2 * ncb + j)),
                  pl.BlockSpec((length, c_blk), lambda o, j: (0, o * 2 * ncb + ncb + j)),
                  pl.BlockSpec(fk.shape, lambda o, j: (0, 0)),
                  pl.BlockSpec(f3.shape, lambda o, j: (0, 0, 0))],
        out_specs=pl.BlockSpec((None, DFT_K1, 2, DFT_N2, c_blk), lambda o, j: (o, 0, 0, 0, j)),
        out_shape=jax.ShapeDtypeStruct((n_order, DFT_K1, 2, DFT_N2, width), BF16),
        scratch_shapes=[pltpu.VMEM((length, c_blk), F32),
                        pltpu.VMEM((DFT_ROWS * DFT_N2, c_blk), F32),
                        pltpu.VMEM((DFT_ROWS * DFT_N2, c_blk), F32)],
        compiler_params=_cparams(("arbitrary", "arbitrary"), VMEM_LIMIT_BYTES),
        name="hyena_filter_spectrum",
    )(hraw, hraw, fk, f3)


def _hyena_conv_kernel(s_ref, m_ref, kf_ref, bias_ref, fk_ref, f3_ref, g3_ref, gk_ref, o_ref, a_ref,
                       *, group):
    half = DFT_N1 // 2
    blk_rows = 2 * DFT_N2
    _dft_stage1(s_ref, a_ref, fk_ref)

    def body3(i, carry):
        k1s = [i * group + q for q in range(group)]
        r0s = [pl.multiple_of(k1 * blk_rows, blk_rows) for k1 in k1s]
        blocks = [a_ref[pl.ds(r0, blk_rows), :].astype(BF16) for r0 in r0s]
        outs = []
        for k1, a in zip(k1s, blocks):
            x = jnp.dot(f3_ref[k1], a, preferred_element_type=F32)
            xr, xi = x[:DFT_N2], x[DFT_N2:]
            kr = kf_ref[k1, 0].astype(F32)
            ki = kf_ref[k1, 1].astype(F32)
            y = jnp.concatenate([xr * kr - xi * ki, xr * ki + xi * kr], axis=0).astype(BF16)
            outs.append(jnp.dot(g3_ref[k1], y, preferred_element_type=F32))
        for r0, o in zip(r0s, outs):
            a_ref[pl.ds(r0, blk_rows), :] = o
        return carry

    lax.fori_loop(0, DFT_K1 // group, body3, 0)
    bias = bias_ref[...]

    def body1(m, carry):
        sub = pl.ds(pl.multiple_of(m * SUBLANES, SUBLANES), SUBLANES)
        blk = [a_ref.at[pl.ds(DFT_N2 * j, DFT_N2)][sub, :] for j in range(DFT_ROWS)]
        rhs = jnp.concatenate(blk, axis=0).astype(BF16)
        out = jnp.dot(gk_ref[...], rhs, preferred_element_type=F32)
        for n1 in range(half):
            blk_n1 = pl.ds(DFT_N2 * n1, DFT_N2)
            conv = out[SUBLANES * n1:SUBLANES * (n1 + 1)]
            o_ref.at[blk_n1][sub, :] = m_ref.at[blk_n1][sub, :] * (
                conv + s_ref.at[blk_n1][sub, :] * bias)
        return carry

    lax.fori_loop(0, DFT_N2 // SUBLANES, body1, 0, unroll=2)


def hyena_conv(sig, sig_col, mul, mul_col, kf, bias, order, c_blk=256, group=11):
    b, length, _ = sig.shape
    width = kf.shape[-1]
    ncb = width // c_blk
    assert DFT_K1 % group == 0
    fk, gk, f3, g3 = _dft_tables()
    once = pl.Buffered(1)
    const2 = lambda a: pl.BlockSpec(a.shape, lambda j, i: (0, 0), pipeline_mode=once)
    const3 = lambda a: pl.BlockSpec(a.shape, lambda j, i: (0, 0, 0), pipeline_mode=once)
    return pl.pallas_call(
        functools.partial(_hyena_conv_kernel, group=group),
        grid=(ncb, b),
        in_specs=[pl.BlockSpec((None, length, c_blk), lambda j, i: (i, 0, sig_col * ncb + j)),
                  pl.BlockSpec((None, length, c_blk), lambda j, i: (i, 0, mul_col * ncb + j)),
                  pl.BlockSpec((None, DFT_K1, 2, DFT_N2, c_blk), lambda j, i: (order, 0, 0, 0, j),
                               pipeline_mode=once),
                  pl.BlockSpec((None, 1, c_blk), lambda j, i: (order, 0, j), pipeline_mode=once),
                  const2(fk), const3(f3), const3(g3), const2(gk)],
        out_specs=pl.BlockSpec((None, length, c_blk), lambda j, i: (i, 0, j)),
        out_shape=jax.ShapeDtypeStruct((b, length, width), F32),
        scratch_shapes=[pltpu.VMEM((DFT_ROWS * DFT_N2, c_blk), F32)],
        compiler_params=_cparams(("arbitrary", "arbitrary"), VMEM_LIMIT_BYTES),
        name="hyena_conv",
    )(sig, mul, kf, bias.reshape(-1, 1, width), fk, f3, g3, gk)


def _s5_powers_kernel(lr_ref, li_ref, dt_ref, vr_ref, vi_ref, or_ref, oi_ref, *, n_pow, zoh):
    lr, li, dt = lr_ref[...], li_ref[...], jnp.exp(dt_ref[...])
    mag = jnp.exp(lr * dt)
    ar = mag * jnp.cos(li * dt)
    ai = mag * jnp.sin(li * dt)
    vr, vi = vr_ref[...], vi_ref[...]
    if zoh:
        den = 1.0 / (lr * lr + li * li)
        qr = ((ar - 1.0) * lr + ai * li) * den
        qi = (ai * lr - (ar - 1.0) * li) * den
        vr, vi = qr * vr - qi * vi, qr * vi + qi * vr
    for j in range(n_pow):
        or_ref[j] = vr
        oi_ref[j] = vi
        vr, vi = ar * vr - ai * vi, ar * vi + ai * vr


def s5_powers(lam_re, lam_im, log_step, v_re, v_im, state_axis, n_pow, zoh):
    nd, g, a, b = v_re.shape
    expand = (lambda x: x[..., :, None]) if state_axis == 2 else (lambda x: x[..., None, :])
    rep = lambda x: jnp.broadcast_to(expand(x), v_re.shape).reshape(nd * g, a * b)
    dt = jnp.broadcast_to(log_step[:, :, None, None], v_re.shape).reshape(nd * g, a * b)
    flat = lambda x: x.reshape(nd * g, a * b)
    shp = jax.ShapeDtypeStruct((n_pow, nd * g, a * b), F32)
    o_r, o_i = pl.pallas_call(
        functools.partial(_s5_powers_kernel, n_pow=n_pow, zoh=zoh), out_shape=[shp, shp],
        name="s5_powers",
    )(rep(lam_re), rep(lam_im), dt, flat(v_re), flat(v_im))
    un = lambda x: x.reshape(n_pow, nd, g, a, b)
    return un(o_r), un(o_i)


def _s5_taps_kernel(cr_ref, ci_ref, br_ref, bi_ref, o_ref):
    n_pow, nk = br_ref.shape[0], br_ref.shape[1]
    for j in range(n_pow):
        for k in range(nk):
            o_ref[j, k] = (jnp.dot(cr_ref[k], br_ref[j, k], preferred_element_type=F32)
                           - jnp.dot(ci_ref[k], bi_ref[j, k], preferred_element_type=F32))


def s5_taps(c_re, c_im, bbp_r, bbp_i, nk):
    nd, g, h, p = c_re.shape
    n_pow = bbp_r.shape[0]
    gpk = g // nk
    eye = jnp.eye(gpk, dtype=F32)
    bdiag = lambda c: jnp.einsum('dkahp,ab->dkahbp', c.reshape(nd, nk, gpk, h, p), eye).reshape(
        nd, nk, gpk * h, gpk * p).astype(BF16)
    flat = lambda x: jnp.transpose(x, (1, 0, 2, 3, 4)).reshape(nd, n_pow, nk, gpk * p, h).astype(BF16)
    per_d = lambda a: pl.BlockSpec((None,) + a.shape[1:], lambda d: (d,) + (0,) * (a.ndim - 1))
    args = (bdiag(c_re), bdiag(c_im), flat(bbp_r), flat(bbp_i))
    shp = jax.ShapeDtypeStruct((nd, n_pow, nk, gpk * h, h), F32)
    out = pl.pallas_call(
        _s5_taps_kernel,
        grid=(nd,),
        in_specs=[per_d(a) for a in args],
        out_specs=per_d(shp),
        out_shape=shp,
        compiler_params=_cparams(("arbitrary",)),
        name="s5_taps",
    )(*args)
    return out.reshape(nd, n_pow, g, h, h)


def _s5_scan_kernel(ucf_ref, uf_ref, ucb_ref, ub_ref, qf_ref, qb_ref, pf_ref, pb_ref, mf_ref, mb_ref,
                    lam_ref, yf_ref, yb_ref, xf_ref, xb_ref, st_ref, *, nc, kpp, tlen):
    width = uf_ref.shape[1]
    batch = st_ref.shape[3]
    nk = xf_ref.shape[1]
    ck = width // nk
    rows = xf_ref.shape[2]
    cpg = rows // batch
    i = pl.program_id(0)
    is_ctx = i < nc

    @pl.when(i == 0)
    def _():
        st_ref[...] = jnp.zeros_like(st_ref)

    pick = lambda c_ref, l_ref: jnp.where(is_ctx, c_ref[...], l_ref[...]).astype(F32).reshape(
        cpg, tlen, batch, width)
    uf = pick(ucf_ref, uf_ref)
    ub = pick(ucb_ref, ub_ref)

    def scan(x_ref, d, reverse):
        for k0 in range(0, nk, kpp):
            ks = slice(k0, k0 + kpp)
            lr, li = lam_ref[d, 0, ks], lam_ref[d, 1, ks]

            def body(c, carry, ks=ks, lr=lr, li=li):
                sr, si = carry
                cc = (cpg - 1 - c) if reverse else c
                r = pl.ds(pl.multiple_of(cc * batch, batch), batch)
                qr, qi = x_ref[0, ks, r, :], x_ref[1, ks, r, :]
                x_ref[0, ks, r, :] = sr
                x_ref[1, ks, r, :] = si
                return lr * sr - li * si + qr, lr * si + li * sr + qi

            st_ref[d, 0, ks], st_ref[d, 1, ks] = lax.fori_loop(
                0, cpg, body, (st_ref[d, 0, ks], st_ref[d, 1, ks]))

    for u, q_ref, p_ref, m_ref, x_ref, y_ref, d in ((uf, qf_ref, pf_ref, mf_ref, xf_ref, yf_ref, 0),
                                                    (ub, qb_ref, pb_ref, mb_ref, xb_ref, yb_ref, 1)):
        uks = []
        for k in range(nk):
            uk = jnp.concatenate([u[:, t, :, k * ck:(k + 1) * ck].reshape(rows, ck)
                                  for t in range(tlen)], axis=1).astype(BF16)
            uks.append(uk)
            for ri in range(2):
                x_ref[ri, k] = jnp.dot(uk, q_ref[ri, k], preferred_element_type=F32)
        scan(x_ref, d, d == 1)
        for k in range(nk):
            yk = (jnp.dot(uks[k], m_ref[k], preferred_element_type=F32)
                  + jnp.dot(x_ref[0, k].astype(BF16), p_ref[0, k], preferred_element_type=F32)
                  + jnp.dot(x_ref[1, k].astype(BF16), p_ref[1, k], preferred_element_type=F32))
            y_ref[:, k * ck:(k + 1) * ck] = jnp.stack(
                [yk[:, t * ck:(t + 1) * ck].reshape(cpg, batch, ck) for t in range(tlen)],
                axis=1).reshape(cpg * tlen * batch, ck)


def s5_chunk_weights(lam_re, lam_im, log_step, b_re, b_im, c_re, c_im, nk, tlen):
    nd, g, p, h = b_re.shape
    gpk, sk = g // nk, g * p // nk
    bb_r, bb_i = s5_powers(lam_re, lam_im, log_step, b_re, b_im, 2, tlen, True)
    cl_r, cl_i = s5_powers(lam_re, lam_im, log_step, c_re, c_im, 3, tlen + 1, False)
    ones = jnp.ones((nd, g, p, 2), F32)
    pw_r, pw_i = s5_powers(lam_re, lam_im, log_step, ones, jnp.zeros_like(ones), 2, tlen + 1, False)
    taps = s5_taps(c_re, c_im, bb_r, bb_i, nk)
    dirs = np.arange(nd)[:, None]
    step = np.arange(tlen)[None, :]
    jq = np.where(dirs == 0, tlen - 1 - step, step)
    jp = np.where(dirs == 0, step + 1, tlen - step)
    lag = step[0][None, None, :] - step[0][None, :, None]
    lag = np.where(dirs[:, :, None] == 0, lag, -lag)
    ch = tlen * gpk * h
    rep_p = np.tile(np.eye(p, dtype=np.float32), (1, gpk))
    rep_h = np.einsum('tu,hi,b->thubi', np.eye(tlen), np.eye(h), np.ones(gpk)).reshape(
        tlen * h, ch).astype(np.float32)
    grp_ch = np.tile(np.repeat(np.arange(gpk), h), tlen)
    grp_st = np.repeat(np.arange(gpk), p)
    qsel = jnp.stack([bb_r, bb_i])[:, jq, dirs].reshape(2, nd, tlen, nk, gpk, p, h)
    qrows = jnp.transpose(qsel, (1, 0, 3, 2, 4, 6, 5)).reshape(nd, 2, nk, ch, p)
    q = jnp.where(grp_ch[:, None] == grp_st[None, :], qrows @ rep_p, 0.0)
    psel = jnp.stack([cl_r, -cl_i])[:, jp, dirs].reshape(2, nd, tlen, nk, gpk, h, p)
    prows = jnp.transpose(psel, (1, 0, 3, 4, 6, 2, 5)).reshape(nd, 2, nk, sk, tlen * h)
    pm = jnp.where(grp_st[:, None] == grp_ch[None, :], prows @ rep_h, 0.0)
    tsel = jnp.where((lag >= 0)[..., None, None, None], taps[dirs[:, :, None], np.maximum(lag, 0)], 0.0)
    tsel = tsel.reshape(nd, tlen, tlen, nk, gpk, h, h)
    trows = jnp.transpose(tsel, (0, 3, 1, 4, 6, 2, 5)).reshape(nd, nk, ch, tlen * h)
    m = jnp.where(grp_ch[:, None] == grp_ch[None, :], trows @ rep_h, 0.0)
    decay = jnp.stack([pw_r[tlen, ..., 0], pw_i[tlen, ..., 0]], axis=1).reshape(nd, 2, nk, 1, sk)
    return q.astype(BF16), pm.astype(BF16), m.astype(BF16), decay


def s5_scan(uc, u, weights, batch, steps=64, kpp=2):
    q, pm, m, decay = weights
    width = u.shape[1]
    nk, sk = q.shape[2], q.shape[4]
    tlen = m.shape[2] * nk // width
    cpg = steps // tlen
    rpc = steps * batch
    nc, nl = uc.shape[0] // rpc, u.shape[0] // rpc
    assert steps % tlen == 0 and uc.shape[0] % rpc == 0 and u.shape[0] % rpc == 0
    lam = jnp.broadcast_to(decay, (2, 2, nk, batch, sk))
    blk = lambda f: pl.BlockSpec((rpc, width), lambda i: (f(i), 0))
    lat_f = lambda i: jnp.maximum(i - nc, 0)
    lat_b = lambda i: jnp.clip(nl - 1 - i + nc, 0, nl - 1)
    once = pl.Buffered(1)
    par = lambda a, d: pl.BlockSpec((None,) + a.shape[1:], lambda i: (d,) + (0,) * (a.ndim - 1),
                                    pipeline_mode=once)
    shp = jax.ShapeDtypeStruct(u.shape, F32)
    return pl.pallas_call(
        functools.partial(_s5_scan_kernel, nc=nc, kpp=kpp, tlen=tlen),
        grid=(nc + nl,),
        in_specs=[blk(lambda i: jnp.minimum(i, nc - 1)), blk(lat_f),
                  blk(lambda i: jnp.maximum(nc - 1 - i, 0)), blk(lat_b),
                  par(q, 0), par(q, 1), par(pm, 0), par(pm, 1), par(m, 0), par(m, 1),
                  pl.BlockSpec(lam.shape, lambda i: (0, 0, 0, 0, 0), pipeline_mode=once)],
        out_specs=[blk(lat_f), blk(lat_b)],
        out_shape=[shp, shp],
        scratch_shapes=[pltpu.VMEM((2, nk, cpg * batch, sk), F32),
                        pltpu.VMEM((2, nk, cpg * batch, sk), F32),
                        pltpu.VMEM((2, 2, nk, batch, sk), F32)],
        compiler_params=_cparams(("arbitrary",), VMEM_LIMIT_BYTES),
        name="s5_scan",
    )(uc, u, uc, u, q, q, pm, pm, m, m, lam)


def _mixer_tail_kernel(x_ref, hy_ref, yf_ref, yb_ref, u_ref, d_ref, gw_ref, gb_ref, woh_ref, wos_ref,
                       g1_ref, n2_ref, sh2_ref, sc2_ref, rwt_ref, perm_ref, x1_ref, h2_ref, lg_ref):
    bsz, tt, d = x_ref.shape
    rows = bsz * tt
    y = yf_ref[...] + yb_ref[...] + d_ref[...] * u_ref[...].astype(F32)
    y = 0.5 * y * (1.0 + jnp.tanh(math.sqrt(2.0 / math.pi) * (y + 0.044715 * (y * y * y))))
    gate = jnp.dot(y.astype(BF16), gw_ref[...], preferred_element_type=F32) + gb_ref[...]
    s5 = (y * (1.0 / (1.0 + jnp.exp(-gate)))).astype(BF16)
    s5 = jnp.dot(perm_ref[...], s5, preferred_element_type=F32).astype(BF16)
    hy = hy_ref[...].reshape(rows, hy_ref.shape[2]).astype(BF16)
    mix = (jnp.dot(hy, woh_ref[...], preferred_element_type=F32)
           + jnp.dot(s5, wos_ref[...], preferred_element_type=F32))
    x1 = x_ref[...] + g1_ref[...] * mix.reshape(bsz, tt, d)
    x1_ref[...] = x1
    h2 = _rmsnorm(x1, n2_ref[...]) * (1.0 + sc2_ref[...]) + sh2_ref[...]
    h2_hi = h2.astype(BF16)
    h2_ref[...] = h2_hi
    rw = rwt_ref[...]
    rw_hi = rw.astype(BF16)
    rw_lo = (rw - rw_hi.astype(F32)).astype(BF16)
    h2_hi = h2_hi.reshape(rows, d)
    h2_lo = (h2.reshape(rows, d) - h2_hi.astype(F32)).astype(BF16)
    nt = lambda a, b: lax.dot_general(a, b, (((1,), (1,)), ((), ())), preferred_element_type=F32)
    lg = nt(rw_hi, h2_hi) + (nt(rw_hi, h2_lo) + nt(rw_lo, h2_hi))
    for i in range(bsz):
        lg_ref[i] = lg[:, i * tt:(i + 1) * tt]


def mixer_tail(x, hy, yf, yb, u_tm, s5_d, glu_w, glu_b, w_out, g1, norm2_g, sh2, sc2, router_w,
               tt=128):
    b, s, d = x.shape
    hw = hy.shape[2]
    sw = u_tm.shape[1]
    ne = router_w.shape[1]
    perm = _perm_time_major(b, tt).T
    once = pl.Buffered(1)
    tok = lambda n: pl.BlockSpec((b, tt, n), lambda j: (0, j, 0))
    tmj = pl.BlockSpec((tt * b, sw), lambda j: (j, 0))
    const = lambda a: pl.BlockSpec(a.shape, lambda j: (0,) * a.ndim, pipeline_mode=once)
    consts = [s5_d.reshape(1, sw), glu_w.astype(BF16), glu_b.reshape(1, sw), w_out[:hw].astype(BF16),
              w_out[hw:].astype(BF16), g1, norm2_g.reshape(1, d), sh2, sc2, router_w.T, perm]
    return pl.pallas_call(
        _mixer_tail_kernel,
        grid=(s // tt,),
        in_specs=[tok(d), tok(hw), tmj, tmj, tmj] + [const(a) for a in consts],
        out_specs=[tok(d), tok(d), pl.BlockSpec((b, ne, tt), lambda j: (0, 0, j))],
        out_shape=[jax.ShapeDtypeStruct((b, s, d), F32), jax.ShapeDtypeStruct((b, s, d), BF16),
                   jax.ShapeDtypeStruct((b, ne, s), F32)],
        compiler_params=_cparams(("arbitrary",), VMEM_LIMIT_BYTES),
        name="mixer_tail",
    )(x, hy, yf, yb, u_tm, *consts)


def _lane_cumsum_exclusive(x):
    rows, s = x.shape
    ii = lax.broadcasted_iota(I32, (LANES, LANES), 0)
    jj = lax.broadcasted_iota(I32, (LANES, LANES), 1)
    tri = jnp.where(ii < jj, 1.0, 0.0).astype(BF16)
    carry = jnp.zeros((rows, 1), F32)
    out, base = [], []
    for blk in range(s // LANES):
        xb = x[:, blk * LANES:(blk + 1) * LANES]
        out.append(jnp.dot(xb.astype(BF16), tri, preferred_element_type=F32) + carry)
        base.append(carry)
        carry = carry + jnp.sum(xb, axis=1, keepdims=True)
    return jnp.concatenate(out, axis=1), jnp.concatenate(base, axis=1)


def _route_kernel(lg_ref, pos_em_ref, pos_tm_ref, gate_tm_ref, base_ref, aff_ref, *, cap):
    lg = lg_ref[...]
    ne, s = lg.shape
    ex = jnp.exp(lg - jnp.max(lg, axis=0, keepdims=True))
    aff_ref[...] = ex / jnp.sum(ex, axis=0, keepdims=True)
    aff = aff_ref[...]
    count_ge = lambda v, t: jnp.sum(jnp.where(v >= t, 1.0, 0.0), axis=1, keepdims=True)

    def coarse(i, tb):
        cand = tb | jnp.left_shift(jnp.int32(1), 30 - i)
        return jnp.where(count_ge(aff, pltpu.bitcast(cand, F32)) >= cap, cand, tb)

    tb = lax.fori_loop(0, 31, coarse, jnp.zeros((ne, 1), I32))
    t_hi = pltpu.bitcast(tb, F32)
    ulp = pltpu.bitcast(tb + 1, F32) - t_hi
    resid = aff - t_hi

    def fine(j, carry):
        c, step = carry
        cand = c + step
        return jnp.where(count_ge(resid, cand) >= cap, cand, c), step * 0.5

    t_lo, _ = lax.fori_loop(0, 24, fine, (jnp.zeros((ne, 1), F32), ulp * 0.5))
    gt = resid > t_lo
    eq = resid == t_lo
    need = cap - jnp.sum(jnp.where(gt, 1.0, 0.0), axis=1, keepdims=True)
    eq_rank, _ = _lane_cumsum_exclusive(jnp.where(eq, 1.0, 0.0))
    sel = gt | (eq & (eq_rank < need))
    pos, base = _lane_cumsum_exclusive(jnp.where(sel, 1.0, 0.0))
    posf = jnp.where(sel, pos + 1.0, 0.0)
    gate = jnp.where(sel, aff, 0.0)
    pos_em_ref[...] = posf.astype(I32) - 1
    base_ref[...] = base.astype(I32)
    hi = jnp.floor(posf * (1.0 / 16.0))
    lo = posf - 16.0 * hi
    ii = lax.broadcasted_iota(I32, (LANES, LANES), 0)
    jj = lax.broadcasted_iota(I32, (LANES, LANES), 1)
    eye = jnp.where(ii == jj, 1.0, 0.0)
    nt = (((1,), (1,)), ((), ()))
    for blk in range(s // LANES):
        sl = slice(blk * LANES, (blk + 1) * LANES)
        t_hi = lax.dot_general(eye.astype(BF16), hi[:, sl].astype(BF16), nt, preferred_element_type=F32)
        t_lo = lax.dot_general(eye.astype(BF16), lo[:, sl].astype(BF16), nt, preferred_element_type=F32)
        pos_tm_ref[sl, :] = (16.0 * t_hi + t_lo).astype(I32) - 1
        gate_tm_ref[sl, :] = lax.dot_general(eye, gate[:, sl], nt, precision=HIGHEST,
                                             preferred_element_type=F32)


def route(logits, cap):
    b, ne, s = logits.shape
    nb = s // LANES
    return pl.pallas_call(
        functools.partial(_route_kernel, cap=cap),
        grid=(b,),
        in_specs=[pl.BlockSpec((None, ne, s), lambda i: (i, 0, 0))],
        out_specs=[pl.BlockSpec((None, ne, s), lambda i: (i, 0, 0)),
                   pl.BlockSpec((None, s, ne), lambda i: (i, 0, 0)),
                   pl.BlockSpec((None, s, ne), lambda i: (i, 0, 0)),
                   pl.BlockSpec((None, ne, nb), lambda i: (i, 0, 0))],
        out_shape=[jax.ShapeDtypeStruct((b, ne, s), I32), jax.ShapeDtypeStruct((b, s, ne), I32),
                   jax.ShapeDtypeStruct((b, s, ne), F32), jax.ShapeDtypeStruct((b, ne, nb), I32)],
        scratch_shapes=[pltpu.VMEM((ne, s), F32)],
        compiler_params=_cparams(("arbitrary",)),
        name="route",
    )(logits)


SLOT_ALIGN = 16


def _slot_windows(base, cap, chunk, win):
    lo = base[:, :, ::chunk // LANES]
    hi = jnp.concatenate([lo[:, :, 1:], jnp.full_like(lo[:, :, :1], cap)], axis=2)
    start = (lo // SLOT_ALIGN) * SLOT_ALIGN
    nwin = jnp.max((hi - start + win - 1) // win, axis=1)
    return jnp.transpose(start, (0, 2, 1)).reshape(-1), nwin.reshape(-1)


def _window(st_ref, idx, w, win, cap):
    first = st_ref[idx] + w * win
    return first, pl.multiple_of(jnp.minimum(first, cap - win), SLOT_ALIGN)


def _gather_kernel(st_ref, nw_ref, pos_ref, h_ref, o_ref, *, win):
    b, j, nch = pl.program_id(0), pl.program_id(1), pl.num_programs(1)
    ne, cap, _ = o_ref.shape
    tk = h_ref.shape[0]

    @pl.when(j == 0)
    def _():
        o_ref[...] = jnp.zeros_like(o_ref)

    pos = pos_ref[...]
    h = h_ref[...]
    row = lax.broadcasted_iota(I32, (win, tk), 0)

    def window(w, carry):
        starts, lhs = [], []
        for e in range(ne):
            first, start = _window(st_ref, (b * nch + j) * ne + e, w, win, cap)
            slot = row + start
            hit = (pos[e:e + 1, :] == slot) & (slot >= first)
            lhs.append(jnp.where(hit, 1.0, 0.0).astype(BF16))
            starts.append(start)
        res = jnp.dot(jnp.concatenate(lhs, axis=0), h, preferred_element_type=F32)
        for e, start in enumerate(starts):
            o_ref[e, pl.ds(start, win), :] += res[e * win:(e + 1) * win].astype(o_ref.dtype)
        return carry

    lax.fori_loop(0, nw_ref[b * nch + j], window, 0)


def moe_gather(pos_em, h2, base, cap, tk=256, win=64):
    b, ne, s = pos_em.shape
    d = h2.shape[2]
    starts, nwin = _slot_windows(base, cap, tk, win)
    return pl.pallas_call(
        functools.partial(_gather_kernel, win=win),
        grid_spec=pltpu.PrefetchScalarGridSpec(
            num_scalar_prefetch=2,
            grid=(b, s // tk),
            in_specs=[pl.BlockSpec((None, ne, tk), lambda i, j, st, nw: (i, 0, j)),
                      pl.BlockSpec((None, tk, d), lambda i, j, st, nw: (i, j, 0))],
            out_specs=pl.BlockSpec((ne, cap, d), lambda i, j, st, nw: (0, i, 0))),
        out_shape=jax.ShapeDtypeStruct((ne, b * cap, d), BF16),
        compiler_params=_cparams(("arbitrary", "arbitrary"), VMEM_LIMIT_BYTES),
        name="moe_gather",
    )(starts, nwin, pos_em, h2)


def _ffn_kernel(x_ref, wg_ref, wu_ref, wd_ref, o_ref, acc_ref, *, sub):
    f = pl.program_id(2)

    @pl.when(f == 0)
    def _():
        acc_ref[...] = jnp.zeros_like(acc_ref)

    wg = wg_ref[...].astype(BF16)
    wu = wu_ref[...].astype(BF16)
    wd = wd_ref[...].astype(BF16)
    for r in range(x_ref.shape[0] // sub):
        rows = pl.ds(r * sub, sub)
        x = x_ref[rows, :]
        g = jnp.dot(x, wg, preferred_element_type=F32)
        u = jnp.dot(x, wu, preferred_element_type=F32)
        h = (_silu(g) * u).astype(BF16)
        acc_ref[rows, :] += jnp.dot(h, wd, preferred_element_type=F32)

    @pl.when(f == pl.num_programs(2) - 1)
    def _():
        o_ref[...] = acc_ref[...].astype(o_ref.dtype)


def moe_ffn(xe, w_gate, w_up, w_down, tm=2048, tf=256, sub=512):
    ne, m, d = xe.shape
    ff = w_gate.shape[2]
    tm = min(tm, m)
    return pl.pallas_call(
        functools.partial(_ffn_kernel, sub=min(sub, tm)),
        grid=(ne, m // tm, ff // tf),
        in_specs=[pl.BlockSpec((None, tm, d), lambda e, i, f: (e, i, 0)),
                  pl.BlockSpec((None, d, tf), lambda e, i, f: (e, 0, f)),
                  pl.BlockSpec((None, d, tf), lambda e, i, f: (e, 0, f)),
                  pl.BlockSpec((None, tf, d), lambda e, i, f: (e, f, 0))],
        out_specs=pl.BlockSpec((None, tm, d), lambda e, i, f: (e, i, 0)),
        out_shape=jax.ShapeDtypeStruct((ne, m, d), BF16),
        scratch_shapes=[pltpu.VMEM((tm, d), F32)],
        compiler_params=_cparams(("arbitrary", "arbitrary", "arbitrary"), VMEM_LIMIT_BYTES),
        name="moe_ffn",
    )(xe, w_gate, w_up, w_down)


def _combine_kernel(st_ref, nw_ref, pos_ref, gate_ref, ye_ref, x1_ref, g2_ref, fg_ref, o_ref, acc_ref,
                    *, win):
    b, j, nt = pl.program_id(0), pl.program_id(1), pl.num_programs(1)
    tt, ne = pos_ref.shape
    cap = ye_ref.shape[1]
    acc_ref[...] = jnp.zeros_like(acc_ref)
    pos = pos_ref[...]
    gate = gate_ref[...]
    col = lax.broadcasted_iota(I32, (tt, win), 1)

    def window(w, carry):
        acc = None
        for p in range(ne // 2):
            lhs, rhs = [], []
            for e in (2 * p, 2 * p + 1):
                first, start = _window(st_ref, (b * nt + j) * ne + e, w, win, cap)
                slot = col + start
                hit = (pos[:, e:e + 1] == slot) & (slot >= first)
                lhs.append(jnp.where(hit, gate[:, e:e + 1], 0.0).astype(BF16))
                rhs.append(ye_ref[e, pl.ds(start, win), :])
            part = jnp.dot(jnp.concatenate(lhs, axis=1), jnp.concatenate(rhs, axis=0),
                           preferred_element_type=F32)
            acc = part if acc is None else acc + part
        acc_ref[...] += acc
        return carry

    lax.fori_loop(0, nw_ref[b * nt + j], window, 0)
    xo = x1_ref[...] + g2_ref[...] * acc_ref[...]
    o_ref[...] = _rmsnorm(xo, fg_ref[...])


def moe_combine(pos_tm, gate_tm, base, ye, x1, g2, final_g, cap, tt=512, win=128):
    b, s, ne = pos_tm.shape
    d = x1.shape[2]
    starts, nwin = _slot_windows(base, cap, tt, win)
    tok = lambda n: pl.BlockSpec((None, tt, n), lambda i, j, st, nw: (i, j, 0))
    return pl.pallas_call(
        functools.partial(_combine_kernel, win=win),
        grid_spec=pltpu.PrefetchScalarGridSpec(
            num_scalar_prefetch=2,
            grid=(b, s // tt),
            in_specs=[tok(ne), tok(ne),
                      pl.BlockSpec((ne, cap, d), lambda i, j, st, nw: (0, i, 0)),
                      tok(d),
                      pl.BlockSpec((None, 1, d), lambda i, j, st, nw: (i, 0, 0)),
                      pl.BlockSpec((1, d), lambda i, j, st, nw: (0, 0))],
            out_specs=tok(d),
            scratch_shapes=[pltpu.VMEM((tt, d), F32)]),
        out_shape=jax.ShapeDtypeStruct((b, s, d), F32),
        compiler_params=_cparams(("arbitrary", "arbitrary"), VMEM_LIMIT_BYTES),
        name="moe_combine",
    )(starts, nwin, pos_tm, gate_tm, ye, x1, g2, final_g.reshape(1, d))


def _layer(x, ctx, mods, norm1_g, norm2_g, w_in, w_out, conv_w, conv_b, filt, hy_bias, s5p,
           s5_c_re, s5_c_im, s5_d, s5_glu_w, s5_glu_b, router_w, ex_w_gate, ex_w_up, ex_w_down,
           final_g):
    b, s, d = x.shape
    n_order, hw = hy_bias.shape
    hy_cols = (n_order + 1) * hw
    sw = w_in.shape[1] - hy_cols
    rows = s // GRID_W
    ne = router_w.shape[1]
    cap = CAPACITY_FACTOR * s // ne

    per_b = lambda k: mods[:b, k * d:(k + 1) * d].reshape(b, 1, d)
    ctx_v = lambda k: jnp.broadcast_to(mods[b:b + 1, k * d:(k + 1) * d].reshape(1, 1, d), (b, 1, d))
    sh1, sc1, g1, sh2, sc2, g2 = [per_b(k) for k in range(N_MOD)]

    w_in_bf = w_in.astype(BF16)
    (u_ctx,) = inproj(ctx, norm1_g, ctx_v(0), ctx_v(1), w_in_bf, 0, w_cols=(hy_cols // sw, sw))
    z_hy, u = inproj(x, norm1_g, sh1, sc1, w_in_bf, hy_cols, conv=(conv_w, conv_b, GRID_W))

    hraw = hyena_filter_mlp(s, *filt)
    kf = hyena_filter_spectra(hraw, n_order, hw)
    y1 = hyena_conv(z_hy, 0, z_hy, 1, kf, hy_bias, 0)
    hy = hyena_conv(y1, 0, z_hy, 2, kf, hy_bias, 1)

    s5w = s5_chunk_weights(*s5p, s5_c_re, s5_c_im, sw // LANES, S5_CHUNK)
    yf, yb = s5_scan(u_ctx, u, s5w, b)

    x1, h2, logits = mixer_tail(x, hy, yf, yb, u, s5_d, s5_glu_w, s5_glu_b, w_out, g1, norm2_g,
                                sh2, sc2, router_w)
    pos_em, pos_tm, gate_tm, base = route(logits, cap)
    xe = moe_gather(pos_em, h2, base, cap)
    ye = moe_ffn(xe, ex_w_gate, ex_w_up, ex_w_down)
    return moe_combine(pos_tm, gate_tm, base, ye, x1, g2, final_g, cap)


def kernel(x, c, ctx, c_ctx, mod_w, mod_b, norm1_g, norm2_g, w_in, w_out, conv_w, conv_b, hy_w1, hy_b1, hy_freq, hy_w2, hy_b2, hy_w3, hy_b3, hy_bias, s5_lam_re, s5_lam_im, s5_log_step, s5_b_re, s5_b_im, s5_c_re, s5_c_im, s5_d, s5_glu_w, s5_glu_b, router_w, ex_w_gate, ex_w_up, ex_w_down, final_g):
    depth = mod_w.shape[0]
    assert depth == 1, "context-token updates of non-final layers are not implemented"
    b, _, d = x.shape
    l = 0
    pad = (-(b + 1)) % SUBLANES
    cond = jnp.concatenate([c, c_ctx[None], jnp.zeros((pad, d), F32)], axis=0)
    mods = adaln_mods(cond, mod_w[l], mod_b[l])
    filt = (hy_w1[l], hy_b1[l], hy_freq[l], hy_w2[l], hy_b2[l], hy_w3[l], hy_b3[l])
    s5p = (s5_lam_re[l], s5_lam_im[l], s5_log_step[l], s5_b_re[l], s5_b_im[l])
    return _layer(x, ctx, mods, norm1_g[l], norm2_g[l], w_in[l], w_out[l], conv_w[l], conv_b[l],
                  filt, hy_bias[l], s5p, s5_c_re[l], s5_c_im[l], s5_d[l], s5_glu_w[l], s5_glu_b[l],
                  router_w[l], ex_w_gate[l], ex_w_up[l], ex_w_down[l], final_g)
```

```python
import functools
import math

import numpy as np
import jax
import jax.numpy as jnp
from jax import lax
from jax.experimental import pallas as pl
from jax.experimental.pallas import tpu as pltpu

F32 = jnp.float32
BF16 = jnp.bfloat16
I32 = jnp.int32
HIGHEST = lax.Precision.HIGHEST

SUBLANES = 8
LANES = 128
VMEM_LIMIT_BYTES = 58 * 1024 * 1024

GRID_W = 64
N_MOD = 6
NORM_EPS = 1e-6
POS_BANDS = 16
DECAY_FAST = 0.3
DECAY_SLOW = 1.5
DECAY_TARGET = 1e-2
CAPACITY_FACTOR = 2

DFT_N1 = 64
DFT_N2 = 128
DFT_K1 = DFT_N1 // 2 + 1
DFT_ROWS = 2 * DFT_K1

S5_CHUNK = 4


def _cparams(sem, vmem=None):
    return pltpu.CompilerParams(dimension_semantics=sem, vmem_limit_bytes=vmem)


def _silu(x):
    return x * (1.0 / (1.0 + jnp.exp(-x)))


def _rmsnorm(x, g):
    ms = jnp.mean(x * x, axis=-1, keepdims=True)
    return x * lax.rsqrt(ms + NORM_EPS) * g


def _adaln_kernel(c_ref, w_ref, b_ref, o_ref):
    s = _silu(c_ref[...])
    o_ref[...] = jnp.dot(s, w_ref[...], precision=HIGHEST, preferred_element_type=F32) + b_ref[...]


def adaln_mods(cond, mod_w, mod_b, tn=1536):
    rows, d = cond.shape
    n = mod_w.shape[1]
    return pl.pallas_call(
        _adaln_kernel,
        grid=(n // tn,),
        in_specs=[pl.BlockSpec((rows, d), lambda j: (0, 0)),
                  pl.BlockSpec((d, tn), lambda j: (0, j)),
                  pl.BlockSpec((1, tn), lambda j: (0, j))],
        out_specs=pl.BlockSpec((rows, tn), lambda j: (0, j)),
        out_shape=jax.ShapeDtypeStruct((rows, n), F32),
        compiler_params=_cparams(("arbitrary",)),
        name="adaln",
    )(cond, mod_w, mod_b.reshape(1, n))


def _short_conv(z, w_ref, b_ref, row_len):
    length = z.shape[0]
    pos = lax.broadcasted_iota(I32, (length, 1), 0) % row_len
    zm = jnp.where(pos == 0, 0.0, pltpu.roll(z, 1, 0))
    zp = jnp.where(pos == row_len - 1, 0.0, pltpu.roll(z, length - 1, 0))
    w = w_ref[...]
    return zm * w[0:1, :] + z * w[1:2, :] + zp * w[2:3, :] + b_ref[...]


@functools.lru_cache(maxsize=None)
def _perm_time_major(batch, tt):
    n = batch * tt
    p = np.zeros((n, n), np.float32)
    t, b = np.meshgrid(np.arange(tt), np.arange(batch), indexing='ij')
    p[(t * batch + b).ravel(), (b * tt + t).ravel()] = 1.0
    return jnp.asarray(p, dtype=BF16)


def _inproj_kernel(x_ref, g_ref, sh_ref, sc_ref, w_ref, perm_ref, *refs, n_tok, row_len):
    bsz, tt, d = x_ref.shape
    h = _rmsnorm(x_ref[...], g_ref[...])
    h = h * (1.0 + sc_ref[...]) + sh_ref[...]
    z = jnp.dot(h.reshape(bsz * tt, d).astype(BF16), w_ref[...], preferred_element_type=F32)
    if n_tok:
        cw_ref, cb_ref, tok_ref, tm_ref = refs
        tok_ref[...] = _short_conv(z[:, :n_tok], cw_ref, cb_ref, row_len).reshape(bsz, tt, n_tok)
    else:
        (tm_ref,) = refs
    u = z[:, n_tok:].astype(BF16)
    tm_ref[...] = jnp.dot(perm_ref[...], u, preferred_element_type=F32).astype(BF16)


def inproj(x, g, shift, scale, w_bf16, n_tok, conv=None, w_cols=None, tt=64):
    b, s, d = x.shape
    col, n = (0, w_bf16.shape[1]) if w_cols is None else w_cols
    perm = _perm_time_major(b, tt)
    const = lambda a: (pl.BlockSpec((d, n), lambda j: (0, col)) if a is w_bf16
                       else pl.BlockSpec(a.shape, lambda j: (0,) * a.ndim))
    args = [x, g.reshape(1, d), shift, scale, w_bf16, perm]
    out_specs = [pl.BlockSpec((tt * b, n - n_tok), lambda j: (j, 0))]
    out_shape = [jax.ShapeDtypeStruct((s * b, n - n_tok), BF16)]
    row_len = 1
    if n_tok:
        cw, cb, row_len = conv
        assert tt % row_len == 0
        args += [cw, cb.reshape(1, -1)]
        out_specs.insert(0, pl.BlockSpec((b, tt, n_tok), lambda j: (0, j, 0)))
        out_shape.insert(0, jax.ShapeDtypeStruct((b, s, n_tok), F32))
    return pl.pallas_call(
        functools.partial(_inproj_kernel, n_tok=n_tok, row_len=row_len),
        grid=(s // tt,),
        in_specs=[pl.BlockSpec((b, tt, d), lambda j: (0, j, 0))] + [const(a) for a in args[1:]],
        out_specs=out_specs,
        out_shape=out_shape,
        compiler_params=_cparams(("arbitrary",)),
        name="inproj",
    )(*args)


def _filt_mlp_kernel(w1t_ref, w1c_ref, w1s_ref, b1_ref, fr_ref, w2_ref, b2_ref, w3_ref, b3_ref,
                     o_ref, *, length, tl):
    i0 = pl.program_id(0) * tl
    idx = (lax.broadcasted_iota(I32, (tl, 1), 0) + i0).astype(F32)
    t = idx / float(length - 1)
    omega = (2.0 * math.pi) * idx / float(length)
    fstep = ((POS_BANDS - 1) - 1e-4) / (POS_BANDS - 1)
    f = 1e-4 + lax.broadcasted_iota(I32, (1, POS_BANDS), 1).astype(F32) * fstep
    arg = omega * f
    pre = (t * w1t_ref[...]
           + jnp.dot(jnp.cos(arg), w1c_ref[...], precision=HIGHEST, preferred_element_type=F32)
           - jnp.dot(jnp.sin(arg), w1s_ref[...], precision=HIGHEST, preferred_element_type=F32)
           + b1_ref[...])
    fr = fr_ref[...]
    h = jnp.sin(fr[0:1, :] * pre)
    h = jnp.sin(fr[1:2, :] * (jnp.dot(h, w2_ref[...], precision=HIGHEST,
                                      preferred_element_type=F32) + b2_ref[...]))
    o_ref[...] = jnp.dot(h, w3_ref[...], precision=HIGHEST, preferred_element_type=F32) + b3_ref[...]


def hyena_filter_mlp(length, w1, b1, freq, w2, b2, w3, b3, tl=512):
    fw = w1.shape[1]
    n = w3.shape[1]
    full = lambda shape: pl.BlockSpec(shape, lambda i: (0, 0))
    return pl.pallas_call(
        functools.partial(_filt_mlp_kernel, length=length, tl=tl),
        grid=(length // tl,),
        in_specs=[full((1, fw)), full((POS_BANDS, fw)), full((POS_BANDS, fw)), full((1, fw)),
                  full((2, fw)), full((fw, fw)), full((1, fw)), full((fw, n)), full((1, n))],
        out_specs=pl.BlockSpec((tl, n), lambda i: (i, 0)),
        out_shape=jax.ShapeDtypeStruct((length, n), F32),
        compiler_params=_cparams(("arbitrary",)),
        name="hyena_filter_mlp",
    )(w1[0:1], w1[1:1 + POS_BANDS], w1[1 + POS_BANDS:], b1.reshape(1, fw), freq, w2,
      b2.reshape(1, fw), w3, b3.reshape(1, n))


@functools.lru_cache(maxsize=None)
def _dft_tables():
    n1n, n2n, k1n = DFT_N1, DFT_N2, DFT_K1
    n = n1n * n2n
    half = n1n // 2
    k1 = np.arange(k1n)[:, None]
    n1 = np.arange(half)[None, :]
    th = 2.0 * np.pi * k1 * n1 / n1n
    f1 = np.zeros((DFT_ROWS, half))
    f1[0::2] = np.cos(th)
    f1[1::2] = -np.sin(th)
    wgt = np.where((k1 == 0) | (k1 == half), 1.0, 2.0)
    g1 = np.zeros((half, DFT_ROWS))
    g1[:, 0::2] = (wgt * np.cos(th)).T / n
    g1[:, 1::2] = (-wgt * np.sin(th)).T / n
    eye = np.eye(SUBLANES)
    fk = np.kron(f1, eye)
    gk = np.kron(g1, eye)
    k2 = np.arange(n2n)[:, None]
    n2 = np.arange(n2n)[None, :]
    f3 = np.zeros((k1n, 2 * n2n, 2 * n2n))
    for kk in range(k1n):
        ph = 2.0 * np.pi * n2 * (n1n * k2 + kk) / n
        tr, ti = np.cos(ph), -np.sin(ph)
        f3[kk, :n2n, :n2n] = tr
        f3[kk, :n2n, n2n:] = -ti
        f3[kk, n2n:, :n2n] = ti
        f3[kk, n2n:, n2n:] = tr
    g3 = np.transpose(f3, (0, 2, 1))
    to = lambda a: jnp.asarray(a, dtype=F32).astype(BF16)
    return to(fk), to(gk), to(f3), to(g3)


def _dft_stage1(src_ref, a_ref, fk_ref):
    half = DFT_N1 // 2

    def body(m, carry):
        sub = pl.ds(pl.multiple_of(m * SUBLANES, SUBLANES), SUBLANES)
        rows = [src_ref.at[pl.ds(DFT_N2 * n1, DFT_N2)][sub, :] for n1 in range(half)]
        rhs = jnp.concatenate(rows, axis=0).astype(BF16)
        out = jnp.dot(fk_ref[...], rhs, preferred_element_type=F32)
        for j in range(DFT_ROWS):
            a_ref.at[pl.ds(DFT_N2 * j, DFT_N2)][sub, :] = out[SUBLANES * j:SUBLANES * (j + 1)]
        return carry

    lax.fori_loop(0, DFT_N2 // SUBLANES, body, 0, unroll=2)


def _dft_stage3(a_ref, f3_ref, k1):
    r0 = pl.multiple_of(k1 * (2 * DFT_N2), 2 * DFT_N2)
    a = a_ref[pl.ds(r0, 2 * DFT_N2), :].astype(BF16)
    x = jnp.dot(f3_ref[k1], a, preferred_element_type=F32)
    return x[:DFT_N2], x[DFT_N2:]


def _filt_spec_kernel(hf_ref, hb_ref, fk_ref, f3_ref, o_ref, src_ref, af_ref, ab_ref, *, length):
    c = hf_ref.shape[1]
    cb = pl.program_id(1)
    hw = pl.num_programs(1) * c
    row = lax.broadcasted_iota(I32, (length, 1), 0)
    t = row.astype(F32) / float(length - 1)
    ch = (lax.broadcasted_iota(I32, (1, c), 1) + cb * c).astype(F32)
    d0 = math.log(DECAY_TARGET) / DECAY_FAST
    d1 = math.log(DECAY_TARGET) / DECAY_SLOW
    deltas = jnp.abs(d0 + ch * ((d1 - d0) / float(hw - 1)))
    decay = jnp.exp(-t * deltas)
    fwd = hf_ref[...] * decay
    bwd = jnp.where(row == 0, 0.0, hb_ref[...] * decay)
    inv = 1.0 / (jnp.sum(jnp.abs(fwd), axis=0, keepdims=True)
                 + jnp.sum(jnp.abs(bwd), axis=0, keepdims=True))
    src_ref[...] = fwd
    _dft_stage1(src_ref, af_ref, fk_ref)
    src_ref[...] = bwd
    _dft_stage1(src_ref, ab_ref, fk_ref)

    def body(k1, carry):
        fr, fi = _dft_stage3(af_ref, f3_ref, k1)
        br, bi = _dft_stage3(ab_ref, f3_ref, k1)
        o_ref[k1, 0] = ((fr + br) * inv).astype(o_ref.dtype)
        o_ref[k1, 1] = ((fi - bi) * inv).astype(o_ref.dtype)
        return carry

    lax.fori_loop(0, DFT_K1, body, 0, unroll=3)


def hyena_filter_spectra(hraw, n_order, width, c_blk=256):
    length = hraw.shape[0]
    assert 2 * length == DFT_N1 * DFT_N2
    fk, _, f3, _ = _dft_tables()
    ncb = width // c_blk
    return pl.pallas_call(
        functools.partial(_filt_spec_kernel, length=length),
        grid=(n_order, ncb),
        in_specs=[pl.BlockSpec((length, c_blk), lambda o, j: (0, o * 2 * ncb + j)),
                  pl.BlockSpec((length, c_blk), lambda o, j: (0, o * 2 * ncb + ncb + j)),
                  pl.BlockSpec(fk.shape, lambda o, j: (0, 0)),
                  pl.BlockSpec(f3.shape, lambda o, j: (0, 0, 0))],
        out_specs=pl.BlockSpec((None, DFT_K1, 2, DFT_N2, c_blk), lambda o, j: (o, 0, 0, 0, j)),
        out_shape=jax.ShapeDtypeStruct((n_order, DFT_K1, 2, DFT_N2, width), BF16),
        scratch_shapes=[pltpu.VMEM((length, c_blk), F32),
                        pltpu.VMEM((DFT_ROWS * DFT_N2, c_blk), F32),
                        pltpu.VMEM((DFT_ROWS * DFT_N2, c_blk), F32)],
        compiler_params=_cparams(("arbitrary", "arbitrary"), VMEM_LIMIT_BYTES),
        name="hyena_filter_spectrum",
    )(hraw, hraw, fk, f3)


def _hyena_conv_kernel(s_ref, m_ref, kf_ref, bias_ref, fk_ref, f3_ref, g3_ref, gk_ref, o_ref, a_ref,
                       *, group):
    half = DFT_N1 // 2
    blk_rows = 2 * DFT_N2
    _dft_stage1(s_ref, a_ref, fk_ref)

    def body3(i, carry):
        k1s = [i * group + q for q in range(group)]
        r0s = [pl.multiple_of(k1 * blk_rows, blk_rows) for k1 in k1s]
        blocks = [a_ref[pl.ds(r0, blk_rows), :].astype(BF16) for r0 in r0s]
        outs = []
        for k1, a in zip(k1s, blocks):
            x = jnp.dot(f3_ref[k1], a, preferred_element_type=F32)
            xr, xi = x[:DFT_N2], x[DFT_N2:]
            kr = kf_ref[k1, 0].astype(F32)
            ki = kf_ref[k1, 1].astype(F32)
            y = jnp.concatenate([xr * kr - xi * ki, xr * ki + xi * kr], axis=0).astype(BF16)
            outs.append(jnp.dot(g3_ref[k1], y, preferred_element_type=F32))
        for r0, o in zip(r0s, outs):
            a_ref[pl.ds(r0, blk_rows), :] = o
        return carry

    lax.fori_loop(0, DFT_K1 // group, body3, 0)
    bias = bias_ref[...]

    def body1(m, carry):
        sub = pl.ds(pl.multiple_of(m * SUBLANES, SUBLANES), SUBLANES)
        blk = [a_ref.at[pl.ds(DFT_N2 * j, DFT_N2)][sub, :] for j in range(DFT_ROWS)]
        rhs = jnp.concatenate(blk, axis=0).astype(BF16)
        out = jnp.dot(gk_ref[...], rhs, preferred_element_type=F32)
        for n1 in range(half):
            blk_n1 = pl.ds(DFT_N2 * n1, DFT_N2)
            conv = out[SUBLANES * n1:SUBLANES * (n1 + 1)]
            o_ref.at[blk_n1][sub, :] = m_ref.at[blk_n1][sub, :] * (
                conv + s_ref.at[blk_n1][sub, :] * bias)
        return carry

    lax.fori_loop(0, DFT_N2 // SUBLANES, body1, 0, unroll=2)


def hyena_conv(sig, sig_col, mul, mul_col, kf, bias, order, c_blk=256, group=11):
    b, length, _ = sig.shape
    width = kf.shape[-1]
    ncb = width // c_blk
    assert DFT_K1 % group == 0
    fk, gk, f3, g3 = _dft_tables()
    once = pl.Buffered(1)
    const2 = lambda a: pl.BlockSpec(a.shape, lambda j, i: (0, 0), pipeline_mode=once)
    const3 = lambda a: pl.BlockSpec(a.shape, lambda j, i: (0, 0, 0), pipeline_mode=once)
    return pl.pallas_call(
        functools.partial(_hyena_conv_kernel, group=group),
        grid=(ncb, b),
        in_specs=[pl.BlockSpec((None, length, c_blk), lambda j, i: (i, 0, sig_col * ncb + j)),
                  pl.BlockSpec((None, length, c_blk), lambda j, i: (i, 0, mul_col * ncb + j)),
                  pl.BlockSpec((None, DFT_K1, 2, DFT_N2, c_blk), lambda j, i: (order, 0, 0, 0, j),
                               pipeline_mode=once),
                  pl.BlockSpec((None, 1, c_blk), lambda j, i: (order, 0, j), pipeline_mode=once),
                  const2(fk), const3(f3), const3(g3), const2(gk)],
        out_specs=pl.BlockSpec((None, length, c_blk), lambda j, i: (i, 0, j)),
        out_shape=jax.ShapeDtypeStruct((b, length, width), F32),
        scratch_shapes=[pltpu.VMEM((DFT_ROWS * DFT_N2, c_blk), F32)],
        compiler_params=_cparams(("arbitrary", "arbitrary"), VMEM_LIMIT_BYTES),
        name="hyena_conv",
    )(sig, mul, kf, bias.reshape(-1, 1, width), fk, f3, g3, gk)


def _s5_powers_kernel(lr_ref, li_ref, dt_ref, vr_ref, vi_ref, or_ref, oi_ref, *, n_pow, zoh):
    lr, li, dt = lr_ref[...], li_ref[...], jnp.exp(dt_ref[...])
    mag = jnp.exp(lr * dt)
    ar = mag * jnp.cos(li * dt)
    ai = mag * jnp.sin(li * dt)
    vr, vi = vr_ref[...], vi_ref[...]
    if zoh:
        den = 1.0 / (lr * lr + li * li)
        qr = ((ar - 1.0) * lr + ai * li) * den
        qi = (ai * lr - (ar - 1.0) * li) * den
        vr, vi = qr * vr - qi * vi, qr * vi + qi * vr
    for j in range(n_pow):
        or_ref[j] = vr
        oi_ref[j] = vi
        vr, vi = ar * vr - ai * vi, ar * vi + ai * vr


def s5_powers(lam_re, lam_im, log_step, v_re, v_im, state_axis, n_pow, zoh):
    nd, g, a, b = v_re.shape
    expand = (lambda x: x[..., :, None]) if state_axis == 2 else (lambda x: x[..., None, :])
    rep = lambda x: jnp.broadcast_to(expand(x), v_re.shape).reshape(nd * g, a * b)
    dt = jnp.broadcast_to(log_step[:, :, None, None], v_re.shape).reshape(nd * g, a * b)
    flat = lambda x: x.reshape(nd * g, a * b)
    shp = jax.ShapeDtypeStruct((n_pow, nd * g, a * b), F32)
    o_r, o_i = pl.pallas_call(
        functools.partial(_s5_powers_kernel, n_pow=n_pow, zoh=zoh), out_shape=[shp, shp],
        name="s5_powers",
    )(rep(lam_re), rep(lam_im), dt, flat(v_re), flat(v_im))
    un = lambda x: x.reshape(n_pow, nd, g, a, b)
    return un(o_r), un(o_i)


def _s5_taps_kernel(cr_ref, ci_ref, br_ref, bi_ref, o_ref):
    n_pow, nk = br_ref.shape[0], br_ref.shape[1]
    for j in range(n_pow):
        for k in range(nk):
            o_ref[j, k] = (jnp.dot(cr_ref[k], br_ref[j, k], preferred_element_type=F32)
                           - jnp.dot(ci_ref[k], bi_ref[j, k], preferred_element_type=F32))


def s5_taps(c_re, c_im, bbp_r, bbp_i, nk):
    nd, g, h, p = c_re.shape
    n_pow = bbp_r.shape[0]
    gpk = g // nk
    eye = jnp.eye(gpk, dtype=F32)
    bdiag = lambda c: jnp.einsum('dkahp,ab->dkahbp', c.reshape(nd, nk, gpk, h, p), eye).reshape(
        nd, nk, gpk * h, gpk * p).astype(BF16)
    flat = lambda x: jnp.transpose(x, (1, 0, 2, 3, 4)).reshape(nd, n_pow, nk, gpk * p, h).astype(BF16)
    per_d = lambda a: pl.BlockSpec((None,) + a.shape[1:], lambda d: (d,) + (0,) * (a.ndim - 1))
    args = (bdiag(c_re), bdiag(c_im), flat(bbp_r), flat(bbp_i))
    shp = jax.ShapeDtypeStruct((nd, n_pow, nk, gpk * h, h), F32)
    out = pl.pallas_call(
        _s5_taps_kernel,
        grid=(nd,),
        in_specs=[per_d(a) for a in args],
        out_specs=per_d(shp),
        out_shape=shp,
        compiler_params=_cparams(("arbitrary",)),
        name="s5_taps",
    )(*args)
    return out.reshape(nd, n_pow, g, h, h)


def _s5_scan_kernel(ucf_ref, uf_ref, ucb_ref, ub_ref, qf_ref, qb_ref, pf_ref, pb_ref, mf_ref, mb_ref,
                    lam_ref, yf_ref, yb_ref, xf_ref, xb_ref, st_ref, *, nc, kpp, tlen):
    width = uf_ref.shape[1]
    batch = st_ref.shape[3]
    nk = xf_ref.shape[1]
    ck = width // nk
    rows = xf_ref.shape[2]
    cpg = rows // batch
    i = pl.program_id(0)
    is_ctx = i < nc

    @pl.when(i == 0)
    def _():
        st_ref[...] = jnp.zeros_like(st_ref)

    pick = lambda c_ref, l_ref: jnp.where(is_ctx, c_ref[...], l_ref[...]).astype(F32).reshape(
        cpg, tlen, batch, width)
    uf = pick(ucf_ref, uf_ref)
    ub = pick(ucb_ref, ub_ref)

    def scan(x_ref, d, reverse):
        for k0 in range(0, nk, kpp):
            ks = slice(k0, k0 + kpp)
            lr, li = lam_ref[d, 0, ks], lam_ref[d, 1, ks]

            def body(c, carry, ks=ks, lr=lr, li=li):
                sr, si = carry
                cc = (cpg - 1 - c) if reverse else c
                r = pl.ds(pl.multiple_of(cc * batch, batch), batch)
                qr, qi = x_ref[0, ks, r, :], x_ref[1, ks, r, :]
                x_ref[0, ks, r, :] = sr
                x_ref[1, ks, r, :] = si
                return lr * sr - li * si + qr, lr * si + li * sr + qi

            st_ref[d, 0, ks], st_ref[d, 1, ks] = lax.fori_loop(
                0, cpg, body, (st_ref[d, 0, ks], st_ref[d, 1, ks]))

    for u, q_ref, p_ref, m_ref, x_ref, y_ref, d in ((uf, qf_ref, pf_ref, mf_ref, xf_ref, yf_ref, 0),
                                                    (ub, qb_ref, pb_ref, mb_ref, xb_ref, yb_ref, 1)):
        uks = []
        for k in range(nk):
            uk = jnp.concatenate([u[:, t, :, k * ck:(k + 1) * ck].reshape(rows, ck)
                                  for t in range(tlen)], axis=1).astype(BF16)
            uks.append(uk)
            for ri in range(2):
                x_ref[ri, k] = jnp.dot(uk, q_ref[ri, k], preferred_element_type=F32)
        scan(x_ref, d, d == 1)
        for k in range(nk):
            yk = (jnp.dot(uks[k], m_ref[k], preferred_element_type=F32)
                  + jnp.dot(x_ref[0, k].astype(BF16), p_ref[0, k], preferred_element_type=F32)
                  + jnp.dot(x_ref[1, k].astype(BF16), p_ref[1, k], preferred_element_type=F32))
            y_ref[:, k * ck:(k + 1) * ck] = jnp.stack(
                [yk[:, t * ck:(t + 1) * ck].reshape(cpg, batch, ck) for t in range(tlen)],
                axis=1).reshape(cpg * tlen * batch, ck)


def s5_chunk_weights(lam_re, lam_im, log_step, b_re, b_im, c_re, c_im, nk, tlen):
    nd, g, p, h = b_re.shape
    gpk, sk = g // nk, g * p // nk
    bb_r, bb_i = s5_powers(lam_re, lam_im, log_step, b_re, b_im, 2, tlen, True)
    cl_r, cl_i = s5_powers(lam_re, lam_im, log_step, c_re, c_im, 3, tlen + 1, False)
    ones = jnp.ones((nd, g, p, 2), F32)
    pw_r, pw_i = s5_powers(lam_re, lam_im, log_step, ones, jnp.zeros_like(ones), 2, tlen + 1, False)
    taps = s5_taps(c_re, c_im, bb_r, bb_i, nk)
    dirs = np.arange(nd)[:, None]
    step = np.arange(tlen)[None, :]
    jq = np.where(dirs == 0, tlen - 1 - step, step)
    jp = np.where(dirs == 0, step + 1, tlen - step)
    lag = step[0][None, None, :] - step[0][None, :, None]
    lag = np.where(dirs[:, :, None] == 0, lag, -lag)
    ch = tlen * gpk * h
    rep_p = np.tile(np.eye(p, dtype=np.float32), (1, gpk))
    rep_h = np.einsum('tu,hi,b->thubi', np.eye(tlen), np.eye(h), np.ones(gpk)).reshape(
        tlen * h, ch).astype(np.float32)
    grp_ch = np.tile(np.repeat(np.arange(gpk), h), tlen)
    grp_st = np.repeat(np.arange(gpk), p)
    qsel = jnp.stack([bb_r, bb_i])[:, jq, dirs].reshape(2, nd, tlen, nk, gpk, p, h)
    qrows = jnp.transpose(qsel, (1, 0, 3, 2, 4, 6, 5)).reshape(nd, 2, nk, ch, p)
    q = jnp.where(grp_ch[:, None] == grp_st[None, :], qrows @ rep_p, 0.0)
    psel = jnp.stack([cl_r, -cl_i])[:, jp, dirs].reshape(2, nd, tlen, nk, gpk, h, p)
    prows = jnp.transpose(psel, (1, 0, 3, 4, 6, 2, 5)).reshape(nd, 2, nk, sk, tlen * h)
    pm = jnp.where(grp_st[:, None] == grp_ch[None, :], prows @ rep_h, 0.0)
    tsel = jnp.where((lag >= 0)[..., None, None, None], taps[dirs[:, :, None], np.maximum(lag, 0)], 0.0)
    tsel = tsel.reshape(nd, tlen, tlen, nk, gpk, h, h)
    trows = jnp.transpose(tsel, (0, 3, 1, 4, 6, 2, 5)).reshape(nd, nk, ch, tlen * h)
    m = jnp.where(grp_ch[:, None] == grp_ch[None, :], trows @ rep_h, 0.0)
    decay = jnp.stack([pw_r[tlen, ..., 0], pw_i[tlen, ..., 0]], axis=1).reshape(nd, 2, nk, 1, sk)
    return q.astype(BF16), pm.astype(BF16), m.astype(BF16), decay


def s5_scan(uc, u, weights, batch, steps=128, kpp=2):
    q, pm, m, decay = weights
    width = u.shape[1]
    nk, sk = q.shape[2], q.shape[4]
    tlen = m.shape[2] * nk // width
    cpg = steps // tlen
    rpc = steps * batch
    nc, nl = uc.shape[0] // rpc, u.shape[0] // rpc
    assert steps % tlen == 0 and uc.shape[0] % rpc == 0 and u.shape[0] % rpc == 0
    lam = jnp.broadcast_to(decay, (2, 2, nk, batch, sk))
    blk = lambda f: pl.BlockSpec((rpc, width), lambda i: (f(i), 0))
    lat_f = lambda i: jnp.maximum(i - nc, 0)
    lat_b = lambda i: jnp.clip(nl - 1 - i + nc, 0, nl - 1)
    once = pl.Buffered(1)
    par = lambda a, d: pl.BlockSpec((None,) + a.shape[1:], lambda i: (d,) + (0,) * (a.ndim - 1),
                                    pipeline_mode=once)
    shp = jax.ShapeDtypeStruct(u.shape, F32)
    return pl.pallas_call(
        functools.partial(_s5_scan_kernel, nc=nc, kpp=kpp, tlen=tlen),
        grid=(nc + nl,),
        in_specs=[blk(lambda i: jnp.minimum(i, nc - 1)), blk(lat_f),
                  blk(lambda i: jnp.maximum(nc - 1 - i, 0)), blk(lat_b),
                  par(q, 0), par(q, 1), par(pm, 0), par(pm, 1), par(m, 0), par(m, 1),
                  pl.BlockSpec(lam.shape, lambda i: (0, 0, 0, 0, 0), pipeline_mode=once)],
        out_specs=[blk(lat_f), blk(lat_b)],
        out_shape=[shp, shp],
        scratch_shapes=[pltpu.VMEM((2, nk, cpg * batch, sk), F32),
                        pltpu.VMEM((2, nk, cpg * batch, sk), F32),
                        pltpu.VMEM((2, 2, nk, batch, sk), F32)],
        compiler_params=_cparams(("arbitrary",), VMEM_LIMIT_BYTES),
        name="s5_scan",
    )(uc, u, uc, u, q, q, pm, pm, m, m, lam)


def _mixer_tail_kernel(x_ref, hy_ref, yf_ref, yb_ref, u_ref, d_ref, gw_ref, gb_ref, woh_ref, wos_ref,
                       g1_ref, n2_ref, sh2_ref, sc2_ref, rwt_ref, perm_ref, x1_ref, h2_ref, lg_ref):
    bsz, tt, d = x_ref.shape
    rows = bsz * tt
    y = yf_ref[...] + yb_ref[...] + d_ref[...] * u_ref[...].astype(F32)
    y = 0.5 * y * (1.0 + jnp.tanh(math.sqrt(2.0 / math.pi) * (y + 0.044715 * (y * y * y))))
    gate = jnp.dot(y.astype(BF16), gw_ref[...], preferred_element_type=F32) + gb_ref[...]
    s5 = (y * (1.0 / (1.0 + jnp.exp(-gate)))).astype(BF16)
    s5 = jnp.dot(perm_ref[...], s5, preferred_element_type=F32).astype(BF16)
    hy = hy_ref[...].reshape(rows, hy_ref.shape[2]).astype(BF16)
    mix = (jnp.dot(hy, woh_ref[...], preferred_element_type=F32)
           + jnp.dot(s5, wos_ref[...], preferred_element_type=F32))
    x1 = x_ref[...] + g1_ref[...] * mix.reshape(bsz, tt, d)
    x1_ref[...] = x1
    h2 = _rmsnorm(x1, n2_ref[...]) * (1.0 + sc2_ref[...]) + sh2_ref[...]
    h2_hi = h2.astype(BF16)
    h2_ref[...] = h2_hi
    rw = rwt_ref[...]
    rw_hi = rw.astype(BF16)
    rw_lo = (rw - rw_hi.astype(F32)).astype(BF16)
    h2_hi = h2_hi.reshape(rows, d)
    h2_lo = (h2.reshape(rows, d) - h2_hi.astype(F32)).astype(BF16)
    nt = lambda a, b: lax.dot_general(a, b, (((1,), (1,)), ((), ())), preferred_element_type=F32)
    lg = nt(rw_hi, h2_hi) + (nt(rw_hi, h2_lo) + nt(rw_lo, h2_hi))
    for i in range(bsz):
        lg_ref[i] = lg[:, i * tt:(i + 1) * tt]


def mixer_tail(x, hy, yf, yb, u_tm, s5_d, glu_w, glu_b, w_out, g1, norm2_g, sh2, sc2, router_w,
               tt=128):
    b, s, d = x.shape
    hw = hy.shape[2]
    sw = u_tm.shape[1]
    ne = router_w.shape[1]
    perm = _perm_time_major(b, tt).T
    once = pl.Buffered(1)
    tok = lambda n: pl.BlockSpec((b, tt, n), lambda j: (0, j, 0))
    tmj = pl.BlockSpec((tt * b, sw), lambda j: (j, 0))
    const = lambda a: pl.BlockSpec(a.shape, lambda j: (0,) * a.ndim, pipeline_mode=once)
    consts = [s5_d.reshape(1, sw), glu_w.astype(BF16), glu_b.reshape(1, sw), w_out[:hw].astype(BF16),
              w_out[hw:].astype(BF16), g1, norm2_g.reshape(1, d), sh2, sc2, router_w.T, perm]
    return pl.pallas_call(
        _mixer_tail_kernel,
        grid=(s // tt,),
        in_specs=[tok(d), tok(hw), tmj, tmj, tmj] + [const(a) for a in consts],
        out_specs=[tok(d), tok(d), pl.BlockSpec((b, ne, tt), lambda j: (0, 0, j))],
        out_shape=[jax.ShapeDtypeStruct((b, s, d), F32), jax.ShapeDtypeStruct((b, s, d), BF16),
                   jax.ShapeDtypeStruct((b, ne, s), F32)],
        compiler_params=_cparams(("arbitrary",), VMEM_LIMIT_BYTES),
        name="mixer_tail",
    )(x, hy, yf, yb, u_tm, *consts)


def _lane_cumsum_exclusive(x):
    rows, s = x.shape
    ii = lax.broadcasted_iota(I32, (LANES, LANES), 0)
    jj = lax.broadcasted_iota(I32, (LANES, LANES), 1)
    tri = jnp.where(ii < jj, 1.0, 0.0).astype(BF16)
    carry = jnp.zeros((rows, 1), F32)
    out, base = [], []
    for blk in range(s // LANES):
        xb = x[:, blk * LANES:(blk + 1) * LANES]
        out.append(jnp.dot(xb.astype(BF16), tri, preferred_element_type=F32) + carry)
        base.append(carry)
        carry = carry + jnp.sum(xb, axis=1, keepdims=True)
    return jnp.concatenate(out, axis=1), jnp.concatenate(base, axis=1)


def _route_kernel(lg_ref, pos_em_ref, pos_tm_ref, gate_tm_ref, base_ref, aff_ref, *, cap):
    lg = lg_ref[...]
    ne, s = lg.shape
    ex = jnp.exp(lg - jnp.max(lg, axis=0, keepdims=True))
    aff_ref[...] = ex / jnp.sum(ex, axis=0, keepdims=True)
    aff = aff_ref[...]
    count_ge = lambda v, t: jnp.sum(jnp.where(v >= t, 1.0, 0.0), axis=1, keepdims=True)

    def coarse(i, tb):
        cand = tb | jnp.left_shift(jnp.int32(1), 30 - i)
        return jnp.where(count_ge(aff, pltpu.bitcast(cand, F32)) >= cap, cand, tb)

    tb = lax.fori_loop(0, 31, coarse, jnp.zeros((ne, 1), I32))
    t_hi = pltpu.bitcast(tb, F32)
    ulp = pltpu.bitcast(tb + 1, F32) - t_hi
    resid = aff - t_hi

    def fine(j, carry):
        c, step = carry
        cand = c + step
        return jnp.where(count_ge(resid, cand) >= cap, cand, c), step * 0.5

    t_lo, _ = lax.fori_loop(0, 12, fine, (jnp.zeros((ne, 1), F32), ulp * 0.5))
    gt = resid > t_lo
    eq = resid == t_lo
    need = cap - jnp.sum(jnp.where(gt, 1.0, 0.0), axis=1, keepdims=True)
    eq_rank, _ = _lane_cumsum_exclusive(jnp.where(eq, 1.0, 0.0))
    sel = gt | (eq & (eq_rank < need))
    pos, base = _lane_cumsum_exclusive(jnp.where(sel, 1.0, 0.0))
    posf = jnp.where(sel, pos + 1.0, 0.0)
    gate = jnp.where(sel, aff, 0.0)
    pos_em_ref[...] = posf.astype(I32) - 1
    base_ref[...] = base.astype(I32)
    hi = jnp.floor(posf * (1.0 / 16.0))
    lo = posf - 16.0 * hi
    g1 = gate.astype(BF16)
    r1 = gate - g1.astype(F32)
    g2 = r1.astype(BF16)
    g3 = (r1 - g2.astype(F32)).astype(BF16)
    ii = lax.broadcasted_iota(I32, (LANES, LANES), 0)
    jj = lax.broadcasted_iota(I32, (LANES, LANES), 1)
    eye = jnp.where(ii == jj, 1.0, 0.0).astype(BF16)
    tr = lambda v: lax.dot_general(eye, v, (((1,), (1,)), ((), ())), preferred_element_type=F32)
    for blk in range(s // LANES):
        sl = slice(blk * LANES, (blk + 1) * LANES)
        pos_tm_ref[sl, :] = (16.0 * tr(hi[:, sl].astype(BF16)) + tr(lo[:, sl].astype(BF16))).astype(I32) - 1
        gate_tm_ref[sl, :] = tr(g1[:, sl]) + (tr(g2[:, sl]) + tr(g3[:, sl]))


def route(logits, cap):
    b, ne, s = logits.shape
    nb = s // LANES
    return pl.pallas_call(
        functools.partial(_route_kernel, cap=cap),
        grid=(b,),
        in_specs=[pl.BlockSpec((None, ne, s), lambda i: (i, 0, 0))],
        out_specs=[pl.BlockSpec((None, ne, s), lambda i: (i, 0, 0)),
                   pl.BlockSpec((None, s, ne), lambda i: (i, 0, 0)),
                   pl.BlockSpec((None, s, ne), lambda i: (i, 0, 0)),
                   pl.BlockSpec((None, ne, nb), lambda i: (i, 0, 0))],
        out_shape=[jax.ShapeDtypeStruct((b, ne, s), I32), jax.ShapeDtypeStruct((b, s, ne), I32),
                   jax.ShapeDtypeStruct((b, s, ne), F32), jax.ShapeDtypeStruct((b, ne, nb), I32)],
        scratch_shapes=[pltpu.VMEM((ne, s), F32)],
        compiler_params=_cparams(("arbitrary",)),
        name="route",
    )(logits)


SLOT_ALIGN = 16


def _slot_windows(base, cap, chunk, win):
    lo = base[:, :, ::chunk // LANES]
    hi = jnp.concatenate([lo[:, :, 1:], jnp.full_like(lo[:, :, :1], cap)], axis=2)
    start = (lo // SLOT_ALIGN) * SLOT_ALIGN
    nwin = jnp.max((hi - start + win - 1) // win, axis=1)
    return jnp.transpose(start, (0, 2, 1)).reshape(-1), nwin.reshape(-1)


def _window(st_ref, idx, w, win, cap):
    first = st_ref[idx] + w * win
    return first, pl.multiple_of(jnp.minimum(first, cap - win), SLOT_ALIGN)


def _gather_kernel(st_ref, nw_ref, pos_ref, h_ref, o_ref, *, win):
    b, j, nch = pl.program_id(0), pl.program_id(1), pl.num_programs(1)
    ne, cap, _ = o_ref.shape
    tk = h_ref.shape[0]

    @pl.when(j == 0)
    def _():
        o_ref[...] = jnp.zeros_like(o_ref)

    pos = pos_ref[...]
    h = h_ref[...]
    row = lax.broadcasted_iota(I32, (win, tk), 0)

    def window(w, carry):
        starts, lhs = [], []
        for e in range(ne):
            first, start = _window(st_ref, (b * nch + j) * ne + e, w, win, cap)
            slot = row + start
            hit = (pos[e:e + 1, :] == slot) & (slot >= first)
            lhs.append(jnp.where(hit, 1.0, 0.0).astype(BF16))
            starts.append(start)
        res = jnp.dot(jnp.concatenate(lhs, axis=0), h, preferred_element_type=F32)
        for e, start in enumerate(starts):
            o_ref[e, pl.ds(start, win), :] += res[e * win:(e + 1) * win].astype(o_ref.dtype)
        return carry

    lax.fori_loop(0, nw_ref[b * nch + j], window, 0)


def moe_gather(pos_em, h2, base, cap, tk=256, win=64):
    b, ne, s = pos_em.shape
    d = h2.shape[2]
    starts, nwin = _slot_windows(base, cap, tk, win)
    return pl.pallas_call(
        functools.partial(_gather_kernel, win=win),
        grid_spec=pltpu.PrefetchScalarGridSpec(
            num_scalar_prefetch=2,
            grid=(b, s // tk),
            in_specs=[pl.BlockSpec((None, ne, tk), lambda i, j, st, nw: (i, 0, j)),
                      pl.BlockSpec((None, tk, d), lambda i, j, st, nw: (i, j, 0))],
            out_specs=pl.BlockSpec((ne, cap, d), lambda i, j, st, nw: (0, i, 0))),
        out_shape=jax.ShapeDtypeStruct((ne, b * cap, d), BF16),
        compiler_params=_cparams(("arbitrary", "arbitrary"), VMEM_LIMIT_BYTES),
        name="moe_gather",
    )(starts, nwin, pos_em, h2)


def _ffn_kernel(x_ref, wg_ref, wu_ref, wd_ref, o_ref, acc_ref, *, sub):
    f = pl.program_id(2)

    @pl.when(f == 0)
    def _():
        acc_ref[...] = jnp.zeros_like(acc_ref)

    wg = wg_ref[...].astype(BF16)
    wu = wu_ref[...].astype(BF16)
    wd = wd_ref[...].astype(BF16)
    for r in range(x_ref.shape[0] // sub):
        rows = pl.ds(r * sub, sub)
        x = x_ref[rows, :]
        g = jnp.dot(x, wg, preferred_element_type=F32)
        u = jnp.dot(x, wu, preferred_element_type=F32)
        h = (_silu(g) * u).astype(BF16)
        acc_ref[rows, :] += jnp.dot(h, wd, preferred_element_type=F32)

    @pl.when(f == pl.num_programs(2) - 1)
    def _():
        o_ref[...] = acc_ref[...].astype(o_ref.dtype)


def moe_ffn(xe, w_gate, w_up, w_down, tm=2048, tf=256, sub=512):
    ne, m, d = xe.shape
    ff = w_gate.shape[2]
    tm = min(tm, m)
    return pl.pallas_call(
        functools.partial(_ffn_kernel, sub=min(sub, tm)),
        grid=(ne, m // tm, ff // tf),
        in_specs=[pl.BlockSpec((None, tm, d), lambda e, i, f: (e, i, 0)),
                  pl.BlockSpec((None, d, tf), lambda e, i, f: (e, 0, f)),
                  pl.BlockSpec((None, d, tf), lambda e, i, f: (e, 0, f)),
                  pl.BlockSpec((None, tf, d), lambda e, i, f: (e, f, 0))],
        out_specs=pl.BlockSpec((None, tm, d), lambda e, i, f: (e, i, 0)),
        out_shape=jax.ShapeDtypeStruct((ne, m, d), BF16),
        scratch_shapes=[pltpu.VMEM((tm, d), F32)],
        compiler_params=_cparams(("arbitrary", "arbitrary", "arbitrary"), VMEM_LIMIT_BYTES),
        name="moe_ffn",
    )(xe, w_gate, w_up, w_down)


def _combine_kernel(st_ref, nw_ref, pos_ref, gate_ref, ye_ref, x1_ref, g2_ref, fg_ref, o_ref, acc_ref,
                    *, win):
    b, j, nt = pl.program_id(0), pl.program_id(1), pl.num_programs(1)
    tt, ne = pos_ref.shape
    cap = ye_ref.shape[1]
    acc_ref[...] = jnp.zeros_like(acc_ref)
    pos = pos_ref[...]
    gate = gate_ref[...]
    col = lax.broadcasted_iota(I32, (tt, win), 1)

    def window(w, carry):
        acc = None
        for p in range(ne // 2):
            lhs, rhs = [], []
            for e in (2 * p, 2 * p + 1):
                first, start = _window(st_ref, (b * nt + j) * ne + e, w, win, cap)
                slot = col + start
                hit = (pos[:, e:e + 1] == slot) & (slot >= first)
                lhs.append(jnp.where(hit, gate[:, e:e + 1], 0.0).astype(BF16))
                rhs.append(ye_ref[e, pl.ds(start, win), :])
            part = jnp.dot(jnp.concatenate(lhs, axis=1), jnp.concatenate(rhs, axis=0),
                           preferred_element_type=F32)
            acc = part if acc is None else acc + part
        acc_ref[...] += acc
        return carry

    lax.fori_loop(0, nw_ref[b * nt + j], window, 0)
    xo = x1_ref[...] + g2_ref[...] * acc_ref[...]
    o_ref[...] = _rmsnorm(xo, fg_ref[...])


def moe_combine(pos_tm, gate_tm, base, ye, x1, g2, final_g, cap, tt=512, win=128):
    b, s, ne = pos_tm.shape
    d = x1.shape[2]
    starts, nwin = _slot_windows(base, cap, tt, win)
    tok = lambda n: pl.BlockSpec((None, tt, n), lambda i, j, st, nw: (i, j, 0))
    return pl.pallas_call(
        functools.partial(_combine_kernel, win=win),
        grid_spec=pltpu.PrefetchScalarGridSpec(
            num_scalar_prefetch=2,
            grid=(b, s // tt),
            in_specs=[tok(ne), tok(ne),
                      pl.BlockSpec((ne, cap, d), lambda i, j, st, nw: (0, i, 0)),
                      tok(d),
                      pl.BlockSpec((None, 1, d), lambda i, j, st, nw: (i, 0, 0)),
                      pl.BlockSpec((1, d), lambda i, j, st, nw: (0, 0))],
            out_specs=tok(d),
            scratch_shapes=[pltpu.VMEM((tt, d), F32)]),
        out_shape=jax.ShapeDtypeStruct((b, s, d), F32),
        compiler_params=_cparams(("arbitrary", "arbitrary"), VMEM_LIMIT_BYTES),
        name="moe_combine",
    )(starts, nwin, pos_tm, gate_tm, ye, x1, g2, final_g.reshape(1, d))


def _layer(x, ctx, mods, norm1_g, norm2_g, w_in, w_out, conv_w, conv_b, filt, hy_bias, s5p,
           s5_c_re, s5_c_im, s5_d, s5_glu_w, s5_glu_b, router_w, ex_w_gate, ex_w_up, ex_w_down,
           final_g):
    b, s, d = x.shape
    n_order, hw = hy_bias.shape
    hy_cols = (n_order + 1) * hw
    sw = w_in.shape[1] - hy_cols
    rows = s // GRID_W
    ne = router_w.shape[1]
    cap = CAPACITY_FACTOR * s // ne

    per_b = lambda k: mods[:b, k * d:(k + 1) * d].reshape(b, 1, d)
    ctx_v = lambda k: jnp.broadcast_to(mods[b:b + 1, k * d:(k + 1) * d].reshape(1, 1, d), (b, 1, d))
    sh1, sc1, g1, sh2, sc2, g2 = [per_b(k) for k in range(N_MOD)]

    w_in_bf = w_in.astype(BF16)
    (u_ctx,) = inproj(ctx, norm1_g, ctx_v(0), ctx_v(1), w_in_bf, 0, w_cols=(hy_cols // sw, sw))
    z_hy, u = inproj(x, norm1_g, sh1, sc1, w_in_bf, hy_cols, conv=(conv_w, conv_b, GRID_W))

    hraw = hyena_filter_mlp(s, *filt)
    kf = hyena_filter_spectra(hraw, n_order, hw)
    y1 = hyena_conv(z_hy, 0, z_hy, 1, kf, hy_bias, 0)
    hy = hyena_conv(y1, 0, z_hy, 2, kf, hy_bias, 1)

    s5w = s5_chunk_weights(*s5p, s5_c_re, s5_c_im, sw // LANES, S5_CHUNK)
    yf, yb = s5_scan(u_ctx, u, s5w, b)

    x1, h2, logits = mixer_tail(x, hy, yf, yb, u, s5_d, s5_glu_w, s5_glu_b, w_out, g1, norm2_g,
                                sh2, sc2, router_w)
    pos_em, pos_tm, gate_tm, base = route(logits, cap)
    xe = moe_gather(pos_em, h2, base, cap)
    ye = moe_ffn(xe, ex_w_gate, ex_w_up, ex_w_down)
    return moe_combine(pos_tm, gate_tm, base, ye, x1, g2, final_g, cap)


def kernel(x, c, ctx, c_ctx, mod_w, mod_b, norm1_g, norm2_g, w_in, w_out, conv_w, conv_b, hy_w1, hy_b1, hy_freq, hy_w2, hy_b2, hy_w3, hy_b3, hy_bias, s5_lam_re, s5_lam_im, s5_log_step, s5_b_re, s5_b_im, s5_c_re, s5_c_im, s5_d, s5_glu_w, s5_glu_b, router_w, ex_w_gate, ex_w_up, ex_w_down, final_g):
    depth = mod_w.shape[0]
    assert depth == 1, "context-token updates of non-final layers are not implemented"
    b, _, d = x.shape
    l = 0
    pad = (-(b + 1)) % SUBLANES
    cond = jnp.concatenate([c, c_ctx[None], jnp.zeros((pad, d), F32)], axis=0)
    mods = adaln_mods(cond, mod_w[l], mod_b[l])
    filt = (hy_w1[l], hy_b1[l], hy_freq[l], hy_w2[l], hy_b2[l], hy_w3[l], hy_b3[l])
    s5p = (s5_lam_re[l], s5_lam_im[l], s5_log_step[l], s5_b_re[l], s5_b_im[l])
    return _layer(x, ctx, mods, norm1_g[l], norm2_g[l], w_in[l], w_out[l], conv_w[l], conv_b[l],
                  filt, hy_bias[l], s5p, s5_c_re[l], s5_c_im[l], s5_d[l], s5_glu_w[l], s5_glu_b[l],
                  router_w[l], ex_w_gate[l], ex_w_up[l], ex_w_down[l], final_g)
```

```python
import functools
import math

import numpy as np
import jax
import jax.numpy as jnp
from jax import lax
from jax.experimental import pallas as pl
from jax.experimental.pallas import tpu as pltpu

F32 = jnp.float32
BF16 = jnp.bfloat16
I32 = jnp.int32
HIGHEST = lax.Precision.HIGHEST

SUBLANES = 8
LANES = 128
VMEM_LIMIT_BYTES = 58 * 1024 * 1024

GRID_W = 64
N_MOD = 6
NORM_EPS = 1e-6
POS_BANDS = 16
DECAY_FAST = 0.3
DECAY_SLOW = 1.5
DECAY_TARGET = 1e-2
CAPACITY_FACTOR = 2

DFT_N1 = 64
DFT_N2 = 128
DFT_K1 = DFT_N1 // 2 + 1
DFT_ROWS = 2 * DFT_K1

S5_CHUNK = 4


def _cparams(sem, vmem=None):
    return pltpu.CompilerParams(dimension_semantics=sem, vmem_limit_bytes=vmem)


def _silu(x):
    return x * (1.0 / (1.0 + jnp.exp(-x)))


def _rmsnorm(x, g):
    ms = jnp.mean(x * x, axis=-1, keepdims=True)
    return x * lax.rsqrt(ms + NORM_EPS) * g


def _adaln_kernel(c_ref, w_ref, b_ref, o_ref):
    s = _silu(c_ref[...])
    o_ref[...] = jnp.dot(s, w_ref[...], precision=HIGHEST, preferred_element_type=F32) + b_ref[...]


def adaln_mods(cond, mod_w, mod_b, tn=1536):
    rows, d = cond.shape
    n = mod_w.shape[1]
    return pl.pallas_call(
        _adaln_kernel,
        grid=(n // tn,),
        in_specs=[pl.BlockSpec((rows, d), lambda j: (0, 0)),
                  pl.BlockSpec((d, tn), lambda j: (0, j)),
                  pl.BlockSpec((1, tn), lambda j: (0, j))],
        out_specs=pl.BlockSpec((rows, tn), lambda j: (0, j)),
        out_shape=jax.ShapeDtypeStruct((rows, n), F32),
        compiler_params=_cparams(("arbitrary",)),
        name="adaln",
    )(cond, mod_w, mod_b.reshape(1, n))


def _short_conv(z, w_ref, b_ref, row_len):
    length = z.shape[0]
    pos = lax.broadcasted_iota(I32, (length, 1), 0) % row_len
    zm = jnp.where(pos == 0, 0.0, pltpu.roll(z, 1, 0))
    zp = jnp.where(pos == row_len - 1, 0.0, pltpu.roll(z, length - 1, 0))
    w = w_ref[...]
    return zm * w[0:1, :] + z * w[1:2, :] + zp * w[2:3, :] + b_ref[...]


@functools.lru_cache(maxsize=None)
def _perm_time_major(batch, tt):
    n = batch * tt
    p = np.zeros((n, n), np.float32)
    t, b = np.meshgrid(np.arange(tt), np.arange(batch), indexing='ij')
    p[(t * batch + b).ravel(), (b * tt + t).ravel()] = 1.0
    return jnp.asarray(p, dtype=BF16)


def _inproj_kernel(x_ref, g_ref, sh_ref, sc_ref, w_ref, perm_ref, *refs, n_tok, row_len):
    bsz, tt, d = x_ref.shape
    h = _rmsnorm(x_ref[...], g_ref[...])
    h = h * (1.0 + sc_ref[...]) + sh_ref[...]
    z = jnp.dot(h.reshape(bsz * tt, d).astype(BF16), w_ref[...], preferred_element_type=F32)
    if n_tok:
        cw_ref, cb_ref, tok_ref, tm_ref = refs
        tok_ref[...] = _short_conv(z[:, :n_tok], cw_ref, cb_ref, row_len).reshape(bsz, tt, n_tok)
    else:
        (tm_ref,) = refs
    u = z[:, n_tok:].astype(BF16)
    tm_ref[...] = jnp.dot(perm_ref[...], u, preferred_element_type=F32).astype(BF16)


def inproj(x, g, shift, scale, w_bf16, n_tok, conv=None, w_cols=None, tt=64):
    b, s, d = x.shape
    col, n = (0, w_bf16.shape[1]) if w_cols is None else w_cols
    perm = _perm_time_major(b, tt)
    const = lambda a: (pl.BlockSpec((d, n), lambda j: (0, col)) if a is w_bf16
                       else pl.BlockSpec(a.shape, lambda j: (0,) * a.ndim))
    args = [x, g.reshape(1, d), shift, scale, w_bf16, perm]
    out_specs = [pl.BlockSpec((tt * b, n - n_tok), lambda j: (j, 0))]
    out_shape = [jax.ShapeDtypeStruct((s * b, n - n_tok), BF16)]
    row_len = 1
    if n_tok:
        cw, cb, row_len = conv
        assert tt % row_len == 0
        args += [cw, cb.reshape(1, -1)]
        out_specs.insert(0, pl.BlockSpec((b, tt, n_tok), lambda j: (0, j, 0)))
        out_shape.insert(0, jax.ShapeDtypeStruct((b, s, n_tok), F32))
    return pl.pallas_call(
        functools.partial(_inproj_kernel, n_tok=n_tok, row_len=row_len),
        grid=(s // tt,),
        in_specs=[pl.BlockSpec((b, tt, d), lambda j: (0, j, 0))] + [const(a) for a in args[1:]],
        out_specs=out_specs,
        out_shape=out_shape,
        compiler_params=_cparams(("arbitrary",)),
        name="inproj",
    )(*args)


def _filt_mlp_kernel(w1t_ref, w1c_ref, w1s_ref, b1_ref, fr_ref, w2_ref, b2_ref, w3_ref, b3_ref,
                     o_ref, *, length, tl):
    i0 = pl.program_id(0) * tl
    idx = (lax.broadcasted_iota(I32, (1, tl), 1) + i0).astype(F32)
    t = idx / float(length - 1)
    omega = (2.0 * math.pi) * idx / float(length)
    fstep = ((POS_BANDS - 1) - 1e-4) / (POS_BANDS - 1)
    f = 1e-4 + lax.broadcasted_iota(I32, (POS_BANDS, 1), 0).astype(F32) * fstep
    arg = f * omega
    pre = (w1t_ref[...] * t
           + jnp.dot(w1c_ref[...], jnp.cos(arg), precision=HIGHEST, preferred_element_type=F32)
           - jnp.dot(w1s_ref[...], jnp.sin(arg), precision=HIGHEST, preferred_element_type=F32)
           + b1_ref[...])
    fr = fr_ref[...]
    h = jnp.sin(fr[:, 0:1] * pre)
    h = jnp.sin(fr[:, 1:2] * (jnp.dot(w2_ref[...], h, precision=HIGHEST,
                                      preferred_element_type=F32) + b2_ref[...]))
    o_ref[...] = jnp.dot(h.T, w3_ref[...], precision=HIGHEST, preferred_element_type=F32) + b3_ref[...]


def hyena_filter_mlp(length, w1, b1, freq, w2, b2, w3, b3, tl=512):
    fw = w1.shape[1]
    n = w3.shape[1]
    full = lambda shape: pl.BlockSpec(shape, lambda i: (0, 0))
    return pl.pallas_call(
        functools.partial(_filt_mlp_kernel, length=length, tl=tl),
        grid=(length // tl,),
        in_specs=[full((fw, 1)), full((fw, POS_BANDS)), full((fw, POS_BANDS)), full((fw, 1)),
                  full((fw, 2)), full((fw, fw)), full((fw, 1)), full((fw, n)), full((1, n))],
        out_specs=pl.BlockSpec((tl, n), lambda i: (i, 0)),
        out_shape=jax.ShapeDtypeStruct((length, n), F32),
        compiler_params=_cparams(("arbitrary",)),
        name="hyena_filter_mlp",
    )(w1[0:1].T, w1[1:1 + POS_BANDS].T, w1[1 + POS_BANDS:].T, b1.reshape(fw, 1), freq.T, w2.T,
      b2.reshape(fw, 1), w3, b3.reshape(1, n))


@functools.lru_cache(maxsize=None)
def _dft_tables():
    n1n, n2n, k1n = DFT_N1, DFT_N2, DFT_K1
    n = n1n * n2n
    half = n1n // 2
    k1 = np.arange(k1n)[:, None]
    n1 = np.arange(half)[None, :]
    th = 2.0 * np.pi * k1 * n1 / n1n
    f1 = np.zeros((DFT_ROWS, half))
    f1[0::2] = np.cos(th)
    f1[1::2] = -np.sin(th)
    wgt = np.where((k1 == 0) | (k1 == half), 1.0, 2.0)
    g1 = np.zeros((half, DFT_ROWS))
    g1[:, 0::2] = (wgt * np.cos(th)).T / n
    g1[:, 1::2] = (-wgt * np.sin(th)).T / n
    eye = np.eye(SUBLANES)
    fk = np.kron(f1, eye)
    gk = np.kron(g1, eye)
    k2 = np.arange(n2n)[:, None]
    n2 = np.arange(n2n)[None, :]
    f3 = np.zeros((k1n, 2 * n2n, 2 * n2n))
    for kk in range(k1n):
        ph = 2.0 * np.pi * n2 * (n1n * k2 + kk) / n
        tr, ti = np.cos(ph), -np.sin(ph)
        f3[kk, :n2n, :n2n] = tr
        f3[kk, :n2n, n2n:] = -ti
        f3[kk, n2n:, :n2n] = ti
        f3[kk, n2n:, n2n:] = tr
    g3 = np.transpose(f3, (0, 2, 1))
    to = lambda a: jnp.asarray(a, dtype=F32).astype(BF16)
    return to(fk), to(gk), to(f3), to(g3)


def _dft_stage1(src_ref, a_ref, fk_ref):
    half = DFT_N1 // 2

    def body(m, carry):
        sub = pl.ds(pl.multiple_of(m * SUBLANES, SUBLANES), SUBLANES)
        rows = [src_ref.at[pl.ds(DFT_N2 * n1, DFT_N2)][sub, :] for n1 in range(half)]
        rhs = jnp.concatenate(rows, axis=0).astype(BF16)
        out = jnp.dot(fk_ref[...], rhs, preferred_element_type=F32)
        for j in range(DFT_ROWS):
            a_ref.at[pl.ds(DFT_N2 * j, DFT_N2)][sub, :] = out[SUBLANES * j:SUBLANES * (j + 1)]
        return carry

    lax.fori_loop(0, DFT_N2 // SUBLANES, body, 0, unroll=2)


def _dft_stage3(a_ref, f3_ref, k1):
    r0 = pl.multiple_of(k1 * (2 * DFT_N2), 2 * DFT_N2)
    a = a_ref[pl.ds(r0, 2 * DFT_N2), :].astype(BF16)
    x = jnp.dot(f3_ref[k1], a, preferred_element_type=F32)
    return x[:DFT_N2], x[DFT_N2:]


def _filt_spec_kernel(hf_ref, hb_ref, fk_ref, f3_ref, o_ref, src_ref, af_ref, ab_ref, *, length):
    c = hf_ref.shape[1]
    cb = pl.program_id(1)
    hw = pl.num_programs(1) * c
    row = lax.broadcasted_iota(I32, (length, 1), 0)
    t = row.astype(F32) / float(length - 1)
    ch = (lax.broadcasted_iota(I32, (1, c), 1) + cb * c).astype(F32)
    d0 = math.log(DECAY_TARGET) / DECAY_FAST
    d1 = math.log(DECAY_TARGET) / DECAY_SLOW
    deltas = jnp.abs(d0 + ch * ((d1 - d0) / float(hw - 1)))
    decay = jnp.exp(-t * deltas)
    fwd = hf_ref[...] * decay
    bwd = jnp.where(row == 0, 0.0, hb_ref[...] * decay)
    inv = 1.0 / (jnp.sum(jnp.abs(fwd), axis=0, keepdims=True)
                 + jnp.sum(jnp.abs(bwd), axis=0, keepdims=True))
    src_ref[...] = fwd
    _dft_stage1(src_ref, af_ref, fk_ref)
    src_ref[...] = bwd
    _dft_stage1(src_ref, ab_ref, fk_ref)

    def body(k1, carry):
        fr, fi = _dft_stage3(af_ref, f3_ref, k1)
        br, bi = _dft_stage3(ab_ref, f3_ref, k1)
        o_ref[k1, 0] = ((fr + br) * inv).astype(o_ref.dtype)
        o_ref[k1, 1] = ((fi - bi) * inv).astype(o_ref.dtype)
        return carry

    lax.fori_loop(0, DFT_K1, body, 0, unroll=3)


def hyena_filter_spectra(hraw, n_order, width, c_blk=256):
    length = hraw.shape[0]
    assert 2 * length == DFT_N1 * DFT_N2
    fk, _, f3, _ = _dft_tables()
    ncb = width // c_blk
    return pl.pallas_call(
        functools.partial(_filt_spec_kernel, length=length),
        grid=(n_order, ncb),
        in_specs=[pl.BlockSpec((length, c_blk), lambda o, j: (0, o * 2 * ncb + j)),
                  pl.BlockSpec((length, c_blk), lambda o, j: (0, o * 2 * ncb + ncb + j)),
                  pl.BlockSpec(fk.shape, lambda o, j: (0, 0)),
                  pl.BlockSpec(f3.shape, lambda o, j: (0, 0, 0))],
        out_specs=pl.BlockSpec((None, DFT_K1, 2, DFT_N2, c_blk), lambda o, j: (o, 0, 0, 0, j)),
        out_shape=jax.ShapeDtypeStruct((n_order, DFT_K1, 2, DFT_N2, width), BF16),
        scratch_shapes=[pltpu.VMEM((length, c_blk), F32),
                        pltpu.VMEM((DFT_ROWS * DFT_N2, c_blk), F32),
                        pltpu.VMEM((DFT_ROWS * DFT_N2, c_blk), F32)],
        compiler_params=_cparams(("arbitrary", "arbitrary"), VMEM_LIMIT_BYTES),
        name="hyena_filter_spectrum",
    )(hraw, hraw, fk, f3)


def _hyena_conv_kernel(s_ref, m_ref, kf_ref, bias_ref, fk_ref, f3_ref, g3_ref, gk_ref, o_ref, a_ref,
                       *, group):
    half = DFT_N1 // 2
    blk_rows = 2 * DFT_N2
    _dft_stage1(s_ref, a_ref, fk_ref)

    def body3(i, carry):
        k1s = [i * group + q for q in range(group)]
        r0s = [pl.multiple_of(k1 * blk_rows, blk_rows) for k1 in k1s]
        blocks = [a_ref[pl.ds(r0, blk_rows), :].astype(BF16) for r0 in r0s]
        outs = []
        for k1, a in zip(k1s, blocks):
            x = jnp.dot(f3_ref[k1], a, preferred_element_type=F32)
            xr, xi = x[:DFT_N2], x[DFT_N2:]
            kr = kf_ref[k1, 0].astype(F32)
            ki = kf_ref[k1, 1].astype(F32)
            y = jnp.concatenate([xr * kr - xi * ki, xr * ki + xi * kr], axis=0).astype(BF16)
            outs.append(jnp.dot(g3_ref[k1], y, preferred_element_type=F32))
        for r0, o in zip(r0s, outs):
            a_ref[pl.ds(r0, blk_rows), :] = o
        return carry

    lax.fori_loop(0, DFT_K1 // group, body3, 0)
    bias = bias_ref[...]

    def body1(m, carry):
        sub = pl.ds(pl.multiple_of(m * SUBLANES, SUBLANES), SUBLANES)
        blk = [a_ref.at[pl.ds(DFT_N2 * j, DFT_N2)][sub, :] for j in range(DFT_ROWS)]
        rhs = jnp.concatenate(blk, axis=0).astype(BF16)
        out = jnp.dot(gk_ref[...], rhs, preferred_element_type=F32)
        for n1 in range(half):
            blk_n1 = pl.ds(DFT_N2 * n1, DFT_N2)
            conv = out[SUBLANES * n1:SUBLANES * (n1 + 1)]
            o_ref.at[blk_n1][sub, :] = m_ref.at[blk_n1][sub, :] * (
                conv + s_ref.at[blk_n1][sub, :] * bias)
        return carry

    lax.fori_loop(0, DFT_N2 // SUBLANES, body1, 0, unroll=2)


def hyena_conv(sig, sig_col, mul, mul_col, kf, bias, order, c_blk=256, group=11):
    b, length, _ = sig.shape
    width = kf.shape[-1]
    ncb = width // c_blk
    assert DFT_K1 % group == 0
    fk, gk, f3, g3 = _dft_tables()
    once = pl.Buffered(1)
    const2 = lambda a: pl.BlockSpec(a.shape, lambda j, i: (0, 0), pipeline_mode=once)
    const3 = lambda a: pl.BlockSpec(a.shape, lambda j, i: (0, 0, 0), pipeline_mode=once)
    return pl.pallas_call(
        functools.partial(_hyena_conv_kernel, group=group),
        grid=(ncb, b),
        in_specs=[pl.BlockSpec((None, length, c_blk), lambda j, i: (i, 0, sig_col * ncb + j)),
                  pl.BlockSpec((None, length, c_blk), lambda j, i: (i, 0, mul_col * ncb + j)),
                  pl.BlockSpec((None, DFT_K1, 2, DFT_N2, c_blk), lambda j, i: (order, 0, 0, 0, j),
                               pipeline_mode=once),
                  pl.BlockSpec((None, 1, c_blk), lambda j, i: (order, 0, j), pipeline_mode=once),
                  const2(fk), const3(f3), const3(g3), const2(gk)],
        out_specs=pl.BlockSpec((None, length, c_blk), lambda j, i: (i, 0, j)),
        out_shape=jax.ShapeDtypeStruct((b, length, width), F32),
        scratch_shapes=[pltpu.VMEM((DFT_ROWS * DFT_N2, c_blk), F32)],
        compiler_params=_cparams(("arbitrary", "arbitrary"), VMEM_LIMIT_BYTES),
        name="hyena_conv",
    )(sig, mul, kf, bias.reshape(-1, 1, width), fk, f3, g3, gk)


def _s5_powers_kernel(lr_ref, li_ref, dt_ref, vr_ref, vi_ref, or_ref, oi_ref, *, n_pow, zoh):
    lr, li, dt = lr_ref[...], li_ref[...], jnp.exp(dt_ref[...])
    mag = jnp.exp(lr * dt)
    ar = mag * jnp.cos(li * dt)
    ai = mag * jnp.sin(li * dt)
    vr, vi = vr_ref[...], vi_ref[...]
    if zoh:
        den = 1.0 / (lr * lr + li * li)
        qr = ((ar - 1.0) * lr + ai * li) * den
        qi = (ai * lr - (ar - 1.0) * li) * den
        vr, vi = qr * vr - qi * vi, qr * vi + qi * vr
    for j in range(n_pow):
        or_ref[j] = vr
        oi_ref[j] = vi
        vr, vi = ar * vr - ai * vi, ar * vi + ai * vr


def s5_powers(lam_re, lam_im, log_step, v_re, v_im, state_axis, n_pow, zoh):
    nd, g, a, b = v_re.shape
    expand = (lambda x: x[..., :, None]) if state_axis == 2 else (lambda x: x[..., None, :])
    rep = lambda x: jnp.broadcast_to(expand(x), v_re.shape).reshape(nd * g, a * b)
    dt = jnp.broadcast_to(log_step[:, :, None, None], v_re.shape).reshape(nd * g, a * b)
    flat = lambda x: x.reshape(nd * g, a * b)
    shp = jax.ShapeDtypeStruct((n_pow, nd * g, a * b), F32)
    o_r, o_i = pl.pallas_call(
        functools.partial(_s5_powers_kernel, n_pow=n_pow, zoh=zoh), out_shape=[shp, shp],
        name="s5_powers",
    )(rep(lam_re), rep(lam_im), dt, flat(v_re), flat(v_im))
    un = lambda x: x.reshape(n_pow, nd, g, a, b)
    return un(o_r), un(o_i)


def _s5_taps_kernel(cr_ref, ci_ref, br_ref, bi_ref, o_ref):
    n_pow, nk = br_ref.shape[0], br_ref.shape[1]
    for j in range(n_pow):
        for k in range(nk):
            o_ref[j, k] = (jnp.dot(cr_ref[k], br_ref[j, k], preferred_element_type=F32)
                           - jnp.dot(ci_ref[k], bi_ref[j, k], preferred_element_type=F32))


def s5_taps(c_re, c_im, bbp_r, bbp_i, nk):
    nd, g, h, p = c_re.shape
    n_pow = bbp_r.shape[0]
    gpk = g // nk
    eye = jnp.eye(gpk, dtype=F32)
    bdiag = lambda c: jnp.einsum('dkahp,ab->dkahbp', c.reshape(nd, nk, gpk, h, p), eye).reshape(
        nd, nk, gpk * h, gpk * p).astype(BF16)
    flat = lambda x: jnp.transpose(x, (1, 0, 2, 3, 4)).reshape(nd, n_pow, nk, gpk * p, h).astype(BF16)
    per_d = lambda a: pl.BlockSpec((None,) + a.shape[1:], lambda d: (d,) + (0,) * (a.ndim - 1))
    args = (bdiag(c_re), bdiag(c_im), flat(bbp_r), flat(bbp_i))
    shp = jax.ShapeDtypeStruct((nd, n_pow, nk, gpk * h, h), F32)
    out = pl.pallas_call(
        _s5_taps_kernel,
        grid=(nd,),
        in_specs=[per_d(a) for a in args],
        out_specs=per_d(shp),
        out_shape=shp,
        compiler_params=_cparams(("arbitrary",)),
        name="s5_taps",
    )(*args)
    return out.reshape(nd, n_pow, g, h, h)


def _s5_scan_kernel(ucf_ref, uf_ref, ucb_ref, ub_ref, qf_ref, qb_ref, pf_ref, pb_ref, mf_ref, mb_ref,
                    lam_ref, yf_ref, yb_ref, xf_ref, xb_ref, st_ref, *, nc, kpp, tlen):
    width = uf_ref.shape[1]
    batch = st_ref.shape[3]
    nk = xf_ref.shape[1]
    ck = width // nk
    rows = xf_ref.shape[2]
    cpg = rows // batch
    i = pl.program_id(0)
    is_ctx = i < nc

    @pl.when(i == 0)
    def _():
        st_ref[...] = jnp.zeros_like(st_ref)

    pick = lambda c_ref, l_ref: jnp.where(is_ctx, c_ref[...], l_ref[...]).astype(F32).reshape(
        cpg, tlen, batch, width)
    uf = pick(ucf_ref, uf_ref)
    ub = pick(ucb_ref, ub_ref)

    def scan(x_ref, d, reverse):
        for k0 in range(0, nk, kpp):
            ks = slice(k0, k0 + kpp)
            lr, li = lam_ref[d, 0, ks], lam_ref[d, 1, ks]

            def body(c, carry, ks=ks, lr=lr, li=li):
                sr, si = carry
                cc = (cpg - 1 - c) if reverse else c
                r = pl.ds(pl.multiple_of(cc * batch, batch), batch)
                qr, qi = x_ref[0, ks, r, :], x_ref[1, ks, r, :]
                x_ref[0, ks, r, :] = sr
                x_ref[1, ks, r, :] = si
                return lr * sr - li * si + qr, lr * si + li * sr + qi

            st_ref[d, 0, ks], st_ref[d, 1, ks] = lax.fori_loop(
                0, cpg, body, (st_ref[d, 0, ks], st_ref[d, 1, ks]))

    for u, q_ref, p_ref, m_ref, x_ref, y_ref, d in ((uf, qf_ref, pf_ref, mf_ref, xf_ref, yf_ref, 0),
                                                    (ub, qb_ref, pb_ref, mb_ref, xb_ref, yb_ref, 1)):
        uks = []
        for k in range(nk):
            uk = jnp.concatenate([u[:, t, :, k * ck:(k + 1) * ck].reshape(rows, ck)
                                  for t in range(tlen)], axis=1).astype(BF16)
            uks.append(uk)
            for ri in range(2):
                x_ref[ri, k] = jnp.dot(uk, q_ref[ri, k], preferred_element_type=F32)
        scan(x_ref, d, d == 1)
        for k in range(nk):
            yk = (jnp.dot(uks[k], m_ref[k], preferred_element_type=F32)
                  + jnp.dot(x_ref[0, k].astype(BF16), p_ref[0, k], preferred_element_type=F32)
                  + jnp.dot(x_ref[1, k].astype(BF16), p_ref[1, k], preferred_element_type=F32))
            y_ref[:, k * ck:(k + 1) * ck] = jnp.stack(
                [yk[:, t * ck:(t + 1) * ck].reshape(cpg, batch, ck) for t in range(tlen)],
                axis=1).reshape(cpg * tlen * batch, ck)


def s5_chunk_weights(lam_re, lam_im, log_step, b_re, b_im, c_re, c_im, nk, tlen):
    nd, g, p, h = b_re.shape
    gpk, sk = g // nk, g * p // nk
    bb_r, bb_i = s5_powers(lam_re, lam_im, log_step, b_re, b_im, 2, tlen, True)
    cl_r, cl_i = s5_powers(lam_re, lam_im, log_step, c_re, c_im, 3, tlen + 1, False)
    ones = jnp.ones((nd, g, p, 2), F32)
    pw_r, pw_i = s5_powers(lam_re, lam_im, log_step, ones, jnp.zeros_like(ones), 2, tlen + 1, False)
    taps = s5_taps(c_re, c_im, bb_r, bb_i, nk)
    dirs = np.arange(nd)[:, None]
    step = np.arange(tlen)[None, :]
    jq = np.where(dirs == 0, tlen - 1 - step, step)
    jp = np.where(dirs == 0, step + 1, tlen - step)
    lag = step[0][None, None, :] - step[0][None, :, None]
    lag = np.where(dirs[:, :, None] == 0, lag, -lag)
    ch = tlen * gpk * h
    rep_p = np.tile(np.eye(p, dtype=np.float32), (1, gpk))
    rep_h = np.einsum('tu,hi,b->thubi', np.eye(tlen), np.eye(h), np.ones(gpk)).reshape(
        tlen * h, ch).astype(np.float32)
    grp_ch = np.tile(np.repeat(np.arange(gpk), h), tlen)
    grp_st = np.repeat(np.arange(gpk), p)
    qsel = jnp.stack([bb_r, bb_i])[:, jq, dirs].reshape(2, nd, tlen, nk, gpk, p, h)
    qrows = jnp.transpose(qsel, (1, 0, 3, 2, 4, 6, 5)).reshape(nd, 2, nk, ch, p)
    q = jnp.where(grp_ch[:, None] == grp_st[None, :], qrows @ rep_p, 0.0)
    psel = jnp.stack([cl_r, -cl_i])[:, jp, dirs].reshape(2, nd, tlen, nk, gpk, h, p)
    prows = jnp.transpose(psel, (1, 0, 3, 4, 6, 2, 5)).reshape(nd, 2, nk, sk, tlen * h)
    pm = jnp.where(grp_st[:, None] == grp_ch[None, :], prows @ rep_h, 0.0)
    tsel = jnp.where((lag >= 0)[..., None, None, None], taps[dirs[:, :, None], np.maximum(lag, 0)], 0.0)
    tsel = tsel.reshape(nd, tlen, tlen, nk, gpk, h, h)
    trows = jnp.transpose(tsel, (0, 3, 1, 4, 6, 2, 5)).reshape(nd, nk, ch, tlen * h)
    m = jnp.where(grp_ch[:, None] == grp_ch[None, :], trows @ rep_h, 0.0)
    decay = jnp.stack([pw_r[tlen, ..., 0], pw_i[tlen, ..., 0]], axis=1).reshape(nd, 2, nk, 1, sk)
    return q.astype(BF16), pm.astype(BF16), m.astype(BF16), decay


def s5_scan(uc, u, weights, batch, steps=128, kpp=2):
    q, pm, m, decay = weights
    width = u.shape[1]
    nk, sk = q.shape[2], q.shape[4]
    tlen = m.shape[2] * nk // width
    cpg = steps // tlen
    rpc = steps * batch
    nc, nl = uc.shape[0] // rpc, u.shape[0] // rpc
    assert steps % tlen == 0 and uc.shape[0] % rpc == 0 and u.shape[0] % rpc == 0
    lam = jnp.broadcast_to(decay, (2, 2, nk, batch, sk))
    blk = lambda f: pl.BlockSpec((rpc, width), lambda i: (f(i), 0))
    lat_f = lambda i: jnp.maximum(i - nc, 0)
    lat_b = lambda i: jnp.clip(nl - 1 - i + nc, 0, nl - 1)
    once = pl.Buffered(1)
    par = lambda a, d: pl.BlockSpec((None,) + a.shape[1:], lambda i: (d,) + (0,) * (a.ndim - 1),
                                    pipeline_mode=once)
    shp = jax.ShapeDtypeStruct(u.shape, F32)
    return pl.pallas_call(
        functools.partial(_s5_scan_kernel, nc=nc, kpp=kpp, tlen=tlen),
        grid=(nc + nl,),
        in_specs=[blk(lambda i: jnp.minimum(i, nc - 1)), blk(lat_f),
                  blk(lambda i: jnp.maximum(nc - 1 - i, 0)), blk(lat_b),
                  par(q, 0), par(q, 1), par(pm, 0), par(pm, 1), par(m, 0), par(m, 1),
                  pl.BlockSpec(lam.shape, lambda i: (0, 0, 0, 0, 0), pipeline_mode=once)],
        out_specs=[blk(lat_f), blk(lat_b)],
        out_shape=[shp, shp],
        scratch_shapes=[pltpu.VMEM((2, nk, cpg * batch, sk), F32),
                        pltpu.VMEM((2, nk, cpg * batch, sk), F32),
                        pltpu.VMEM((2, 2, nk, batch, sk), F32)],
        compiler_params=_cparams(("arbitrary",), VMEM_LIMIT_BYTES),
        name="s5_scan",
    )(uc, u, uc, u, q, q, pm, pm, m, m, lam)


def _mixer_tail_kernel(x_ref, hy_ref, yf_ref, yb_ref, u_ref, d_ref, gw_ref, gb_ref, woh_ref, wos_ref,
                       g1_ref, n2_ref, sh2_ref, sc2_ref, rwt_ref, perm_ref, x1_ref, h2_ref, lg_ref):
    bsz, tt, d = x_ref.shape
    rows = bsz * tt
    y = yf_ref[...] + yb_ref[...] + d_ref[...] * u_ref[...].astype(F32)
    y = 0.5 * y * (1.0 + jnp.tanh(math.sqrt(2.0 / math.pi) * (y + 0.044715 * (y * y * y))))
    gate = jnp.dot(y.astype(BF16), gw_ref[...], preferred_element_type=F32) + gb_ref[...]
    s5 = (y * (1.0 / (1.0 + jnp.exp(-gate)))).astype(BF16)
    s5 = jnp.dot(perm_ref[...], s5, preferred_element_type=F32).astype(BF16)
    hy = hy_ref[...].reshape(rows, hy_ref.shape[2]).astype(BF16)
    mix = (jnp.dot(hy, woh_ref[...], preferred_element_type=F32)
           + jnp.dot(s5, wos_ref[...], preferred_element_type=F32))
    x1 = x_ref[...] + g1_ref[...] * mix.reshape(bsz, tt, d)
    x1_ref[...] = x1
    h2 = _rmsnorm(x1, n2_ref[...]) * (1.0 + sc2_ref[...]) + sh2_ref[...]
    h2_hi = h2.astype(BF16)
    h2_ref[...] = h2_hi
    rw = rwt_ref[...]
    rw_hi = rw.astype(BF16)
    rw_lo = (rw - rw_hi.astype(F32)).astype(BF16)
    h2_hi = h2_hi.reshape(rows, d)
    h2_lo = (h2.reshape(rows, d) - h2_hi.astype(F32)).astype(BF16)
    nt = lambda a, b: lax.dot_general(a, b, (((1,), (1,)), ((), ())), preferred_element_type=F32)
    lg = nt(rw_hi, h2_hi) + (nt(rw_hi, h2_lo) + nt(rw_lo, h2_hi))
    for i in range(bsz):
        lg_ref[i] = lg[:, i * tt:(i + 1) * tt]


def mixer_tail(x, hy, yf, yb, u_tm, s5_d, glu_w, glu_b, w_out, g1, norm2_g, sh2, sc2, router_w,
               tt=128):
    b, s, d = x.shape
    hw = hy.shape[2]
    sw = u_tm.shape[1]
    ne = router_w.shape[1]
    perm = _perm_time_major(b, tt).T
    once = pl.Buffered(1)
    tok = lambda n: pl.BlockSpec((b, tt, n), lambda j: (0, j, 0))
    tmj = pl.BlockSpec((tt * b, sw), lambda j: (j, 0))
    const = lambda a: pl.BlockSpec(a.shape, lambda j: (0,) * a.ndim, pipeline_mode=once)
    consts = [s5_d.reshape(1, sw), glu_w.astype(BF16), glu_b.reshape(1, sw), w_out[:hw].astype(BF16),
              w_out[hw:].astype(BF16), g1, norm2_g.reshape(1, d), sh2, sc2, router_w.T, perm]
    return pl.pallas_call(
        _mixer_tail_kernel,
        grid=(s // tt,),
        in_specs=[tok(d), tok(hw), tmj, tmj, tmj] + [const(a) for a in consts],
        out_specs=[tok(d), tok(d), pl.BlockSpec((b, ne, tt), lambda j: (0, 0, j))],
        out_shape=[jax.ShapeDtypeStruct((b, s, d), F32), jax.ShapeDtypeStruct((b, s, d), BF16),
                   jax.ShapeDtypeStruct((b, ne, s), F32)],
        compiler_params=_cparams(("arbitrary",), VMEM_LIMIT_BYTES),
        name="mixer_tail",
    )(x, hy, yf, yb, u_tm, *consts)


def _lane_cumsum_exclusive(x):
    rows, s = x.shape
    ii = lax.broadcasted_iota(I32, (LANES, LANES), 0)
    jj = lax.broadcasted_iota(I32, (LANES, LANES), 1)
    tri = jnp.where(ii < jj, 1.0, 0.0).astype(BF16)
    carry = jnp.zeros((rows, 1), F32)
    out, base = [], []
    for blk in range(s // LANES):
        xb = x[:, blk * LANES:(blk + 1) * LANES]
        out.append(jnp.dot(xb.astype(BF16), tri, preferred_element_type=F32) + carry)
        base.append(carry)
        carry = carry + jnp.sum(xb, axis=1, keepdims=True)
    return jnp.concatenate(out, axis=1), jnp.concatenate(base, axis=1)


def _route_kernel(lg_ref, pos_em_ref, pos_tm_ref, gate_tm_ref, base_ref, aff_ref, *, cap):
    lg = lg_ref[...]
    ne, s = lg.shape
    ex = jnp.exp(lg - jnp.max(lg, axis=0, keepdims=True))
    aff_ref[...] = ex / jnp.sum(ex, axis=0, keepdims=True)
    aff = aff_ref[...]
    count_ge = lambda v, t: jnp.sum(jnp.where(v >= t, 1.0, 0.0), axis=1, keepdims=True)

    def coarse(i, tb):
        cand = tb | jnp.left_shift(jnp.int32(1), 30 - i)
        return jnp.where(count_ge(aff, pltpu.bitcast(cand, F32)) >= cap, cand, tb)

    tb = lax.fori_loop(0, 31, coarse, jnp.zeros((ne, 1), I32))
    t_hi = pltpu.bitcast(tb, F32)
    ulp = pltpu.bitcast(tb + 1, F32) - t_hi
    resid = aff - t_hi

    def fine(j, carry):
        c, step = carry
        cand = c + step
        return jnp.where(count_ge(resid, cand) >= cap, cand, c), step * 0.5

    t_lo, _ = lax.fori_loop(0, 12, fine, (jnp.zeros((ne, 1), F32), ulp * 0.5))
    gt = resid > t_lo
    eq = resid == t_lo
    need = cap - jnp.sum(jnp.where(gt, 1.0, 0.0), axis=1, keepdims=True)
    eq_rank, _ = _lane_cumsum_exclusive(jnp.where(eq, 1.0, 0.0))
    sel = gt | (eq & (eq_rank < need))
    pos, base = _lane_cumsum_exclusive(jnp.where(sel, 1.0, 0.0))
    posf = jnp.where(sel, pos + 1.0, 0.0)
    gate = jnp.where(sel, aff, 0.0)
    pos_em_ref[...] = posf.astype(I32) - 1
    base_ref[...] = base.astype(I32)
    hi = jnp.floor(posf * (1.0 / 16.0))
    lo = posf - 16.0 * hi
    g1 = gate.astype(BF16)
    r1 = gate - g1.astype(F32)
    g2 = r1.astype(BF16)
    g3 = (r1 - g2.astype(F32)).astype(BF16)
    ii = lax.broadcasted_iota(I32, (LANES, LANES), 0)
    jj = lax.broadcasted_iota(I32, (LANES, LANES), 1)
    eye = jnp.where(ii == jj, 1.0, 0.0).astype(BF16)
    tr = lambda v: lax.dot_general(eye, v, (((1,), (1,)), ((), ())), preferred_element_type=F32)
    for blk in range(s // LANES):
        sl = slice(blk * LANES, (blk + 1) * LANES)
        pos_tm_ref[sl, :] = (16.0 * tr(hi[:, sl].astype(BF16)) + tr(lo[:, sl].astype(BF16))).astype(I32) - 1
        gate_tm_ref[sl, :] = tr(g1[:, sl]) + (tr(g2[:, sl]) + tr(g3[:, sl]))


def route(logits, cap):
    b, ne, s = logits.shape
    nb = s // LANES
    return pl.pallas_call(
        functools.partial(_route_kernel, cap=cap),
        grid=(b,),
        in_specs=[pl.BlockSpec((None, ne, s), lambda i: (i, 0, 0))],
        out_specs=[pl.BlockSpec((None, ne, s), lambda i: (i, 0, 0)),
                   pl.BlockSpec((None, s, ne), lambda i: (i, 0, 0)),
                   pl.BlockSpec((None, s, ne), lambda i: (i, 0, 0)),
                   pl.BlockSpec((None, ne, nb), lambda i: (i, 0, 0))],
        out_shape=[jax.ShapeDtypeStruct((b, ne, s), I32), jax.ShapeDtypeStruct((b, s, ne), I32),
                   jax.ShapeDtypeStruct((b, s, ne), F32), jax.ShapeDtypeStruct((b, ne, nb), I32)],
        scratch_shapes=[pltpu.VMEM((ne, s), F32)],
        compiler_params=_cparams(("arbitrary",)),
        name="route",
    )(logits)


SLOT_ALIGN = 16


def _slot_windows(base, cap, chunk, win):
    lo = base[:, :, ::chunk // LANES]
    hi = jnp.concatenate([lo[:, :, 1:], jnp.full_like(lo[:, :, :1], cap)], axis=2)
    start = (lo // SLOT_ALIGN) * SLOT_ALIGN
    nwin = jnp.max((hi - start + win - 1) // win, axis=1)
    return jnp.transpose(start, (0, 2, 1)).reshape(-1), nwin.reshape(-1)


def _window(st_ref, idx, w, win, cap):
    first = st_ref[idx] + w * win
    return first, pl.multiple_of(jnp.minimum(first, cap - win), SLOT_ALIGN)


def _gather_kernel(st_ref, nw_ref, pos_ref, h_ref, o_ref, *, win):
    b, j, nch = pl.program_id(0), pl.program_id(1), pl.num_programs(1)
    ne, cap, _ = o_ref.shape
    tk = h_ref.shape[0]

    @pl.when(j == 0)
    def _():
        o_ref[...] = jnp.zeros_like(o_ref)

    pos = pos_ref[...]
    h = h_ref[...]
    row = lax.broadcasted_iota(I32, (win, tk), 0)

    def window(w, carry):
        starts, lhs = [], []
        for e in range(ne):
            first, start = _window(st_ref, (b * nch + j) * ne + e, w, win, cap)
            slot = row + start
            hit = (pos[e:e + 1, :] == slot) & (slot >= first)
            lhs.append(jnp.where(hit, 1.0, 0.0).astype(BF16))
            starts.append(start)
        res = jnp.dot(jnp.concatenate(lhs, axis=0), h, preferred_element_type=F32)
        for e, start in enumerate(starts):
            o_ref[e, pl.ds(start, win), :] += res[e * win:(e + 1) * win].astype(o_ref.dtype)
        return carry

    lax.fori_loop(0, nw_ref[b * nch + j], window, 0)


def moe_gather(pos_em, h2, base, cap, tk=256, win=64):
    b, ne, s = pos_em.shape
    d = h2.shape[2]
    starts, nwin = _slot_windows(base, cap, tk, win)
    return pl.pallas_call(
        functools.partial(_gather_kernel, win=win),
        grid_spec=pltpu.PrefetchScalarGridSpec(
            num_scalar_prefetch=2,
            grid=(b, s // tk),
            in_specs=[pl.BlockSpec((None, ne, tk), lambda i, j, st, nw: (i, 0, j)),
                      pl.BlockSpec((None, tk, d), lambda i, j, st, nw: (i, j, 0))],
            out_specs=pl.BlockSpec((ne, cap, d), lambda i, j, st, nw: (0, i, 0))),
        out_shape=jax.ShapeDtypeStruct((ne, b * cap, d), BF16),
        compiler_params=_cparams(("arbitrary", "arbitrary"), VMEM_LIMIT_BYTES),
        name="moe_gather",
    )(starts, nwin, pos_em, h2)


def _ffn_kernel(x_ref, wg_ref, wu_ref, wd_ref, o_ref, acc_ref, *, sub):
    f = pl.program_id(2)

    @pl.when(f == 0)
    def _():
        acc_ref[...] = jnp.zeros_like(acc_ref)

    wg = wg_ref[...].astype(BF16)
    wu = wu_ref[...].astype(BF16)
    wd = wd_ref[...].astype(BF16)
    for r in range(x_ref.shape[0] // sub):
        rows = pl.ds(r * sub, sub)
        x = x_ref[rows, :]
        g = jnp.dot(x, wg, preferred_element_type=F32)
        u = jnp.dot(x, wu, preferred_element_type=F32)
        h = (_silu(g) * u).astype(BF16)
        acc_ref[rows, :] += jnp.dot(h, wd, preferred_element_type=F32)

    @pl.when(f == pl.num_programs(2) - 1)
    def _():
        o_ref[...] = acc_ref[...].astype(o_ref.dtype)


def moe_ffn(xe, w_gate, w_up, w_down, tm=2048, tf=256, sub=512):
    ne, m, d = xe.shape
    ff = w_gate.shape[2]
    tm = min(tm, m)
    return pl.pallas_call(
        functools.partial(_ffn_kernel, sub=min(sub, tm)),
        grid=(ne, m // tm, ff // tf),
        in_specs=[pl.BlockSpec((None, tm, d), lambda e, i, f: (e, i, 0)),
                  pl.BlockSpec((None, d, tf), lambda e, i, f: (e, 0, f)),
                  pl.BlockSpec((None, d, tf), lambda e, i, f: (e, 0, f)),
                  pl.BlockSpec((None, tf, d), lambda e, i, f: (e, f, 0))],
        out_specs=pl.BlockSpec((None, tm, d), lambda e, i, f: (e, i, 0)),
        out_shape=jax.ShapeDtypeStruct((ne, m, d), BF16),
        scratch_shapes=[pltpu.VMEM((tm, d), F32)],
        compiler_params=_cparams(("arbitrary", "arbitrary", "arbitrary"), VMEM_LIMIT_BYTES),
        name="moe_ffn",
    )(xe, w_gate, w_up, w_down)


def _combine_kernel(st_ref, nw_ref, pos_ref, gate_ref, ye_ref, x1_ref, g2_ref, fg_ref, o_ref, acc_ref,
                    *, win):
    b, j, nt = pl.program_id(0), pl.program_id(1), pl.num_programs(1)
    tt, ne = pos_ref.shape
    cap = ye_ref.shape[1]
    acc_ref[...] = jnp.zeros_like(acc_ref)
    pos = pos_ref[...]
    gate = gate_ref[...]
    col = lax.broadcasted_iota(I32, (tt, win), 1)

    def window(w, carry):
        acc = None
        for p in range(ne // 2):
            lhs, rhs = [], []
            for e in (2 * p, 2 * p + 1):
                first, start = _window(st_ref, (b * nt + j) * ne + e, w, win, cap)
                slot = col + start
                hit = (pos[:, e:e + 1] == slot) & (slot >= first)
                lhs.append(jnp.where(hit, gate[:, e:e + 1], 0.0).astype(BF16))
                rhs.append(ye_ref[e, pl.ds(start, win), :])
            part = jnp.dot(jnp.concatenate(lhs, axis=1), jnp.concatenate(rhs, axis=0),
                           preferred_element_type=F32)
            acc = part if acc is None else acc + part
        acc_ref[...] += acc
        return carry

    lax.fori_loop(0, nw_ref[b * nt + j], window, 0)
    xo = x1_ref[...] + g2_ref[...] * acc_ref[...]
    o_ref[...] = _rmsnorm(xo, fg_ref[...])


def moe_combine(pos_tm, gate_tm, base, ye, x1, g2, final_g, cap, tt=512, win=128):
    b, s, ne = pos_tm.shape
    d = x1.shape[2]
    starts, nwin = _slot_windows(base, cap, tt, win)
    tok = lambda n: pl.BlockSpec((None, tt, n), lambda i, j, st, nw: (i, j, 0))
    return pl.pallas_call(
        functools.partial(_combine_kernel, win=win),
        grid_spec=pltpu.PrefetchScalarGridSpec(
            num_scalar_prefetch=2,
            grid=(b, s // tt),
            in_specs=[tok(ne), tok(ne),
                      pl.BlockSpec((ne, cap, d), lambda i, j, st, nw: (0, i, 0)),
                      tok(d),
                      pl.BlockSpec((None, 1, d), lambda i, j, st, nw: (i, 0, 0)),
                      pl.BlockSpec((1, d), lambda i, j, st, nw: (0, 0))],
            out_specs=tok(d),
            scratch_shapes=[pltpu.VMEM((tt, d), F32)]),
        out_shape=jax.ShapeDtypeStruct((b, s, d), F32),
        compiler_params=_cparams(("arbitrary", "arbitrary"), VMEM_LIMIT_BYTES),
        name="moe_combine",
    )(starts, nwin, pos_tm, gate_tm, ye, x1, g2, final_g.reshape(1, d))


def _layer(x, ctx, mods, norm1_g, norm2_g, w_in, w_out, conv_w, conv_b, filt, hy_bias, s5p,
           s5_c_re, s5_c_im, s5_d, s5_glu_w, s5_glu_b, router_w, ex_w_gate, ex_w_up, ex_w_down,
           final_g):
    b, s, d = x.shape
    n_order, hw = hy_bias.shape
    hy_cols = (n_order + 1) * hw
    sw = w_in.shape[1] - hy_cols
    rows = s // GRID_W
    ne = router_w.shape[1]
    cap = CAPACITY_FACTOR * s // ne

    per_b = lambda k: mods[:b, k * d:(k + 1) * d].reshape(b, 1, d)
    ctx_v = lambda k: jnp.broadcast_to(mods[b:b + 1, k * d:(k + 1) * d].reshape(1, 1, d), (b, 1, d))
    sh1, sc1, g1, sh2, sc2, g2 = [per_b(k) for k in range(N_MOD)]

    w_in_bf = w_in.astype(BF16)
    (u_ctx,) = inproj(ctx, norm1_g, ctx_v(0), ctx_v(1), w_in_bf, 0, w_cols=(hy_cols // sw, sw))
    z_hy, u = inproj(x, norm1_g, sh1, sc1, w_in_bf, hy_cols, conv=(conv_w, conv_b, GRID_W))

    hraw = hyena_filter_mlp(s, *filt)
    kf = hyena_filter_spectra(hraw, n_order, hw)
    y1 = hyena_conv(z_hy, 0, z_hy, 1, kf, hy_bias, 0)
    hy = hyena_conv(y1, 0, z_hy, 2, kf, hy_bias, 1)

    s5w = s5_chunk_weights(*s5p, s5_c_re, s5_c_im, sw // LANES, S5_CHUNK)
    yf, yb = s5_scan(u_ctx, u, s5w, b)

    x1, h2, logits = mixer_tail(x, hy, yf, yb, u, s5_d, s5_glu_w, s5_glu_b, w_out, g1, norm2_g,
                                sh2, sc2, router_w)
    pos_em, pos_tm, gate_tm, base = route(logits, cap)
    xe = moe_gather(pos_em, h2, base, cap)
    ye = moe_ffn(xe, ex_w_gate, ex_w_up, ex_w_down)
    return moe_combine(pos_tm, gate_tm, base, ye, x1, g2, final_g, cap)


def kernel(x, c, ctx, c_ctx, mod_w, mod_b, norm1_g, norm2_g, w_in, w_out, conv_w, conv_b, hy_w1, hy_b1, hy_freq, hy_w2, hy_b2, hy_w3, hy_b3, hy_bias, s5_lam_re, s5_lam_im, s5_log_step, s5_b_re, s5_b_im, s5_c_re, s5_c_im, s5_d, s5_glu_w, s5_glu_b, router_w, ex_w_gate, ex_w_up, ex_w_down, final_g):
    depth = mod_w.shape[0]
    assert depth == 1, "context-token updates of non-final layers are not implemented"
    b, _, d = x.shape
    l = 0
    pad = (-(b + 1)) % SUBLANES
    cond = jnp.concatenate([c, c_ctx[None], jnp.zeros((pad, d), F32)], axis=0)
    mods = adaln_mods(cond, mod_w[l], mod_b[l])
    filt = (hy_w1[l], hy_b1[l], hy_freq[l], hy_w2[l], hy_b2[l], hy_w3[l], hy_b3[l])
    s5p = (s5_lam_re[l], s5_lam_im[l], s5_log_step[l], s5_b_re[l], s5_b_im[l])
    return _layer(x, ctx, mods, norm1_g[l], norm2_g[l], w_in[l], w_out[l], conv_w[l], conv_b[l],
                  filt, hy_bias[l], s5p, s5_c_re[l], s5_c_im[l], s5_d[l], s5_glu_w[l], s5_glu_b[l],
                  router_w[l], ex_w_gate[l], ex_w_up[l], ex_w_down[l], final_g)
```

```python
import functools
import math

import numpy as np
import jax
import jax.numpy as jnp
from jax import lax
from jax.experimental import pallas as pl
from jax.experimental.pallas import tpu as pltpu

F32 = jnp.float32
BF16 = jnp.bfloat16
I32 = jnp.int32
HIGHEST = lax.Precision.HIGHEST

SUBLANES = 8
LANES = 128
VMEM_LIMIT_BYTES = 58 * 1024 * 1024

GRID_W = 64
N_MOD = 6
NORM_EPS = 1e-6
POS_BANDS = 16
DECAY_FAST = 0.3
DECAY_SLOW = 1.5
DECAY_TARGET = 1e-2
CAPACITY_FACTOR = 2

DFT_N1 = 64
DFT_N2 = 128
DFT_K1 = DFT_N1 // 2 + 1
DFT_ROWS = 2 * DFT_K1

S5_CHUNK = 4


def _cparams(sem, vmem=None):
    return pltpu.CompilerParams(dimension_semantics=sem, vmem_limit_bytes=vmem)


def _silu(x):
    return x * (1.0 / (1.0 + jnp.exp(-x)))


def _rmsnorm(x, g):
    ms = jnp.mean(x * x, axis=-1, keepdims=True)
    return x * lax.rsqrt(ms + NORM_EPS) * g


def _adaln_kernel(c_ref, w_ref, b_ref, o_ref):
    s = _silu(c_ref[...])
    o_ref[...] = jnp.dot(s, w_ref[...], precision=HIGHEST, preferred_element_type=F32) + b_ref[...]


def adaln_mods(cond, mod_w, mod_b, tn=1536):
    rows, d = cond.shape
    n = mod_w.shape[1]
    return pl.pallas_call(
        _adaln_kernel,
        grid=(n // tn,),
        in_specs=[pl.BlockSpec((rows, d), lambda j: (0, 0)),
                  pl.BlockSpec((d, tn), lambda j: (0, j)),
                  pl.BlockSpec((1, tn), lambda j: (0, j))],
        out_specs=pl.BlockSpec((rows, tn), lambda j: (0, j)),
        out_shape=jax.ShapeDtypeStruct((rows, n), F32),
        compiler_params=_cparams(("arbitrary",)),
        name="adaln",
    )(cond, mod_w, mod_b.reshape(1, n))


def _short_conv(z, w_ref, b_ref, row_len):
    length = z.shape[0]
    pos = lax.broadcasted_iota(I32, (length, 1), 0) % row_len
    zm = jnp.where(pos == 0, 0.0, pltpu.roll(z, 1, 0))
    zp = jnp.where(pos == row_len - 1, 0.0, pltpu.roll(z, length - 1, 0))
    w = w_ref[...]
    return zm * w[0:1, :] + z * w[1:2, :] + zp * w[2:3, :] + b_ref[...]


@functools.lru_cache(maxsize=None)
def _perm_time_major(batch, tt):
    n = batch * tt
    p = np.zeros((n, n), np.float32)
    t, b = np.meshgrid(np.arange(tt), np.arange(batch), indexing='ij')
    p[(t * batch + b).ravel(), (b * tt + t).ravel()] = 1.0
    return jnp.asarray(p, dtype=BF16)


def _inproj_kernel(x_ref, g_ref, sh_ref, sc_ref, w_ref, perm_ref, *refs, n_tok, row_len):
    bsz, tt, d = x_ref.shape
    h = _rmsnorm(x_ref[...], g_ref[...])
    h = h * (1.0 + sc_ref[...]) + sh_ref[...]
    z = jnp.dot(h.reshape(bsz * tt, d).astype(BF16), w_ref[...], preferred_element_type=F32)
    if n_tok:
        cw_ref, cb_ref, tok_ref, tm_ref = refs
        tok_ref[...] = _short_conv(z[:, :n_tok], cw_ref, cb_ref, row_len).reshape(bsz, tt, n_tok)
    else:
        (tm_ref,) = refs
    u = z[:, n_tok:].astype(BF16)
    tm_ref[...] = jnp.dot(perm_ref[...], u, preferred_element_type=F32).astype(BF16)


def inproj(x, g, shift, scale, w_bf16, n_tok, conv=None, w_cols=None, tt=64):
    b, s, d = x.shape
    col, n = (0, w_bf16.shape[1]) if w_cols is None else w_cols
    perm = _perm_time_major(b, tt)
    const = lambda a: (pl.BlockSpec((d, n), lambda j: (0, col)) if a is w_bf16
                       else pl.BlockSpec(a.shape, lambda j: (0,) * a.ndim))
    args = [x, g.reshape(1, d), shift, scale, w_bf16, perm]
    out_specs = [pl.BlockSpec((tt * b, n - n_tok), lambda j: (j, 0))]
    out_shape = [jax.ShapeDtypeStruct((s * b, n - n_tok), BF16)]
    row_len = 1
    if n_tok:
        cw, cb, row_len = conv
        assert tt % row_len == 0
        args += [cw, cb.reshape(1, -1)]
        out_specs.insert(0, pl.BlockSpec((b, tt, n_tok), lambda j: (0, j, 0)))
        out_shape.insert(0, jax.ShapeDtypeStruct((b, s, n_tok), F32))
    return pl.pallas_call(
        functools.partial(_inproj_kernel, n_tok=n_tok, row_len=row_len),
        grid=(s // tt,),
        in_specs=[pl.BlockSpec((b, tt, d), lambda j: (0, j, 0))] + [const(a) for a in args[1:]],
        out_specs=out_specs,
        out_shape=out_shape,
        compiler_params=_cparams(("arbitrary",)),
        name="inproj",
    )(*args)


def _filt_mlp_kernel(w1t_ref, w1c_ref, w1s_ref, b1_ref, fr_ref, w2_ref, b2_ref, w3_ref, b3_ref,
                     o_ref, *, length, tl):
    i0 = pl.program_id(0) * tl
    idx = (lax.broadcasted_iota(I32, (1, tl), 1) + i0).astype(F32)
    t = idx / float(length - 1)
    omega = (2.0 * math.pi) * idx / float(length)
    fstep = ((POS_BANDS - 1) - 1e-4) / (POS_BANDS - 1)
    f = 1e-4 + lax.broadcasted_iota(I32, (POS_BANDS, 1), 0).astype(F32) * fstep
    arg = f * omega
    pre = (w1t_ref[...] * t
           + jnp.dot(w1c_ref[...], jnp.cos(arg), precision=HIGHEST, preferred_element_type=F32)
           - jnp.dot(w1s_ref[...], jnp.sin(arg), precision=HIGHEST, preferred_element_type=F32)
           + b1_ref[...])
    fr = fr_ref[...]
    h = jnp.sin(fr[:, 0:1] * pre)
    h = jnp.sin(fr[:, 1:2] * (jnp.dot(w2_ref[...], h, precision=HIGHEST,
                                      preferred_element_type=F32) + b2_ref[...]))
    o_ref[...] = jnp.dot(h.T, w3_ref[...], precision=HIGHEST, preferred_element_type=F32) + b3_ref[...]


def hyena_filter_mlp(length, w1, b1, freq, w2, b2, w3, b3, tl=512):
    fw = w1.shape[1]
    n = w3.shape[1]
    full = lambda shape: pl.BlockSpec(shape, lambda i: (0, 0))
    return pl.pallas_call(
        functools.partial(_filt_mlp_kernel, length=length, tl=tl),
        grid=(length // tl,),
        in_specs=[full((fw, 1)), full((fw, POS_BANDS)), full((fw, POS_BANDS)), full((fw, 1)),
                  full((fw, 2)), full((fw, fw)), full((fw, 1)), full((fw, n)), full((1, n))],
        out_specs=pl.BlockSpec((tl, n), lambda i: (i, 0)),
        out_shape=jax.ShapeDtypeStruct((length, n), F32),
        compiler_params=_cparams(("arbitrary",)),
        name="hyena_filter_mlp",
    )(w1[0:1].T, w1[1:1 + POS_BANDS].T, w1[1 + POS_BANDS:].T, b1.reshape(fw, 1), freq.T, w2.T,
      b2.reshape(fw, 1), w3, b3.reshape(1, n))


@functools.lru_cache(maxsize=None)
def _dft_tables():
    n1n, n2n, k1n = DFT_N1, DFT_N2, DFT_K1
    n = n1n * n2n
    half = n1n // 2
    k1 = np.arange(k1n)[:, None]
    n1 = np.arange(half)[None, :]
    th = 2.0 * np.pi * k1 * n1 / n1n
    f1 = np.zeros((DFT_ROWS, half))
    f1[0::2] = np.cos(th)
    f1[1::2] = -np.sin(th)
    wgt = np.where((k1 == 0) | (k1 == half), 1.0, 2.0)
    g1 = np.zeros((half, DFT_ROWS))
    g1[:, 0::2] = (wgt * np.cos(th)).T / n
    g1[:, 1::2] = (-wgt * np.sin(th)).T / n
    eye = np.eye(SUBLANES)
    fk = np.kron(f1, eye)
    gk = np.kron(g1, eye)
    k2 = np.arange(n2n)[:, None]
    n2 = np.arange(n2n)[None, :]
    f3 = np.zeros((k1n, 2 * n2n, 2 * n2n))
    for kk in range(k1n):
        ph = 2.0 * np.pi * n2 * (n1n * k2 + kk) / n
        tr, ti = np.cos(ph), -np.sin(ph)
        f3[kk, :n2n, :n2n] = tr
        f3[kk, :n2n, n2n:] = -ti
        f3[kk, n2n:, :n2n] = ti
        f3[kk, n2n:, n2n:] = tr
    g3 = np.transpose(f3, (0, 2, 1))
    to = lambda a: jnp.asarray(a, dtype=F32).astype(BF16)
    return to(fk), to(gk), to(f3), to(g3)


def _dft_stage1(src_ref, a_ref, fk_ref):
    half = DFT_N1 // 2

    def body(m, carry):
        sub = pl.ds(pl.multiple_of(m * SUBLANES, SUBLANES), SUBLANES)
        rows = [src_ref.at[pl.ds(DFT_N2 * n1, DFT_N2)][sub, :] for n1 in range(half)]
        rhs = jnp.concatenate(rows, axis=0).astype(BF16)
        out = jnp.dot(fk_ref[...], rhs, preferred_element_type=F32)
        for j in range(DFT_ROWS):
            a_ref.at[pl.ds(DFT_N2 * j, DFT_N2)][sub, :] = out[SUBLANES * j:SUBLANES * (j + 1)]
        return carry

    lax.fori_loop(0, DFT_N2 // SUBLANES, body, 0, unroll=2)


def _dft_stage3(a_ref, f3_ref, k1):
    r0 = pl.multiple_of(k1 * (2 * DFT_N2), 2 * DFT_N2)
    a = a_ref[pl.ds(r0, 2 * DFT_N2), :].astype(BF16)
    x = jnp.dot(f3_ref[k1], a, preferred_element_type=F32)
    return x[:DFT_N2], x[DFT_N2:]


def _filt_spec_kernel(hf_ref, hb_ref, fk_ref, f3_ref, o_ref, src_ref, af_ref, ab_ref, *, length):
    c = hf_ref.shape[1]
    cb = pl.program_id(1)
    hw = pl.num_programs(1) * c
    row = lax.broadcasted_iota(I32, (length, 1), 0)
    t = row.astype(F32) / float(length - 1)
    ch = (lax.broadcasted_iota(I32, (1, c), 1) + cb * c).astype(F32)
    d0 = math.log(DECAY_TARGET) / DECAY_FAST
    d1 = math.log(DECAY_TARGET) / DECAY_SLOW
    deltas = jnp.abs(d0 + ch * ((d1 - d0) / float(hw - 1)))
    decay = jnp.exp(-t * deltas)
    fwd = hf_ref[...] * decay
    bwd = jnp.where(row == 0, 0.0, hb_ref[...] * decay)
    inv = 1.0 / (jnp.sum(jnp.abs(fwd), axis=0, keepdims=True)
                 + jnp.sum(jnp.abs(bwd), axis=0, keepdims=True))
    src_ref[...] = fwd
    _dft_stage1(src_ref, af_ref, fk_ref)
    src_ref[...] = bwd
    _dft_stage1(src_ref, ab_ref, fk_ref)

    def body(k1, carry):
        fr, fi = _dft_stage3(af_ref, f3_ref, k1)
        br, bi = _dft_stage3(ab_ref, f3_ref, k1)
        o_ref[k1, 0] = ((fr + br) * inv).astype(o_ref.dtype)
        o_ref[k1, 1] = ((fi - bi) * inv).astype(o_ref.dtype)
        return carry

    lax.fori_loop(0, DFT_K1, body, 0, unroll=3)


def hyena_filter_spectra(hraw, n_order, width, c_blk=256):
    length = hraw.shape[0]
    assert 2 * length == DFT_N1 * DFT_N2
    fk, _, f3, _ = _dft_tables()
    ncb = width // c_blk
    return pl.pallas_call(
        functools.partial(_filt_spec_kernel, length=length),
        grid=(n_order, ncb),
        in_specs=[pl.BlockSpec((length, c_blk), lambda o, j: (0, o * 2 * ncb + j)),
                  pl.BlockSpec((length, c_blk), lambda o, j: (0, o * 2 * ncb + ncb + j)),
                  pl.BlockSpec(fk.shape, lambda o, j: (0, 0)),
                  pl.BlockSpec(f3.shape, lambda o, j: (0, 0, 0))],
        out_specs=pl.BlockSpec((None, DFT_K1, 2, DFT_N2, c_blk), lambda o, j: (o, 0, 0, 0, j)),
        out_shape=jax.ShapeDtypeStruct((n_order, DFT_K1, 2, DFT_N2, width), BF16),
        scratch_shapes=[pltpu.VMEM((length, c_blk), F32),
                        pltpu.VMEM((DFT_ROWS * DFT_N2, c_blk), F32),
                        pltpu.VMEM((DFT_ROWS * DFT_N2, c_blk), F32)],
        compiler_params=_cparams(("arbitrary", "arbitrary"), VMEM_LIMIT_BYTES),
        name="hyena_filter_spectrum",
    )(hraw, hraw, fk, f3)


def _hyena_conv_kernel(s_ref, m_ref, kf_ref, bias_ref, fk_ref, f3_ref, g3_ref, gk_ref, o_ref, a_ref,
                       *, group):
    half = DFT_N1 // 2
    blk_rows = 2 * DFT_N2
    _dft_stage1(s_ref, a_ref, fk_ref)

    def body3(i, carry):
        k1s = [i * group + q for q in range(group)]
        r0s = [pl.multiple_of(k1 * blk_rows, blk_rows) for k1 in k1s]
        blocks = [a_ref[pl.ds(r0, blk_rows), :].astype(BF16) for r0 in r0s]
        outs = []
        for k1, a in zip(k1s, blocks):
            x = jnp.dot(f3_ref[k1], a, preferred_element_type=F32)
            xr, xi = x[:DFT_N2], x[DFT_N2:]
            kr = kf_ref[k1, 0].astype(F32)
            ki = kf_ref[k1, 1].astype(F32)
            y = jnp.concatenate([xr * kr - xi * ki, xr * ki + xi * kr], axis=0).astype(BF16)
            outs.append(jnp.dot(g3_ref[k1], y, preferred_element_type=F32))
        for r0, o in zip(r0s, outs):
            a_ref[pl.ds(r0, blk_rows), :] = o
        return carry

    lax.fori_loop(0, DFT_K1 // group, body3, 0)
    bias = bias_ref[...]

    def body1(m, carry):
        sub = pl.ds(pl.multiple_of(m * SUBLANES, SUBLANES), SUBLANES)
        blk = [a_ref.at[pl.ds(DFT_N2 * j, DFT_N2)][sub, :] for j in range(DFT_ROWS)]
        rhs = jnp.concatenate(blk, axis=0).astype(BF16)
        out = jnp.dot(gk_ref[...], rhs, preferred_element_type=F32)
        for n1 in range(half):
            blk_n1 = pl.ds(DFT_N2 * n1, DFT_N2)
            conv = out[SUBLANES * n1:SUBLANES * (n1 + 1)]
            o_ref.at[blk_n1][sub, :] = m_ref.at[blk_n1][sub, :] * (
                conv + s_ref.at[blk_n1][sub, :] * bias)
        return carry

    lax.fori_loop(0, DFT_N2 // SUBLANES, body1, 0, unroll=2)


def hyena_conv(sig, sig_col, mul, mul_col, kf, bias, order, c_blk=256, group=11):
    b, length, _ = sig.shape
    width = kf.shape[-1]
    ncb = width // c_blk
    assert DFT_K1 % group == 0
    fk, gk, f3, g3 = _dft_tables()
    once = pl.Buffered(1)
    const2 = lambda a: pl.BlockSpec(a.shape, lambda j, i: (0, 0), pipeline_mode=once)
    const3 = lambda a: pl.BlockSpec(a.shape, lambda j, i: (0, 0, 0), pipeline_mode=once)
    return pl.pallas_call(
        functools.partial(_hyena_conv_kernel, group=group),
        grid=(ncb, b),
        in_specs=[pl.BlockSpec((None, length, c_blk), lambda j, i: (i, 0, sig_col * ncb + j)),
                  pl.BlockSpec((None, length, c_blk), lambda j, i: (i, 0, mul_col * ncb + j)),
                  pl.BlockSpec((None, DFT_K1, 2, DFT_N2, c_blk), lambda j, i: (order, 0, 0, 0, j),
                               pipeline_mode=once),
                  pl.BlockSpec((None, 1, c_blk), lambda j, i: (order, 0, j), pipeline_mode=once),
                  const2(fk), const3(f3), const3(g3), const2(gk)],
        out_specs=pl.BlockSpec((None, length, c_blk), lambda j, i: (i, 0, j)),
        out_shape=jax.ShapeDtypeStruct((b, length, width), F32),
        scratch_shapes=[pltpu.VMEM((DFT_ROWS * DFT_N2, c_blk), F32)],
        compiler_params=_cparams(("arbitrary", "arbitrary"), VMEM_LIMIT_BYTES),
        name="hyena_conv",
    )(sig, mul, kf, bias.reshape(-1, 1, width), fk, f3, g3, gk)


def _s5_powers_kernel(lr_ref, li_ref, dt_ref, vr_ref, vi_ref, or_ref, oi_ref, *, n_pow, zoh):
    lr, li, dt = lr_ref[...], li_ref[...], jnp.exp(dt_ref[...])
    mag = jnp.exp(lr * dt)
    ar = mag * jnp.cos(li * dt)
    ai = mag * jnp.sin(li * dt)
    vr, vi = vr_ref[...], vi_ref[...]
    if zoh:
        den = 1.0 / (lr * lr + li * li)
        qr = ((ar - 1.0) * lr + ai * li) * den
        qi = (ai * lr - (ar - 1.0) * li) * den
        vr, vi = qr * vr - qi * vi, qr * vi + qi * vr
    for j in range(n_pow):
        or_ref[j] = vr
        oi_ref[j] = vi
        vr, vi = ar * vr - ai * vi, ar * vi + ai * vr


def s5_powers(lam_re, lam_im, log_step, v_re, v_im, state_axis, n_pow, zoh):
    nd, g, a, b = v_re.shape
    expand = (lambda x: x[..., :, None]) if state_axis == 2 else (lambda x: x[..., None, :])
    rep = lambda x: jnp.broadcast_to(expand(x), v_re.shape).reshape(nd * g, a * b)
    dt = jnp.broadcast_to(log_step[:, :, None, None], v_re.shape).reshape(nd * g, a * b)
    flat = lambda x: x.reshape(nd * g, a * b)
    shp = jax.ShapeDtypeStruct((n_pow, nd * g, a * b), F32)
    o_r, o_i = pl.pallas_call(
        functools.partial(_s5_powers_kernel, n_pow=n_pow, zoh=zoh), out_shape=[shp, shp],
        name="s5_powers",
    )(rep(lam_re), rep(lam_im), dt, flat(v_re), flat(v_im))
    un = lambda x: x.reshape(n_pow, nd, g, a, b)
    return un(o_r), un(o_i)


def _s5_taps_kernel(cr_ref, ci_ref, br_ref, bi_ref, o_ref):
    n_pow, nk = br_ref.shape[0], br_ref.shape[1]
    for j in range(n_pow):
        for k in range(nk):
            o_ref[j, k] = (jnp.dot(cr_ref[k], br_ref[j, k], preferred_element_type=F32)
                           - jnp.dot(ci_ref[k], bi_ref[j, k], preferred_element_type=F32))


def s5_taps(c_re, c_im, bbp_r, bbp_i, nk):
    nd, g, h, p = c_re.shape
    n_pow = bbp_r.shape[0]
    gpk = g // nk
    eye = jnp.eye(gpk, dtype=F32)
    bdiag = lambda c: jnp.einsum('dkahp,ab->dkahbp', c.reshape(nd, nk, gpk, h, p), eye).reshape(
        nd, nk, gpk * h, gpk * p).astype(BF16)
    flat = lambda x: jnp.transpose(x, (1, 0, 2, 3, 4)).reshape(nd, n_pow, nk, gpk * p, h).astype(BF16)
    per_d = lambda a: pl.BlockSpec((None,) + a.shape[1:], lambda d: (d,) + (0,) * (a.ndim - 1))
    args = (bdiag(c_re), bdiag(c_im), flat(bbp_r), flat(bbp_i))
    shp = jax.ShapeDtypeStruct((nd, n_pow, nk, gpk * h, h), F32)
    out = pl.pallas_call(
        _s5_taps_kernel,
        grid=(nd,),
        in_specs=[per_d(a) for a in args],
        out_specs=per_d(shp),
        out_shape=shp,
        compiler_params=_cparams(("arbitrary",)),
        name="s5_taps",
    )(*args)
    return out.reshape(nd, n_pow, g, h, h)


def _s5_scan_kernel(ucf_ref, uf_ref, ucb_ref, ub_ref, qf_ref, qb_ref, pf_ref, pb_ref, mf_ref, mb_ref,
                    lam_ref, yf_ref, yb_ref, xf_ref, xb_ref, st_ref, *, nc, kpp, tlen):
    width = uf_ref.shape[1]
    batch = st_ref.shape[3]
    nk = xf_ref.shape[1]
    ck = width // nk
    rows = xf_ref.shape[2]
    cpg = rows // batch
    i = pl.program_id(0)
    is_ctx = i < nc

    @pl.when(i == 0)
    def _():
        st_ref[...] = jnp.zeros_like(st_ref)

    pick = lambda c_ref, l_ref: jnp.where(is_ctx, c_ref[...], l_ref[...]).astype(F32).reshape(
        cpg, tlen, batch, width)
    uf = pick(ucf_ref, uf_ref)
    ub = pick(ucb_ref, ub_ref)

    def scan(x_ref, d, reverse):
        for k0 in range(0, nk, kpp):
            ks = slice(k0, k0 + kpp)
            lr, li = lam_ref[d, 0, ks], lam_ref[d, 1, ks]

            def body(c, carry, ks=ks, lr=lr, li=li):
                sr, si = carry
                cc = (cpg - 1 - c) if reverse else c
                r = pl.ds(pl.multiple_of(cc * batch, batch), batch)
                qr, qi = x_ref[0, ks, r, :], x_ref[1, ks, r, :]
                x_ref[0, ks, r, :] = sr
                x_ref[1, ks, r, :] = si
                return lr * sr - li * si + qr, lr * si + li * sr + qi

            st_ref[d, 0, ks], st_ref[d, 1, ks] = lax.fori_loop(
                0, cpg, body, (st_ref[d, 0, ks], st_ref[d, 1, ks]))

    for u, q_ref, p_ref, m_ref, x_ref, y_ref, d in ((uf, qf_ref, pf_ref, mf_ref, xf_ref, yf_ref, 0),
                                                    (ub, qb_ref, pb_ref, mb_ref, xb_ref, yb_ref, 1)):
        uks = []
        for k in range(nk):
            uk = jnp.concatenate([u[:, t, :, k * ck:(k + 1) * ck].reshape(rows, ck)
                                  for t in range(tlen)], axis=1).astype(BF16)
            uks.append(uk)
            for ri in range(2):
                x_ref[ri, k] = jnp.dot(uk, q_ref[ri, k], preferred_element_type=F32)
        scan(x_ref, d, d == 1)
        for k in range(nk):
            yk = (jnp.dot(uks[k], m_ref[k], preferred_element_type=F32)
                  + jnp.dot(x_ref[0, k].astype(BF16), p_ref[0, k], preferred_element_type=F32)
                  + jnp.dot(x_ref[1, k].astype(BF16), p_ref[1, k], preferred_element_type=F32))
            y_ref[:, k * ck:(k + 1) * ck] = jnp.stack(
                [yk[:, t * ck:(t + 1) * ck].reshape(cpg, batch, ck) for t in range(tlen)],
                axis=1).reshape(cpg * tlen * batch, ck)


def s5_chunk_weights(lam_re, lam_im, log_step, b_re, b_im, c_re, c_im, nk, tlen):
    nd, g, p, h = b_re.shape
    gpk, sk = g // nk, g * p // nk
    bb_r, bb_i = s5_powers(lam_re, lam_im, log_step, b_re, b_im, 2, tlen, True)
    cl_r, cl_i = s5_powers(lam_re, lam_im, log_step, c_re, c_im, 3, tlen + 1, False)
    ones = jnp.ones((nd, g, p, 2), F32)
    pw_r, pw_i = s5_powers(lam_re, lam_im, log_step, ones, jnp.zeros_like(ones), 2, tlen + 1, False)
    taps = s5_taps(c_re, c_im, bb_r, bb_i, nk)
    dirs = np.arange(nd)[:, None]
    step = np.arange(tlen)[None, :]
    jq = np.where(dirs == 0, tlen - 1 - step, step)
    jp = np.where(dirs == 0, step + 1, tlen - step)
    lag = step[0][None, None, :] - step[0][None, :, None]
    lag = np.where(dirs[:, :, None] == 0, lag, -lag)
    ch = tlen * gpk * h
    rep_p = np.tile(np.eye(p, dtype=np.float32), (1, gpk))
    rep_h = np.einsum('tu,hi,b->thubi', np.eye(tlen), np.eye(h), np.ones(gpk)).reshape(
        tlen * h, ch).astype(np.float32)
    grp_ch = np.tile(np.repeat(np.arange(gpk), h), tlen)
    grp_st = np.repeat(np.arange(gpk), p)
    qsel = jnp.stack([bb_r, bb_i])[:, jq, dirs].reshape(2, nd, tlen, nk, gpk, p, h)
    qrows = jnp.transpose(qsel, (1, 0, 3, 2, 4, 6, 5)).reshape(nd, 2, nk, ch, p)
    q = jnp.where(grp_ch[:, None] == grp_st[None, :], qrows @ rep_p, 0.0)
    psel = jnp.stack([cl_r, -cl_i])[:, jp, dirs].reshape(2, nd, tlen, nk, gpk, h, p)
    prows = jnp.transpose(psel, (1, 0, 3, 4, 6, 2, 5)).reshape(nd, 2, nk, sk, tlen * h)
    pm = jnp.where(grp_st[:, None] == grp_ch[None, :], prows @ rep_h, 0.0)
    tsel = jnp.where((lag >= 0)[..., None, None, None], taps[dirs[:, :, None], np.maximum(lag, 0)], 0.0)
    tsel = tsel.reshape(nd, tlen, tlen, nk, gpk, h, h)
    trows = jnp.transpose(tsel, (0, 3, 1, 4, 6, 2, 5)).reshape(nd, nk, ch, tlen * h)
    m = jnp.where(grp_ch[:, None] == grp_ch[None, :], trows @ rep_h, 0.0)
    decay = jnp.stack([pw_r[tlen, ..., 0], pw_i[tlen, ..., 0]], axis=1).reshape(nd, 2, nk, 1, sk)
    return q.astype(BF16), pm.astype(BF16), m.astype(BF16), decay


def s5_scan(uc, u, weights, batch, steps=128, kpp=2):
    q, pm, m, decay = weights
    width = u.shape[1]
    nk, sk = q.shape[2], q.shape[4]
    tlen = m.shape[2] * nk // width
    cpg = steps // tlen
    rpc = steps * batch
    nc, nl = uc.shape[0] // rpc, u.shape[0] // rpc
    assert steps % tlen == 0 and uc.shape[0] % rpc == 0 and u.shape[0] % rpc == 0
    lam = jnp.broadcast_to(decay, (2, 2, nk, batch, sk))
    blk = lambda f: pl.BlockSpec((rpc, width), lambda i: (f(i), 0))
    lat_f = lambda i: jnp.maximum(i - nc, 0)
    lat_b = lambda i: jnp.clip(nl - 1 - i + nc, 0, nl - 1)
    once = pl.Buffered(1)
    par = lambda a, d: pl.BlockSpec((None,) + a.shape[1:], lambda i: (d,) + (0,) * (a.ndim - 1),
                                    pipeline_mode=once)
    shp = jax.ShapeDtypeStruct(u.shape, F32)
    return pl.pallas_call(
        functools.partial(_s5_scan_kernel, nc=nc, kpp=kpp, tlen=tlen),
        grid=(nc + nl,),
        in_specs=[blk(lambda i: jnp.minimum(i, nc - 1)), blk(lat_f),
                  blk(lambda i: jnp.maximum(nc - 1 - i, 0)), blk(lat_b),
                  par(q, 0), par(q, 1), par(pm, 0), par(pm, 1), par(m, 0), par(m, 1),
                  pl.BlockSpec(lam.shape, lambda i: (0, 0, 0, 0, 0), pipeline_mode=once)],
        out_specs=[blk(lat_f), blk(lat_b)],
        out_shape=[shp, shp],
        scratch_shapes=[pltpu.VMEM((2, nk, cpg * batch, sk), F32),
                        pltpu.VMEM((2, nk, cpg * batch, sk), F32),
                        pltpu.VMEM((2, 2, nk, batch, sk), F32)],
        compiler_params=_cparams(("arbitrary",), VMEM_LIMIT_BYTES),
        name="s5_scan",
    )(uc, u, uc, u, q, q, pm, pm, m, m, lam)


def _mixer_tail_kernel(x_ref, hy_ref, yf_ref, yb_ref, u_ref, d_ref, gw_ref, gb_ref, woh_ref, wos_ref,
                       g1_ref, n2_ref, sh2_ref, sc2_ref, rwt_ref, perm_ref, x1_ref, h2_ref, lg_ref):
    bsz, tt, d = x_ref.shape
    rows = bsz * tt
    y = yf_ref[...] + yb_ref[...] + d_ref[...] * u_ref[...].astype(F32)
    y = 0.5 * y * (1.0 + jnp.tanh(math.sqrt(2.0 / math.pi) * (y + 0.044715 * (y * y * y))))
    gate = jnp.dot(y.astype(BF16), gw_ref[...], preferred_element_type=F32) + gb_ref[...]
    s5 = (y * (1.0 / (1.0 + jnp.exp(-gate)))).astype(BF16)
    pt = perm_ref.shape[0] // bsz
    parts = [jnp.dot(perm_ref[...], s5[q * pt * bsz:(q + 1) * pt * bsz], preferred_element_type=F32
                     ).astype(BF16) for q in range(tt // pt)]
    s5 = jnp.concatenate([p[i * pt:(i + 1) * pt] for i in range(bsz) for p in parts], axis=0)
    hy = hy_ref[...].reshape(rows, hy_ref.shape[2]).astype(BF16)
    mix = (jnp.dot(hy, woh_ref[...], preferred_element_type=F32)
           + jnp.dot(s5, wos_ref[...], preferred_element_type=F32))
    x1 = x_ref[...] + g1_ref[...] * mix.reshape(bsz, tt, d)
    x1_ref[...] = x1
    h2 = _rmsnorm(x1, n2_ref[...]) * (1.0 + sc2_ref[...]) + sh2_ref[...]
    h2_hi = h2.astype(BF16)
    h2_ref[...] = h2_hi
    rw = rwt_ref[...]
    rw_hi = rw.astype(BF16)
    rw_lo = (rw - rw_hi.astype(F32)).astype(BF16)
    h2_hi = h2_hi.reshape(rows, d)
    h2_lo = (h2.reshape(rows, d) - h2_hi.astype(F32)).astype(BF16)
    nt = lambda a, b: lax.dot_general(a, b, (((1,), (1,)), ((), ())), preferred_element_type=F32)
    lg = nt(rw_hi, h2_hi) + (nt(rw_hi, h2_lo) + nt(rw_lo, h2_hi))
    for i in range(bsz):
        lg_ref[i] = lg[:, i * tt:(i + 1) * tt]


def mixer_tail(x, hy, yf, yb, u_tm, s5_d, glu_w, glu_b, w_out, g1, norm2_g, sh2, sc2, router_w,
               tt=128):
    b, s, d = x.shape
    hw = hy.shape[2]
    sw = u_tm.shape[1]
    ne = router_w.shape[1]
    perm = _perm_time_major(b, min(tt, 64)).T
    once = pl.Buffered(1)
    tok = lambda n: pl.BlockSpec((b, tt, n), lambda j: (0, j, 0))
    tmj = pl.BlockSpec((tt * b, sw), lambda j: (j, 0))
    const = lambda a: pl.BlockSpec(a.shape, lambda j: (0,) * a.ndim, pipeline_mode=once)
    consts = [s5_d.reshape(1, sw), glu_w.astype(BF16), glu_b.reshape(1, sw), w_out[:hw].astype(BF16),
              w_out[hw:].astype(BF16), g1, norm2_g.reshape(1, d), sh2, sc2, router_w.T, perm]
    return pl.pallas_call(
        _mixer_tail_kernel,
        grid=(s // tt,),
        in_specs=[tok(d), tok(hw), tmj, tmj, tmj] + [const(a) for a in consts],
        out_specs=[tok(d), tok(d), pl.BlockSpec((b, ne, tt), lambda j: (0, 0, j))],
        out_shape=[jax.ShapeDtypeStruct((b, s, d), F32), jax.ShapeDtypeStruct((b, s, d), BF16),
                   jax.ShapeDtypeStruct((b, ne, s), F32)],
        compiler_params=_cparams(("arbitrary",), VMEM_LIMIT_BYTES),
        name="mixer_tail",
    )(x, hy, yf, yb, u_tm, *consts)


def _lane_cumsum_exclusive(x):
    rows, s = x.shape
    ii = lax.broadcasted_iota(I32, (LANES, LANES), 0)
    jj = lax.broadcasted_iota(I32, (LANES, LANES), 1)
    tri = jnp.where(ii < jj, 1.0, 0.0).astype(BF16)
    carry = jnp.zeros((rows, 1), F32)
    out, base = [], []
    for blk in range(s // LANES):
        xb = x[:, blk * LANES:(blk + 1) * LANES]
        out.append(jnp.dot(xb.astype(BF16), tri, preferred_element_type=F32) + carry)
        base.append(carry)
        carry = carry + jnp.sum(xb, axis=1, keepdims=True)
    return jnp.concatenate(out, axis=1), jnp.concatenate(base, axis=1)


def _route_kernel(lg_ref, pos_em_ref, pos_tm_ref, gate_tm_ref, base_ref, aff_ref, *, cap):
    lg = lg_ref[...]
    ne, s = lg.shape
    ex = jnp.exp(lg - jnp.max(lg, axis=0, keepdims=True))
    aff_ref[...] = ex / jnp.sum(ex, axis=0, keepdims=True)
    aff = aff_ref[...]
    count_ge = lambda v, t: jnp.sum(jnp.where(v >= t, 1.0, 0.0), axis=1, keepdims=True)

    def coarse(i, tb):
        cand = tb | jnp.left_shift(jnp.int32(1), 30 - i)
        return jnp.where(count_ge(aff, pltpu.bitcast(cand, F32)) >= cap, cand, tb)

    tb = lax.fori_loop(0, 31, coarse, jnp.zeros((ne, 1), I32))
    t_hi = pltpu.bitcast(tb, F32)
    ulp = pltpu.bitcast(tb + 1, F32) - t_hi
    resid = aff - t_hi

    def fine(j, carry):
        c, step = carry
        cand = c + step
        return jnp.where(count_ge(resid, cand) >= cap, cand, c), step * 0.5

    t_lo, _ = lax.fori_loop(0, 12, fine, (jnp.zeros((ne, 1), F32), ulp * 0.5))
    gt = resid > t_lo
    eq = resid == t_lo
    need = cap - jnp.sum(jnp.where(gt, 1.0, 0.0), axis=1, keepdims=True)
    eq_rank, _ = _lane_cumsum_exclusive(jnp.where(eq, 1.0, 0.0))
    sel = gt | (eq & (eq_rank < need))
    pos, base = _lane_cumsum_exclusive(jnp.where(sel, 1.0, 0.0))
    posf = jnp.where(sel, pos + 1.0, 0.0)
    gate = jnp.where(sel, aff, 0.0)
    pos_em_ref[...] = posf.astype(I32) - 1
    base_ref[...] = base.astype(I32)
    hi = jnp.floor(posf * (1.0 / 16.0))
    lo = posf - 16.0 * hi
    g1 = gate.astype(BF16)
    r1 = gate - g1.astype(F32)
    g2 = r1.astype(BF16)
    g3 = (r1 - g2.astype(F32)).astype(BF16)
    ii = lax.broadcasted_iota(I32, (LANES, LANES), 0)
    jj = lax.broadcasted_iota(I32, (LANES, LANES), 1)
    eye = jnp.where(ii == jj, 1.0, 0.0).astype(BF16)
    tr = lambda v: lax.dot_general(eye, v, (((1,), (1,)), ((), ())), preferred_element_type=F32)
    for blk in range(s // LANES):
        sl = slice(blk * LANES, (blk + 1) * LANES)
        pos_tm_ref[sl, :] = (16.0 * tr(hi[:, sl].astype(BF16)) + tr(lo[:, sl].astype(BF16))).astype(I32) - 1
        gate_tm_ref[sl, :] = tr(g1[:, sl]) + (tr(g2[:, sl]) + tr(g3[:, sl]))


def route(logits, cap):
    b, ne, s = logits.shape
    nb = s // LANES
    return pl.pallas_call(
        functools.partial(_route_kernel, cap=cap),
        grid=(b,),
        in_specs=[pl.BlockSpec((None, ne, s), lambda i: (i, 0, 0))],
        out_specs=[pl.BlockSpec((None, ne, s), lambda i: (i, 0, 0)),
                   pl.BlockSpec((None, s, ne), lambda i: (i, 0, 0)),
                   pl.BlockSpec((None, s, ne), lambda i: (i, 0, 0)),
                   pl.BlockSpec((None, ne, nb), lambda i: (i, 0, 0))],
        out_shape=[jax.ShapeDtypeStruct((b, ne, s), I32), jax.ShapeDtypeStruct((b, s, ne), I32),
                   jax.ShapeDtypeStruct((b, s, ne), F32), jax.ShapeDtypeStruct((b, ne, nb), I32)],
        scratch_shapes=[pltpu.VMEM((ne, s), F32)],
        compiler_params=_cparams(("arbitrary",)),
        name="route",
    )(logits)


SLOT_ALIGN = 16


def _slot_windows(base, cap, chunk, win):
    lo = base[:, :, ::chunk // LANES]
    hi = jnp.concatenate([lo[:, :, 1:], jnp.full_like(lo[:, :, :1], cap)], axis=2)
    start = (lo // SLOT_ALIGN) * SLOT_ALIGN
    nwin = jnp.max((hi - start + win - 1) // win, axis=1)
    return jnp.transpose(start, (0, 2, 1)).reshape(-1), nwin.reshape(-1)


def _window(st_ref, idx, w, win, cap):
    first = st_ref[idx] + w * win
    return first, pl.multiple_of(jnp.minimum(first, cap - win), SLOT_ALIGN)


def _gather_kernel(st_ref, nw_ref, pos_ref, h_ref, o_ref, *, win):
    b, j, nch = pl.program_id(0), pl.program_id(1), pl.num_programs(1)
    ne, cap, _ = o_ref.shape
    tk = h_ref.shape[0]

    @pl.when(j == 0)
    def _():
        o_ref[...] = jnp.zeros_like(o_ref)

    pos = pos_ref[...]
    h = h_ref[...]
    row = lax.broadcasted_iota(I32, (win, tk), 0)

    def window(w, carry):
        starts, lhs = [], []
        for e in range(ne):
            first, start = _window(st_ref, (b * nch + j) * ne + e, w, win, cap)
            slot = row + start
            hit = (pos[e:e + 1, :] == slot) & (slot >= first)
            lhs.append(jnp.where(hit, 1.0, 0.0).astype(BF16))
            starts.append(start)
        res = jnp.dot(jnp.concatenate(lhs, axis=0), h, preferred_element_type=F32)
        for e, start in enumerate(starts):
            o_ref[e, pl.ds(start, win), :] += res[e * win:(e + 1) * win].astype(o_ref.dtype)
        return carry

    lax.fori_loop(0, nw_ref[b * nch + j], window, 0)


def moe_gather(pos_em, h2, base, cap, tk=256, win=64):
    b, ne, s = pos_em.shape
    d = h2.shape[2]
    starts, nwin = _slot_windows(base, cap, tk, win)
    return pl.pallas_call(
        functools.partial(_gather_kernel, win=win),
        grid_spec=pltpu.PrefetchScalarGridSpec(
            num_scalar_prefetch=2,
            grid=(b, s // tk),
            in_specs=[pl.BlockSpec((None, ne, tk), lambda i, j, st, nw: (i, 0, j)),
                      pl.BlockSpec((None, tk, d), lambda i, j, st, nw: (i, j, 0))],
            out_specs=pl.BlockSpec((ne, cap, d), lambda i, j, st, nw: (0, i, 0))),
        out_shape=jax.ShapeDtypeStruct((ne, b * cap, d), BF16),
        compiler_params=_cparams(("arbitrary", "arbitrary"), VMEM_LIMIT_BYTES),
        name="moe_gather",
    )(starts, nwin, pos_em, h2)


def _ffn_kernel(x_ref, wg_ref, wu_ref, wd_ref, o_ref, acc_ref, *, sub):
    f = pl.program_id(2)

    @pl.when(f == 0)
    def _():
        acc_ref[...] = jnp.zeros_like(acc_ref)

    wg = wg_ref[...].astype(BF16)
    wu = wu_ref[...].astype(BF16)
    wd = wd_ref[...].astype(BF16)
    for r in range(x_ref.shape[0] // sub):
        rows = pl.ds(r * sub, sub)
        x = x_ref[rows, :]
        g = jnp.dot(x, wg, preferred_element_type=F32)
        u = jnp.dot(x, wu, preferred_element_type=F32)
        h = (_silu(g) * u).astype(BF16)
        acc_ref[rows, :] += jnp.dot(h, wd, preferred_element_type=F32)

    @pl.when(f == pl.num_programs(2) - 1)
    def _():
        o_ref[...] = acc_ref[...].astype(o_ref.dtype)


def moe_ffn(xe, w_gate, w_up, w_down, tm=2048, tf=256, sub=512):
    ne, m, d = xe.shape
    ff = w_gate.shape[2]
    tm = min(tm, m)
    return pl.pallas_call(
        functools.partial(_ffn_kernel, sub=min(sub, tm)),
        grid=(ne, m // tm, ff // tf),
        in_specs=[pl.BlockSpec((None, tm, d), lambda e, i, f: (e, i, 0)),
                  pl.BlockSpec((None, d, tf), lambda e, i, f: (e, 0, f)),
                  pl.BlockSpec((None, d, tf), lambda e, i, f: (e, 0, f)),
                  pl.BlockSpec((None, tf, d), lambda e, i, f: (e, f, 0))],
        out_specs=pl.BlockSpec((None, tm, d), lambda e, i, f: (e, i, 0)),
        out_shape=jax.ShapeDtypeStruct((ne, m, d), BF16),
        scratch_shapes=[pltpu.VMEM((tm, d), F32)],
        compiler_params=_cparams(("arbitrary", "arbitrary", "arbitrary"), VMEM_LIMIT_BYTES),
        name="moe_ffn",
    )(xe, w_gate, w_up, w_down)


def _combine_kernel(st_ref, nw_ref, pos_ref, gate_ref, ye_ref, x1_ref, g2_ref, fg_ref, o_ref, acc_ref,
                    *, win):
    b, j, nt = pl.program_id(0), pl.program_id(1), pl.num_programs(1)
    tt, ne = pos_ref.shape
    cap = ye_ref.shape[1]
    acc_ref[...] = jnp.zeros_like(acc_ref)
    pos = pos_ref[...]
    gate = gate_ref[...]
    col = lax.broadcasted_iota(I32, (tt, win), 1)

    def window(w, carry):
        acc = None
        for p in range(ne // 2):
            lhs, rhs = [], []
            for e in (2 * p, 2 * p + 1):
                first, start = _window(st_ref, (b * nt + j) * ne + e, w, win, cap)
                slot = col + start
                hit = (pos[:, e:e + 1] == slot) & (slot >= first)
                lhs.append(jnp.where(hit, gate[:, e:e + 1], 0.0).astype(BF16))
                rhs.append(ye_ref[e, pl.ds(start, win), :])
            part = jnp.dot(jnp.concatenate(lhs, axis=1), jnp.concatenate(rhs, axis=0),
                           preferred_element_type=F32)
            acc = part if acc is None else acc + part
        acc_ref[...] += acc
        return carry

    lax.fori_loop(0, nw_ref[b * nt + j], window, 0)
    xo = x1_ref[...] + g2_ref[...] * acc_ref[...]
    o_ref[...] = _rmsnorm(xo, fg_ref[...])


def moe_combine(pos_tm, gate_tm, base, ye, x1, g2, final_g, cap, tt=512, win=128):
    b, s, ne = pos_tm.shape
    d = x1.shape[2]
    starts, nwin = _slot_windows(base, cap, tt, win)
    tok = lambda n: pl.BlockSpec((None, tt, n), lambda i, j, st, nw: (i, j, 0))
    return pl.pallas_call(
        functools.partial(_combine_kernel, win=win),
        grid_spec=pltpu.PrefetchScalarGridSpec(
            num_scalar_prefetch=2,
            grid=(b, s // tt),
            in_specs=[tok(ne), tok(ne),
                      pl.BlockSpec((ne, cap, d), lambda i, j, st, nw: (0, i, 0)),
                      tok(d),
                      pl.BlockSpec((None, 1, d), lambda i, j, st, nw: (i, 0, 0)),
                      pl.BlockSpec((1, d), lambda i, j, st, nw: (0, 0))],
            out_specs=tok(d),
            scratch_shapes=[pltpu.VMEM((tt, d), F32)]),
        out_shape=jax.ShapeDtypeStruct((b, s, d), F32),
        compiler_params=_cparams(("arbitrary", "arbitrary"), VMEM_LIMIT_BYTES),
        name="moe_combine",
    )(starts, nwin, pos_tm, gate_tm, ye, x1, g2, final_g.reshape(1, d))


def _layer(x, ctx, mods, norm1_g, norm2_g, w_in, w_out, conv_w, conv_b, filt, hy_bias, s5p,
           s5_c_re, s5_c_im, s5_d, s5_glu_w, s5_glu_b, router_w, ex_w_gate, ex_w_up, ex_w_down,
           final_g):
    b, s, d = x.shape
    n_order, hw = hy_bias.shape
    hy_cols = (n_order + 1) * hw
    sw = w_in.shape[1] - hy_cols
    rows = s // GRID_W
    ne = router_w.shape[1]
    cap = CAPACITY_FACTOR * s // ne

    per_b = lambda k: mods[:b, k * d:(k + 1) * d].reshape(b, 1, d)
    ctx_v = lambda k: jnp.broadcast_to(mods[b:b + 1, k * d:(k + 1) * d].reshape(1, 1, d), (b, 1, d))
    sh1, sc1, g1, sh2, sc2, g2 = [per_b(k) for k in range(N_MOD)]

    w_in_bf = w_in.astype(BF16)
    (u_ctx,) = inproj(ctx, norm1_g, ctx_v(0), ctx_v(1), w_in_bf, 0, w_cols=(hy_cols // sw, sw))
    z_hy, u = inproj(x, norm1_g, sh1, sc1, w_in_bf, hy_cols, conv=(conv_w, conv_b, GRID_W))

    hraw = hyena_filter_mlp(s, *filt)
    kf = hyena_filter_spectra(hraw, n_order, hw)
    y1 = hyena_conv(z_hy, 0, z_hy, 1, kf, hy_bias, 0)
    hy = hyena_conv(y1, 0, z_hy, 2, kf, hy_bias, 1)

    s5w = s5_chunk_weights(*s5p, s5_c_re, s5_c_im, sw // LANES, S5_CHUNK)
    yf, yb = s5_scan(u_ctx, u, s5w, b)

    x1, h2, logits = mixer_tail(x, hy, yf, yb, u, s5_d, s5_glu_w, s5_glu_b, w_out, g1, norm2_g,
                                sh2, sc2, router_w)
    pos_em, pos_tm, gate_tm, base = route(logits, cap)
    xe = moe_gather(pos_em, h2, base, cap)
    ye = moe_ffn(xe, ex_w_gate, ex_w_up, ex_w_down)
    return moe_combine(pos_tm, gate_tm, base, ye, x1, g2, final_g, cap)


def kernel(x, c, ctx, c_ctx, mod_w, mod_b, norm1_g, norm2_g, w_in, w_out, conv_w, conv_b, hy_w1, hy_b1, hy_freq, hy_w2, hy_b2, hy_w3, hy_b3, hy_bias, s5_lam_re, s5_lam_im, s5_log_step, s5_b_re, s5_b_im, s5_c_re, s5_c_im, s5_d, s5_glu_w, s5_glu_b, router_w, ex_w_gate, ex_w_up, ex_w_down, final_g):
    depth = mod_w.shape[0]
    assert depth == 1, "context-token updates of non-final layers are not implemented"
    b, _, d = x.shape
    l = 0
    pad = (-(b + 1)) % SUBLANES
    cond = jnp.concatenate([c, c_ctx[None], jnp.zeros((pad, d), F32)], axis=0)
    mods = adaln_mods(cond, mod_w[l], mod_b[l])
    filt = (hy_w1[l], hy_b1[l], hy_freq[l], hy_w2[l], hy_b2[l], hy_w3[l], hy_b3[l])
    s5p = (s5_lam_re[l], s5_lam_im[l], s5_log_step[l], s5_b_re[l], s5_b_im[l])
    return _layer(x, ctx, mods, norm1_g[l], norm2_g[l], w_in[l], w_out[l], conv_w[l], conv_b[l],
                  filt, hy_bias[l], s5p, s5_c_re[l], s5_c_im[l], s5_d[l], s5_glu_w[l], s5_glu_b[l],
                  router_w[l], ex_w_gate[l], ex_w_up[l], ex_w_down[l], final_g)
```
